```python
import math
import jax, jax.numpy as jnp
from jax import lax
import numpy as np

D_MODEL = 2048
BATCH = 1
SEQ = 16384
DEPTH = 2

GRID_W = 64
CTX_LEN = 256
Q_BLOCK = 128
ROPE_THETA = 10000.0
RMS_EPS = 1e-6
N_MOD = 6

DIFF_HEADS = 8
DIFF_DIM = 64
GQA_HEADS = 8
GQA_KV_HEADS = 2
GQA_DIM = 128
A_QK = DIFF_HEADS * 2 * DIFF_DIM
A_V = DIFF_HEADS * 2 * DIFF_DIM
B_Q = GQA_HEADS * GQA_DIM
B_KV = GQA_KV_HEADS * GQA_DIM
ATTN_SPLITS = (A_QK, 2 * A_QK, 2 * A_QK + A_V, 2 * A_QK + A_V + B_Q, 2 * A_QK + A_V + B_Q + B_KV)
ATTN_IN = 2 * A_QK + A_V + B_Q + 2 * B_KV
ATTN_OUT = A_V + B_Q

MLSTM_HEADS = 8
MLSTM_DIM = D_MODEL // MLSTM_HEADS
MLSTM_INNER = MLSTM_HEADS * MLSTM_DIM
MLSTM_CHUNK = 128
CONV_WIDTH = 3
MLSTM_IN = 4 * MLSTM_INNER + 4 * MLSTM_HEADS

N_EXPERTS = 16
EXPERT_FF = D_MODEL
EC_CAPACITY_FACTOR = 2

kernel_name = "hybrid_diffattn_gqa_mlstm_ecmoe_dit"


def rms_norm(x, g):
    xf = x.astype(jnp.float32)
    y = xf * lax.rsqrt(jnp.mean(xf * xf, axis=-1, keepdims=True) + RMS_EPS)
    return (y * g.astype(jnp.float32)).astype(x.dtype)


def modulate(h, shift, scale):
    return h * (1 + scale) + shift


def axial_rope_tables(n_tokens, dim):
    n_rows = n_tokens // GRID_W
    rows = jnp.repeat(jnp.arange(n_rows, dtype=jnp.float32), GRID_W)
    cols = jnp.tile(jnp.arange(GRID_W, dtype=jnp.float32), n_rows)
    axis_dim = dim // 2
    inv_freq = ROPE_THETA ** (-jnp.arange(0, axis_dim, 2, dtype=jnp.float32) / axis_dim)
    ang = jnp.concatenate([rows[:, None] * inv_freq, cols[:, None] * inv_freq], axis=-1)
    return jnp.cos(ang), jnp.sin(ang)


def apply_rope(x, cos, sin):
    half = x.shape[-1] // 2
    c = cos[None, :, None, :].astype(x.dtype)
    s = sin[None, :, None, :].astype(x.dtype)
    x1, x2 = x[..., :half], x[..., half:]
    return jnp.concatenate([x1 * c - x2 * s, x2 * c + x1 * s], axis=-1)


def map_query_blocks(fn, *qs):
    b, n = qs[0].shape[:2]
    nb = n // Q_BLOCK
    blocks = tuple(jnp.moveaxis(q.reshape(b, nb, Q_BLOCK, *q.shape[2:]), 1, 0) for q in qs)
    out = lax.map(lambda blk: fn(*blk), blocks)
    out = jnp.moveaxis(out, 0, 1)
    return out.reshape(b, n, *out.shape[3:])


def diff_attention(q1, q2, k1, k2, v, lam):
    scale = DIFF_DIM ** -0.5

    def block(q1b, q2b):
        s1 = jnp.einsum('bqhd,bkhd->bhqk', q1b, k1).astype(jnp.float32) * scale
        s2 = jnp.einsum('bqhd,bkhd->bhqk', q2b, k2).astype(jnp.float32) * scale
        w = jax.nn.softmax(s1, axis=-1) - lam * jax.nn.softmax(s2, axis=-1)
        return jnp.einsum('bhqk,bkhe->bqhe', w.astype(v.dtype), v)

    return map_query_blocks(block, q1, q2)


def gqa_attention(q, k, v):
    b, n = q.shape[:2]
    scale = GQA_DIM ** -0.5
    qg = q.reshape(b, n, GQA_KV_HEADS, GQA_HEADS // GQA_KV_HEADS, GQA_DIM)

    def block(qb):
        s = jnp.einsum('bqhgd,bkhd->bhgqk', qb, k).astype(jnp.float32) * scale
        p = jax.nn.softmax(s, axis=-1)
        return jnp.einsum('bhgqk,bkhd->bqhgd', p.astype(v.dtype), v)

    return map_query_blocks(block, qg).reshape(b, n, GQA_HEADS, GQA_DIM)


def attention_mixer(h, hc, p, layer, last):
    def project(u):
        b, n = u.shape[:2]
        qa, ka, va, qb, kb, vb = jnp.split(u @ p['w_in'], ATTN_SPLITS, axis=-1)
        qa = qa.reshape(b, n, DIFF_HEADS, 2, DIFF_DIM)
        ka = ka.reshape(b, n, DIFF_HEADS, 2, DIFF_DIM)
        return {
            'q1': qa[..., 0, :], 'q2': qa[..., 1, :], 'k1': ka[..., 0, :], 'k2': ka[..., 1, :],
            'va': va.reshape(b, n, DIFF_HEADS, 2 * DIFF_DIM),
            'qb': rms_norm(qb.reshape(b, n, GQA_HEADS, GQA_DIM), p['g_q']),
            'kb': rms_norm(kb.reshape(b, n, GQA_KV_HEADS, GQA_DIM), p['g_k']),
            'vb': vb.reshape(b, n, GQA_KV_HEADS, GQA_DIM),
        }

    lat = project(h)
    cx = project(hc)
    cos_a, sin_a = axial_rope_tables(h.shape[1], DIFF_DIM)
    cos_b, sin_b = axial_rope_tables(h.shape[1], GQA_DIM)
    for name in ('q1', 'q2', 'k1', 'k2'):
        lat[name] = apply_rope(lat[name], cos_a, sin_a)
    for name in ('qb', 'kb'):
        lat[name] = apply_rope(lat[name], cos_b, sin_b)

    lam_init = 0.8 - 0.6 * math.exp(-0.3 * layer)
    lam = (jnp.exp(jnp.sum(p['lambda_q1'].astype(jnp.float32) * p['lambda_k1'].astype(jnp.float32)))
           - jnp.exp(jnp.sum(p['lambda_q2'].astype(jnp.float32) * p['lambda_k2'].astype(jnp.float32)))
           + lam_init)

    def with_ctx(name):
        return jnp.concatenate([cx[name], lat[name]], axis=1)

    def merge(o_a, o_b):
        b, n = o_a.shape[:2]
        o_a = rms_norm(o_a, p['g_subln']) * (1 - lam_init)
        o = jnp.concatenate([o_a.reshape(b, n, A_V), o_b.reshape(b, n, B_Q)], axis=-1)
        return o @ p['w_out']

    y_lat = merge(
        diff_attention(lat['q1'], lat['q2'], with_ctx('k1'), with_ctx('k2'), with_ctx('va'), lam),
        gqa_attention(lat['qb'], with_ctx('kb'), with_ctx('vb')))
    if last:
        return y_lat, None
    y_ctx = merge(
        diff_attention(cx['q1'], cx['q2'], cx['k1'], cx['k2'], cx['va'], lam),
        gqa_attention(cx['qb'], cx['kb'], cx['vb']))
    return y_lat, y_ctx


def short_conv(x, w, b):
    k, ch = w.shape
    y = lax.conv_general_dilated(x, w[:, None, :].astype(x.dtype), window_strides=(1,),
                                 padding=[(k // 2, k // 2)], dimension_numbers=('NWC', 'WIO', 'NWC'),
                                 feature_group_count=ch)
    return y + b.astype(x.dtype)


def mlstm_chunk_step(carry, xs):
    C, n, m = carry
    q, k, v, ig, lf = xs
    L = q.shape[2]
    b = jnp.cumsum(lf, axis=-1)
    scan_order = jnp.tril(jnp.ones((L, L), dtype=bool))
    log_d = jnp.where(scan_order, b[..., :, None] - b[..., None, :] + ig[..., None, :], -jnp.inf)
    log_inter = b + m[..., None]
    m_t = jnp.maximum(log_inter, jnp.max(log_d, axis=-1))
    w_intra = jnp.exp(log_d - m_t[..., None]) * jnp.einsum('bhtd,bhsd->bhts', q, k)
    w_inter = jnp.exp(log_inter - m_t)
    num = (w_inter[..., None] * jnp.einsum('bhtd,bhed->bhte', q, C)
           + jnp.einsum('bhts,bhse->bhte', w_intra, v))
    den = w_inter * jnp.einsum('bhtd,bhd->bht', q, n) + jnp.sum(w_intra, axis=-1)
    h = num / jnp.maximum(jnp.abs(den), jnp.exp(-m_t))[..., None]
    b_end = b[..., -1]
    log_w = b_end[..., None] - b + ig
    m_new = jnp.maximum(b_end + m, jnp.max(log_w, axis=-1))
    w = jnp.exp(log_w - m_new[..., None])
    decay = jnp.exp(b_end + m - m_new)
    C_new = decay[..., None, None] * C + jnp.einsum('bhs,bhse,bhsd->bhed', w, v, k)
    n_new = decay[..., None] * n + jnp.einsum('bhs,bhsd->bhd', w, k)
    return (C_new, n_new, m_new), h


def mlstm_scan(q, k, v, ig, lf, state):
    b, hh, n = q.shape[:3]
    nc = n // MLSTM_CHUNK

    def chunks(a):
        return jnp.moveaxis(a.reshape(b, hh, nc, MLSTM_CHUNK, *a.shape[3:]), 2, 0)

    state, h = lax.scan(mlstm_chunk_step, state, tuple(chunks(a) for a in (q, k, v, ig, lf)))
    h = jnp.moveaxis(h, 0, 2).reshape(b, hh, n, h.shape[-1])
    return h, state


def mlstm_mixer(h, hc, p, last):
    def project(u):
        b, n = u.shape[:2]
        qk, v, o, g = jnp.split(u @ p['w_in'], (2 * MLSTM_INNER, 3 * MLSTM_INNER, 4 * MLSTM_INNER), axis=-1)
        qk = jax.nn.silu(short_conv(qk, p['w_conv'], p['b_conv']))
        q, k = jnp.split(qk, 2, axis=-1)

        def heads(a):
            return a.reshape(b, n, MLSTM_HEADS, MLSTM_DIM).transpose(0, 2, 1, 3).astype(jnp.float32)

        g = (g + p['b_gate']).astype(jnp.float32).reshape(b, n, 2, 2, MLSTM_HEADS).transpose(2, 3, 0, 4, 1)
        gates = [(g[d, 0], jax.nn.log_sigmoid(g[d, 1])) for d in range(2)]
        return heads(q), heads(k) * (MLSTM_DIM ** -0.5), heads(v), o, gates

    q, k, v, o, gates = project(h)
    qc, kc, vc, oc, gates_c = project(hc)
    b = h.shape[0]
    zero = (jnp.zeros((b, MLSTM_HEADS, MLSTM_DIM, MLSTM_DIM), jnp.float32),
            jnp.zeros((b, MLSTM_HEADS, MLSTM_DIM), jnp.float32),
            jnp.zeros((b, MLSTM_HEADS), jnp.float32))

    def flip(a):
        return jnp.flip(a, axis=2)

    hf_c, st_f = mlstm_scan(qc, kc, vc, gates_c[0][0], gates_c[0][1], zero)
    hf, _ = mlstm_scan(q, k, v, gates[0][0], gates[0][1], st_f)
    hb_c, st_b = mlstm_scan(flip(qc), flip(kc), flip(vc), flip(gates_c[1][0]), flip(gates_c[1][1]), zero)
    hb, _ = mlstm_scan(flip(q), flip(k), flip(v), flip(gates[1][0]), flip(gates[1][1]), st_b)

    def readout(hsum, og):
        bb, n = og.shape[:2]
        hn = rms_norm(hsum.transpose(0, 2, 1, 3), p['g_head'].reshape(MLSTM_HEADS, MLSTM_DIM))
        y = jax.nn.sigmoid(og.astype(jnp.float32)) * hn.reshape(bb, n, MLSTM_INNER)
        return y.astype(og.dtype) @ p['w_out']

    y_lat = readout(hf + flip(hb), o)
    if last:
        return y_lat, None
    return y_lat, readout(hf_c + flip(hb_c), oc)


def expert_choice_ffn(u, p):
    b, n, d = u.shape
    cap = EC_CAPACITY_FACTOR * n // N_EXPERTS
    aff = jax.nn.softmax((u @ p['w_router']).astype(jnp.float32), axis=-1)
    gate, idx = lax.top_k(jnp.swapaxes(aff, 1, 2), cap)
    xe = jax.vmap(lambda ub, ib: ub[ib])(u, idx)
    hid = (jax.nn.silu(jnp.einsum('becd,edf->becf', xe, p['w_gate']))
           * jnp.einsum('becd,edf->becf', xe, p['w_up']))
    ye = jnp.einsum('becf,efd->becd', hid, p['w_down']) * gate[..., None].astype(u.dtype)
    return jax.vmap(lambda yb, ib: jnp.zeros((n, d), u.dtype).at[ib.reshape(-1)].add(yb.reshape(-1, d)))(ye, idx)


def setup_inputs(seed: int = 0) -> dict:
    key = jax.random.key(seed)
    keys = jax.random.split(key, 64)
    counter = [0]

    def nk():
        counter[0] += 1
        return keys[counter[0] - 1]

    def nrm(shape, scale=1.0):
        return jax.random.normal(nk(), shape, jnp.float32) * scale

    def gain(n):
        return 1.0 + 0.1 * nrm((n,))

    D = D_MODEL
    inp = {}
    inp['x'] = nrm((BATCH, SEQ, D))
    inp['c'] = nrm((BATCH, D))
    inp['ctx'] = nrm((BATCH, CTX_LEN, D))
    inp['c_ctx'] = nrm((D,))
    inp['w_mod_0'] = nrm((D, N_MOD * D), 0.5 * D ** -0.5)
    inp['b_mod_0'] = nrm((N_MOD * D,), 0.02)
    inp['g_pre_mix_0'] = gain(D)
    inp['g_post_mix_0'] = gain(D)
    inp['w_in_0'] = nrm((D, ATTN_IN), D ** -0.5)
    inp['lambda_q1_0'] = nrm((DIFF_DIM,), 0.1)
    inp['lambda_k1_0'] = nrm((DIFF_DIM,), 0.1)
    inp['lambda_q2_0'] = nrm((DIFF_DIM,), 0.1)
    inp['lambda_k2_0'] = nrm((DIFF_DIM,), 0.1)
    inp['g_subln_0'] = gain(2 * DIFF_DIM)
    inp['g_q_0'] = gain(GQA_DIM)
    inp['g_k_0'] = gain(GQA_DIM)
    inp['w_out_0'] = nrm((ATTN_OUT, D), ATTN_OUT ** -0.5)
    inp['g_pre_ffn_0'] = gain(D)
    inp['g_post_ffn_0'] = gain(D)
    inp['w_router_0'] = nrm((D, N_EXPERTS), D ** -0.5)
    inp['w_gate_0'] = nrm((N_EXPERTS, D, EXPERT_FF), D ** -0.5)
    inp['w_up_0'] = nrm((N_EXPERTS, D, EXPERT_FF), D ** -0.5)
    inp['w_down_0'] = nrm((N_EXPERTS, EXPERT_FF, D), EXPERT_FF ** -0.5)
    inp['w_mod_1'] = nrm((D, N_MOD * D), 0.5 * D ** -0.5)
    inp['b_mod_1'] = nrm((N_MOD * D,), 0.02)
    inp['g_pre_mix_1'] = gain(D)
    inp['g_post_mix_1'] = gain(D)
    inp['w_in_1'] = nrm((D, MLSTM_IN), D ** -0.5)
    ig_bias = nrm((2, 1, MLSTM_HEADS), 0.1)
    fg_bias = jax.random.uniform(nk(), (2, 1, MLSTM_HEADS), jnp.float32, 3.0, 6.0)
    inp['b_gate_1'] = jnp.concatenate([ig_bias, fg_bias], axis=1).reshape(-1)
    inp['w_conv_1'] = nrm((CONV_WIDTH, 2 * MLSTM_INNER), CONV_WIDTH ** -0.5)
    inp['b_conv_1'] = nrm((2 * MLSTM_INNER,), 0.01)
    inp['g_head_1'] = gain(MLSTM_INNER)
    inp['w_out_1'] = nrm((MLSTM_INNER, D), MLSTM_INNER ** -0.5)
    inp['g_pre_ffn_1'] = gain(D)
    inp['g_post_ffn_1'] = gain(D)
    inp['w_router_1'] = nrm((D, N_EXPERTS), D ** -0.5)
    inp['w_gate_1'] = nrm((N_EXPERTS, D, EXPERT_FF), D ** -0.5)
    inp['w_up_1'] = nrm((N_EXPERTS, D, EXPERT_FF), D ** -0.5)
    inp['w_down_1'] = nrm((N_EXPERTS, EXPERT_FF, D), EXPERT_FF ** -0.5)
    return inp


def reference(x, c, ctx, c_ctx,
              w_mod_0, b_mod_0, g_pre_mix_0, g_post_mix_0, w_in_0, lambda_q1_0, lambda_k1_0,
              lambda_q2_0, lambda_k2_0, g_subln_0, g_q_0, g_k_0, w_out_0, g_pre_ffn_0, g_post_ffn_0,
              w_router_0, w_gate_0, w_up_0, w_down_0,
              w_mod_1, b_mod_1, g_pre_mix_1, g_post_mix_1, w_in_1, b_gate_1, w_conv_1, b_conv_1,
              g_head_1, w_out_1, g_pre_ffn_1, g_post_ffn_1, w_router_1, w_gate_1, w_up_1, w_down_1):
    layers = [
        dict(w_mod=w_mod_0, b_mod=b_mod_0, g_pre_mix=g_pre_mix_0, g_post_mix=g_post_mix_0, w_in=w_in_0,
             lambda_q1=lambda_q1_0, lambda_k1=lambda_k1_0, lambda_q2=lambda_q2_0, lambda_k2=lambda_k2_0,
             g_subln=g_subln_0, g_q=g_q_0, g_k=g_k_0, w_out=w_out_0, g_pre_ffn=g_pre_ffn_0,
             g_post_ffn=g_post_ffn_0, w_router=w_router_0, w_gate=w_gate_0, w_up=w_up_0, w_down=w_down_0),
        dict(w_mod=w_mod_1, b_mod=b_mod_1, g_pre_mix=g_pre_mix_1, g_post_mix=g_post_mix_1, w_in=w_in_1,
             b_gate=b_gate_1, w_conv=w_conv_1, b_conv=b_conv_1, g_head=g_head_1, w_out=w_out_1,
             g_pre_ffn=g_pre_ffn_1, g_post_ffn=g_post_ffn_1, w_router=w_router_1, w_gate=w_gate_1,
             w_up=w_up_1, w_down=w_down_1),
    ]
    b, _, d = x.shape
    xc = ctx
    for layer in range(DEPTH):
        p = layers[layer]
        last = layer == DEPTH - 1
        mod_l = (jax.nn.silu(c) @ p['w_mod'] + p['b_mod']).reshape(b, N_MOD, 1, d)
        mod_c = (jax.nn.silu(c_ctx) @ p['w_mod'] + p['b_mod']).reshape(1, N_MOD, 1, d)

        hl = modulate(rms_norm(x, p['g_pre_mix']), mod_l[:, 0], mod_l[:, 1])
        hc = modulate(rms_norm(xc, p['g_pre_mix']), mod_c[:, 0], mod_c[:, 1])
        if layer % 2 == 0:
            y_lat, y_ctx = attention_mixer(hl, hc, p, layer, last)
        else:
            y_lat, y_ctx = mlstm_mixer(hl, hc, p, last)
        x = x + mod_l[:, 2] * rms_norm(y_lat, p['g_post_mix'])

        hl = modulate(rms_norm(x, p['g_pre_ffn']), mod_l[:, 3], mod_l[:, 4])
        x = x + mod_l[:, 5] * rms_norm(expert_choice_ffn(hl, p), p['g_post_ffn'])
        if not last:
            xc = xc + mod_c[:, 2] * rms_norm(y_ctx, p['g_post_mix'])
            hc = modulate(rms_norm(xc, p['g_pre_ffn']), mod_c[:, 3], mod_c[:, 4])
            xc = xc + mod_c[:, 5] * rms_norm(expert_choice_ffn(hc, p), p['g_post_ffn'])
    return x
```

```python
import functools
import math

import jax
import jax.numpy as jnp
from jax import lax
from jax.experimental import pallas as pl
from jax.experimental.pallas import tpu as pltpu

F32 = jnp.float32
BF16 = jnp.bfloat16
HIGHEST = lax.Precision.HIGHEST

GRID_W = 64
ROPE_THETA = 10000.0
RMS_EPS = 1e-6
N_MOD = 6
DIFF_HEADS = 8
DIFF_DIM = 64
GQA_HEADS = 8
GQA_KV_HEADS = 2
GQA_DIM = 128
MLSTM_HEADS = 8
MLSTM_DIM = 256
MLSTM_CHUNK = 128
N_EXPERTS = 16
EC_CAPACITY_FACTOR = 2

LANES = 128
ROW_TILE = 512
TOPK_BLOCKS = 128
VMEM_LIMIT = 56 * 1024 * 1024


def _cparams(sem):
    return pltpu.CompilerParams(dimension_semantics=sem, vmem_limit_bytes=VMEM_LIMIT)


def _rms(x, g):
    return x * lax.rsqrt(jnp.mean(x * x, axis=-1, keepdims=True) + RMS_EPS) * g


def _dot(a, b):
    return jnp.dot(a, b, preferred_element_type=F32)


def _dot_nt(a, b, precision=None):
    return lax.dot_general(a, b, (((1,), (1,)), ((), ())), precision=precision, preferred_element_type=F32)


def _mod_kernel(c_ref, w_ref, b_ref, o_ref):
    c = c_ref[...]
    s = c * jax.nn.sigmoid(c)
    o_ref[...] = lax.dot_general(s, w_ref[...], (((1,), (0,)), ((), ())), precision=HIGHEST,
                                 preferred_element_type=F32) + b_ref[...]


def _modulation(cc, w_mod, b_mod):
    d, n = w_mod.shape
    tn = 1024
    return pl.pallas_call(
        _mod_kernel,
        grid=(n // tn,),
        in_specs=[pl.BlockSpec((8, d), lambda j: (0, 0)),
                  pl.BlockSpec((d, tn), lambda j: (0, j)),
                  pl.BlockSpec((1, tn), lambda j: (0, j))],
        out_specs=pl.BlockSpec((8, tn), lambda j: (0, j)),
        out_shape=jax.ShapeDtypeStruct((8, n), F32),
        compiler_params=_cparams(("arbitrary",)),
        name="modulation",
    )(cc, w_mod, b_mod.reshape(1, n))


def _mod_rows(mod_ref, r, k, d):
    return mod_ref[pl.ds(r, 1), k * d:(k + 1) * d]


def _premix(x_ref, mod_ref, g_ref, r, d):
    x = x_ref[...]
    return _rms(x, g_ref[...]) * (1.0 + _mod_rows(mod_ref, r, 1, d)) + _mod_rows(mod_ref, r, 0, d)


def _rope_a(x, ca, sa, sb):
    return x * ca + pltpu.roll(x, LANES - DIFF_DIM // 2, 1) * sa + pltpu.roll(x, DIFF_DIM // 2, 1) * sb


def _rope_b(x, cb, sb):
    return x * cb + pltpu.roll(x, GQA_DIM // 2, 1) * sb


def _proj0_kernel(x_ref, mod_ref, gpre_ref, w_ref, ca_ref, saa_ref, sab_ref, cb_ref, sbb_ref, gq_ref, gk_ref,
                  o_ref, h_scr, *, n_lat_tiles, d, tn):
    i = pl.program_id(0)
    j = pl.program_id(1)
    ng = tn // LANES

    @pl.when(j == 0)
    def _():
        r = jnp.where(i >= n_lat_tiles, 1, 0)
        h_scr[...] = _premix(x_ref, mod_ref, gpre_ref, r, d).astype(BF16)

    acc = _dot(h_scr[...], w_ref[...])

    def groups(fn):
        for g in range(ng):
            o_ref[:, g * LANES:(g + 1) * LANES] = fn(acc[:, g * LANES:(g + 1) * LANES], g).astype(BF16)

    qa_tiles = DIFF_HEADS * 2 * DIFF_DIM // tn
    qb0 = 3 * qa_tiles
    qb_tiles = GQA_HEADS * GQA_DIM // tn

    @pl.when(j < 2 * qa_tiles)
    def _():
        sc = jnp.where(j < qa_tiles, DIFF_DIM ** -0.5, 1.0)
        groups(lambda a, g: _rope_a(a, ca_ref[...], saa_ref[...], sab_ref[...]) * sc)

    @pl.when(jnp.logical_and(j >= 2 * qa_tiles, j < qb0))
    def _():
        groups(lambda a, g: a)

    @pl.when(jnp.logical_and(j >= qb0, j < qb0 + qb_tiles))
    def _():
        groups(lambda a, g: _rope_b(_rms(a, gq_ref[...]), cb_ref[...], sbb_ref[...]) * (GQA_DIM ** -0.5))

    @pl.when(j == qb0 + qb_tiles)
    def _():
        def fn(a, g):
            if g < GQA_KV_HEADS:
                return _rope_b(_rms(a, gk_ref[...]), cb_ref[...], sbb_ref[...])
            return a
        groups(fn)


def _proj0(xs, mod, g_pre, w_in, tabs, g_q, g_k, n_lat_tiles):
    s_pad, d = xs.shape
    n_out = w_in.shape[1]
    tm, tn = ROW_TILE, 512
    assert GQA_KV_HEADS * GQA_DIM * 2 == tn
    row = lambda i, j: (i, 0)
    const = lambda i, j: (0, 0)
    tab_spec = pl.BlockSpec((tm, LANES), row)
    return pl.pallas_call(
        functools.partial(_proj0_kernel, n_lat_tiles=n_lat_tiles, d=d, tn=tn),
        grid=(s_pad // tm, n_out // tn),
        in_specs=[pl.BlockSpec((tm, d), row),
                  pl.BlockSpec((8, N_MOD * d), const),
                  pl.BlockSpec((1, d), const),
                  pl.BlockSpec((d, tn), lambda i, j: (0, j)),
                  tab_spec, tab_spec, tab_spec, tab_spec, tab_spec,
                  pl.BlockSpec((1, LANES), const),
                  pl.BlockSpec((1, LANES), const)],
        out_specs=pl.BlockSpec((tm, tn), lambda i, j: (i, j)),
        out_shape=jax.ShapeDtypeStruct((s_pad, n_out), BF16),
        scratch_shapes=[pltpu.VMEM((tm, d), BF16)],
        compiler_params=_cparams(("parallel", "arbitrary")),
        name="proj0",
    )(xs, mod, g_pre.reshape(1, d), w_in, *tabs, g_q.reshape(1, LANES), g_k.reshape(1, LANES))


def _rope_tables(n_lat, n_ctx, s_pad):
    rows = jnp.repeat(jnp.arange(n_lat // GRID_W, dtype=F32), GRID_W)
    cols = jnp.tile(jnp.arange(GRID_W, dtype=F32), n_lat // GRID_W)

    def angles(dim):
        axis_dim = dim // 2
        inv_freq = ROPE_THETA ** (-jnp.arange(0, axis_dim, 2, dtype=F32) / axis_dim)
        ang = jnp.concatenate([rows[:, None] * inv_freq, cols[:, None] * inv_freq], axis=-1)
        return jnp.cos(ang), jnp.sin(ang)

    def pad(t, fill):
        return jnp.concatenate([t, jnp.full((s_pad - n_lat, LANES), fill, F32)], axis=0)

    cos_a, sin_a = angles(DIFF_DIM)
    zero = jnp.zeros_like(sin_a)
    ca = jnp.tile(cos_a, (1, 4))
    saa = jnp.tile(jnp.concatenate([-sin_a, zero], axis=1), (1, 2))
    sab = jnp.tile(jnp.concatenate([zero, sin_a], axis=1), (1, 2))
    cos_b, sin_b = angles(GQA_DIM)
    cb = jnp.tile(cos_b, (1, 2))
    sbb = jnp.concatenate([-sin_b, sin_b], axis=1)
    return pad(ca, 1.0), pad(saa, 0.0), pad(sab, 0.0), pad(cb, 1.0), pad(sbb, 0.0)


def _key_mask(ik, tk, is_ctx_q, n_lat, s_valid):
    kpos = ik * tk + lax.broadcasted_iota(jnp.int32, (1, tk), 1)
    lo = jnp.where(is_ctx_q, n_lat, 0)
    return jnp.logical_and(kpos >= lo, kpos < s_valid)


def _flash_update(s, v, m_ref, l_ref, a_ref):
    m_prev = m_ref[...]
    m_new = jnp.maximum(m_prev, jnp.max(s, axis=1, keepdims=True))
    p = jnp.exp(s - m_new)
    alpha = jnp.exp(m_prev - m_new)
    l_ref[...] = alpha * l_ref[...] + jnp.sum(p, axis=1, keepdims=True)
    a_ref[...] = alpha * a_ref[...] + _dot(p.astype(BF16), v)
    m_ref[...] = m_new


def _attn_diff_kernel(q_ref, k_ref, v_ref, lam_ref, gsub_ref, o_ref, m1, l1, a1, m2, l2, a2,
                      *, n_lat_tiles, tk, n_lat, s_valid, lam_init):
    iq = pl.program_id(1)
    ik = pl.program_id(2)
    nk = pl.num_programs(2)
    is_ctx_q = iq >= n_lat_tiles

    @pl.when(ik == 0)
    def _():
        for m, l, a in ((m1, l1, a1), (m2, l2, a2)):
            m[...] = jnp.full(m.shape, -jnp.inf, F32)
            l[...] = jnp.zeros(l.shape, F32)
            a[...] = jnp.zeros(a.shape, F32)

    @pl.when(jnp.logical_or(jnp.logical_not(is_ctx_q), ik == nk - 1))
    def _():
        valid = _key_mask(ik, tk, is_ctx_q, n_lat, s_valid)
        q = q_ref[...]
        k = k_ref[...]
        v = v_ref[...]
        for mi, (m, l, a) in enumerate(((m1, l1, a1), (m2, l2, a2))):
            s = _dot_nt(q[:, mi * DIFF_DIM:(mi + 1) * DIFF_DIM], k[:, mi * DIFF_DIM:(mi + 1) * DIFF_DIM])
            _flash_update(jnp.where(valid, s, -jnp.inf), v, m, l, a)

    @pl.when(ik == nk - 1)
    def _():
        lam = (jnp.exp(jnp.sum(lam_ref[0:1, :] * lam_ref[1:2, :], axis=1, keepdims=True))
               - jnp.exp(jnp.sum(lam_ref[2:3, :] * lam_ref[3:4, :], axis=1, keepdims=True)) + lam_init)
        o = a1[...] / l1[...] - lam * (a2[...] / l2[...])
        o_ref[...] = (_rms(o, gsub_ref[...]) * (1.0 - lam_init)).astype(BF16)


def _attn_diff(p0, lam4, g_subln, n_lat_tiles, n_lat, s_valid, lam_init):
    s_pad = p0.shape[0]
    tq = tk = ROW_TILE
    nq = s_pad // tq
    hd = 2 * DIFF_DIM
    k0 = DIFF_HEADS
    v0 = 2 * DIFF_HEADS

    def kv_blk(iq, ik):
        return jnp.where(iq >= n_lat_tiles, nq - 1, ik)

    return pl.pallas_call(
        functools.partial(_attn_diff_kernel, n_lat_tiles=n_lat_tiles, tk=tk, n_lat=n_lat, s_valid=s_valid,
                          lam_init=lam_init),
        grid=(DIFF_HEADS, nq, nq),
        in_specs=[pl.BlockSpec((tq, hd), lambda h, iq, ik: (iq, h)),
                  pl.BlockSpec((tk, hd), lambda h, iq, ik: (kv_blk(iq, ik), k0 + h)),
                  pl.BlockSpec((tk, hd), lambda h, iq, ik: (kv_blk(iq, ik), v0 + h)),
                  pl.BlockSpec((4, DIFF_DIM), lambda h, iq, ik: (0, 0)),
                  pl.BlockSpec((1, hd), lambda h, iq, ik: (0, 0))],
        out_specs=pl.BlockSpec((tq, hd), lambda h, iq, ik: (iq, h)),
        out_shape=jax.ShapeDtypeStruct((s_pad, DIFF_HEADS * hd), BF16),
        scratch_shapes=[pltpu.VMEM((tq, 1), F32), pltpu.VMEM((tq, 1), F32), pltpu.VMEM((tq, hd), F32),
                        pltpu.VMEM((tq, 1), F32), pltpu.VMEM((tq, 1), F32), pltpu.VMEM((tq, hd), F32)],
        compiler_params=_cparams(("parallel", "parallel", "arbitrary")),
        name="attn_diff",
    )(p0, p0, p0, lam4, g_subln.reshape(1, hd))


def _attn_gqa_kernel(q_ref, k_ref, v_ref, o_ref, m_scr, l_scr, a_scr, *, n_lat_tiles, tk, n_lat, s_valid, group):
    iq = pl.program_id(1)
    ik = pl.program_id(2)
    nk = pl.num_programs(2)
    is_ctx_q = iq >= n_lat_tiles

    @pl.when(ik == 0)
    def _():
        m_scr[...] = jnp.full(m_scr.shape, -jnp.inf, F32)
        l_scr[...] = jnp.zeros(l_scr.shape, F32)
        a_scr[...] = jnp.zeros(a_scr.shape, F32)

    @pl.when(jnp.logical_or(jnp.logical_not(is_ctx_q), ik == nk - 1))
    def _():
        valid = _key_mask(ik, tk, is_ctx_q, n_lat, s_valid)
        k = k_ref[...]
        v = v_ref[...]
        for g in range(group):
            s = _dot_nt(q_ref[:, g * GQA_DIM:(g + 1) * GQA_DIM], k)
            _flash_update(jnp.where(valid, s, -jnp.inf), v, m_scr.at[g], l_scr.at[g], a_scr.at[g])

    @pl.when(ik == nk - 1)
    def _():
        for g in range(group):
            o_ref[:, g * GQA_DIM:(g + 1) * GQA_DIM] = (a_scr[g] / l_scr[g]).astype(BF16)


def _attn_gqa(p0, n_lat_tiles, n_lat, s_valid):
    s_pad = p0.shape[0]
    tq = tk = ROW_TILE
    nq = s_pad // tq
    group = GQA_HEADS // GQA_KV_HEADS
    qw = group * GQA_DIM
    q0 = 3 * DIFF_HEADS * 2 * DIFF_DIM // qw
    k0 = (3 * DIFF_HEADS * 2 * DIFF_DIM + GQA_HEADS * GQA_DIM) // GQA_DIM
    v0 = k0 + GQA_KV_HEADS

    def kv_blk(iq, ik):
        return jnp.where(iq >= n_lat_tiles, nq - 1, ik)

    return pl.pallas_call(
        functools.partial(_attn_gqa_kernel, n_lat_tiles=n_lat_tiles, tk=tk, n_lat=n_lat, s_valid=s_valid, group=group),
        grid=(GQA_KV_HEADS, nq, nq),
        in_specs=[pl.BlockSpec((tq, qw), lambda h, iq, ik: (iq, q0 + h)),
                  pl.BlockSpec((tk, GQA_DIM), lambda h, iq, ik: (kv_blk(iq, ik), k0 + h)),
                  pl.BlockSpec((tk, GQA_DIM), lambda h, iq, ik: (kv_blk(iq, ik), v0 + h))],
        out_specs=pl.BlockSpec((tq, qw), lambda h, iq, ik: (iq, h)),
        out_shape=jax.ShapeDtypeStruct((s_pad, GQA_HEADS * GQA_DIM), BF16),
        scratch_shapes=[pltpu.VMEM((group, tq, 1), F32), pltpu.VMEM((group, tq, 1), F32),
                        pltpu.VMEM((group, tq, GQA_DIM), F32)],
        compiler_params=_cparams(("parallel", "parallel", "arbitrary")),
        name="attn_gqa",
    )(p0, p0, p0)


def _post_mix(y, x_ref, mod_ref, r, gpost_ref, gffn_ref, wr_ref, xo_ref, hl_ref, aff_ref, d):
    xn = x_ref[...] + _mod_rows(mod_ref, r, 2, d) * _rms(y, gpost_ref[...])
    xo_ref[...] = xn
    hl = _rms(xn, gffn_ref[...]) * (1.0 + _mod_rows(mod_ref, r, 4, d)) + _mod_rows(mod_ref, r, 3, d)
    hl_ref[...] = hl
    logits = _dot_nt(wr_ref[...], hl, precision=HIGHEST)
    e = jnp.exp(logits - jnp.max(logits, axis=0, keepdims=True))
    aff_ref[...] = e / jnp.sum(e, axis=0, keepdims=True)


def _out0_kernel(oa_ref, ob_ref, x_ref, mod_ref, w_ref, gpost_ref, gffn_ref, wr_ref, xo_ref, hl_ref, aff_ref,
                 *, n_lat_tiles, d):
    r = jnp.where(pl.program_id(0) >= n_lat_tiles, 1, 0)
    ka = oa_ref.shape[1]
    y = _dot(oa_ref[...], w_ref[0:ka, :]) + _dot(ob_ref[...], w_ref[ka:, :])
    _post_mix(y, x_ref, mod_ref, r, gpost_ref, gffn_ref, wr_ref, xo_ref, hl_ref, aff_ref, d)


def _out1_kernel(hf_ref, hb_ref, og_ref, ghead_ref, x_ref, mod_ref, w_ref, gpost_ref, gffn_ref, wr_ref,
                 xo_ref, hl_ref, aff_ref, y_scr, *, n_lat_tiles, d):
    r = jnp.where(pl.program_id(0) >= n_lat_tiles, 1, 0)
    for h in range(MLSTM_HEADS):
        sl = slice(h * MLSTM_DIM, (h + 1) * MLSTM_DIM)
        hs = hf_ref[:, sl].astype(F32) + hb_ref[:, sl].astype(F32)
        hn = _rms(hs, ghead_ref[:, sl])
        y_scr[:, sl] = (jax.nn.sigmoid(og_ref[:, sl].astype(F32)) * hn).astype(BF16)
    y = _dot(y_scr[...], w_ref[...])
    _post_mix(y, x_ref, mod_ref, r, gpost_ref, gffn_ref, wr_ref, xo_ref, hl_ref, aff_ref, d)


def _mixer_out(kernel_fn, mix_inputs, mix_specs, x, mod, w_out, g_post, g_ffn, w_router_t, n_rows, n_lat_tiles_tm,
               scratch, name):
    d = x.shape[1]
    tm = 256
    row = lambda i: (i, 0)
    const = lambda i: (0, 0)
    ne = w_router_t.shape[0]
    return pl.pallas_call(
        functools.partial(kernel_fn, n_lat_tiles=n_lat_tiles_tm, d=d),
        grid=(n_rows // tm,),
        in_specs=mix_specs(tm) + [pl.BlockSpec((tm, d), row),
                                  pl.BlockSpec((8, N_MOD * d), const),
                                  pl.BlockSpec(w_out.shape, const),
                                  pl.BlockSpec((1, d), const),
                                  pl.BlockSpec((1, d), const),
                                  pl.BlockSpec((ne, d), const)],
        out_specs=[pl.BlockSpec((tm, d), row), pl.BlockSpec((tm, d), row), pl.BlockSpec((ne, tm), lambda i: (0, i))],
        out_shape=[jax.ShapeDtypeStruct((n_rows, d), F32), jax.ShapeDtypeStruct((n_rows, d), F32),
                   jax.ShapeDtypeStruct((ne, n_rows), F32)],
        scratch_shapes=scratch(tm),
        compiler_params=_cparams(("parallel",)),
        name=name,
    )(*mix_inputs, x, mod, w_out, g_post.reshape(1, d), g_ffn.reshape(1, d), w_router_t)


def _incl_cumsum(maskf, upper, strict_lower):
    mb = maskf.astype(BF16)
    local = _dot(mb, upper)
    start = jnp.sum(_dot(strict_lower, mb), axis=1, keepdims=True)
    return local, start


def _topk_kernel(aff_ref, idx_ref, gate_ref, dest_ref, cnt_ref, cnt_acc, *, cap, dest_stride, row_off):
    e = pl.program_id(0)
    ne = pl.num_programs(0)
    nb = aff_ref.shape[1]
    a = aff_ref[0]
    bits = pltpu.bitcast(a, jnp.int32)

    thr = jnp.zeros((1, 1), jnp.int32)
    for bit in range(30, -1, -1):
        cand = thr | (1 << bit)
        c = jnp.sum(jnp.sum(jnp.where(bits >= cand, 1.0, 0.0), axis=1, keepdims=True), axis=0, keepdims=True)
        thr = jnp.where(c >= cap, cand, thr)

    r_i = lax.broadcasted_iota(jnp.int32, (LANES, LANES), 0)
    c_i = lax.broadcasted_iota(jnp.int32, (LANES, LANES), 1)
    upper = jnp.where(r_i <= c_i, 1.0, 0.0).astype(BF16)
    rb = lax.broadcasted_iota(jnp.int32, (nb, nb), 0)
    cb = lax.broadcasted_iota(jnp.int32, (nb, nb), 1)
    strict_lower = jnp.where(cb < rb, 1.0, 0.0).astype(BF16)

    gt = bits > thr
    eq = bits == thr
    n_gt = jnp.sum(jnp.sum(jnp.where(gt, 1.0, 0.0), axis=1, keepdims=True), axis=0, keepdims=True)
    eq_local, eq_start = _incl_cumsum(jnp.where(eq, 1.0, 0.0), upper, strict_lower)
    sel = jnp.logical_or(gt, jnp.logical_and(eq, eq_local + eq_start <= cap - n_gt))
    self_ = jnp.where(sel, 1.0, 0.0)
    sel_local, sel_start = _incl_cumsum(self_, upper, strict_lower)
    sel_tot = jnp.sum(self_, axis=1, keepdims=True)

    @pl.when(e == 0)
    def _():
        cnt_acc[...] = jnp.zeros(cnt_acc.shape, F32)

    earlier = cnt_acc[...]
    cnt_acc[...] = earlier + self_

    lane = lax.broadcasted_iota(jnp.int32, (nb, LANES), 1)
    cols = jnp.where(lane == 0, sel_start, jnp.where(lane == 1, sel_tot, 0.0))
    rows = cols.T
    start_row = rows[0:1, 0:nb]
    tot_row = rows[1:2, 0:nb]

    p = lax.broadcasted_iota(jnp.int32, (cap, 1), 0).astype(F32)
    onehot = jnp.where(jnp.logical_and(start_row <= p, p < start_row + tot_row), 1.0, 0.0)
    blk = lax.broadcasted_iota(jnp.int32, (1, nb), 1).astype(F32)
    p_start = jnp.sum(onehot * start_row, axis=1, keepdims=True)
    p_blk = jnp.sum(onehot * blk, axis=1, keepdims=True)
    ohb = onehot.astype(BF16)
    local_rows = _dot(ohb, sel_local.astype(BF16))
    sel_rows = _dot(ohb, self_.astype(BF16))
    earlier_rows = _dot(ohb, earlier.astype(BF16))
    aff_rows = lax.dot_general(onehot, a, (((1,), (0,)), ((), ())), precision=HIGHEST, preferred_element_type=F32)
    hit = jnp.where(jnp.logical_and(local_rows == p - p_start + 1.0, sel_rows > 0.5), 1.0, 0.0)
    lane_f = lax.broadcasted_iota(jnp.int32, (1, LANES), 1).astype(F32)
    tok = p_blk * LANES + jnp.sum(hit * lane_f, axis=1, keepdims=True)
    gate = jnp.sum(hit * aff_rows, axis=1, keepdims=True)
    rank = jnp.sum(hit * earlier_rows, axis=1, keepdims=True)
    idx_ref[0] = tok.astype(jnp.int32) + row_off
    gate_ref[0] = gate
    dest_ref[0] = (rank * dest_stride + tok).astype(jnp.int32) + row_off

    @pl.when(e == ne - 1)
    def _():
        cnt_ref[...] = cnt_acc[...].astype(jnp.int32)


def _route(aff_t, cap, dest_stride, row_off):
    ne, n = aff_t.shape
    n_pad = TOPK_BLOCKS * LANES
    assert n <= n_pad
    a = jnp.pad(aff_t, ((0, 0), (0, n_pad - n)), constant_values=-1.0).reshape(ne, TOPK_BLOCKS, LANES)
    slot = pl.BlockSpec((1, cap, 1), lambda e: (e, 0, 0))
    idx, gate, dest, cnt = pl.pallas_call(
        functools.partial(_topk_kernel, cap=cap, dest_stride=dest_stride, row_off=row_off),
        grid=(ne,),
        in_specs=[pl.BlockSpec((1, TOPK_BLOCKS, LANES), lambda e: (e, 0, 0))],
        out_specs=[slot, slot, slot, pl.BlockSpec((TOPK_BLOCKS, LANES), lambda e: (0, 0))],
        out_shape=[jax.ShapeDtypeStruct((ne, cap, 1), jnp.int32), jax.ShapeDtypeStruct((ne, cap, 1), F32),
                   jax.ShapeDtypeStruct((ne, cap, 1), jnp.int32), jax.ShapeDtypeStruct((TOPK_BLOCKS, LANES), jnp.int32)],
        scratch_shapes=[pltpu.VMEM((TOPK_BLOCKS, LANES), F32)],
        compiler_params=_cparams(("arbitrary",)),
        name="route_topk",
    )(a)
    return idx.reshape(ne * cap), gate, dest.reshape(ne * cap), cnt.reshape(n_pad)[:n]


def _row_copy(src, s, dst, t, sem):
    return pltpu.make_async_copy(src.at[pl.ds(s, 1)], dst.at[pl.ds(t, 1)], sem)


def _ffn_body(idx_sm, dest_sm, hl_hbm, gate_ref, wg_ref, wu_ref, wd_ref, z_hbm, xbuf, xb, acc, sem, *, tm, cap):
    e = pl.program_id(0)
    m = pl.program_id(1)
    f = pl.program_id(2)
    nf = pl.num_programs(2)
    base = e * cap + m * tm

    @pl.when(f == 0)
    def _():
        def start(r, c):
            _row_copy(hl_hbm, idx_sm[base + r], xbuf, r, sem).start()
            return c
        lax.fori_loop(0, tm, start, 0)

        def wait(r, c):
            _row_copy(hl_hbm, idx_sm[base + r], xbuf, r, sem).wait()
            return c
        lax.fori_loop(0, tm, wait, 0)
        xb[...] = xbuf[...].astype(BF16)
        acc[...] = jnp.zeros(acc.shape, F32)

    x = xb[...]
    g = _dot(x, wg_ref[0].astype(BF16))
    u = _dot(x, wu_ref[0].astype(BF16))
    hid = (g * jax.nn.sigmoid(g) * u).astype(BF16)
    acc[...] += _dot(hid, wd_ref[0].astype(BF16))

    @pl.when(f == nf - 1)
    def _():
        xbuf[...] = acc[...] * gate_ref[0]

        def start(r, c):
            _row_copy(xbuf, r, z_hbm, dest_sm[base + r], sem).start()
            return c
        lax.fori_loop(0, tm, start, 0)

        def wait(r, c):
            _row_copy(xbuf, r, z_hbm, dest_sm[base + r], sem).wait()
            return c
        lax.fori_loop(0, tm, wait, 0)


def _ffn_kernel_new(idx_sm, dest_sm, hl_hbm, gate_ref, wg_ref, wu_ref, wd_ref, z_hbm, xbuf, xb, acc, sem, **kw):
    _ffn_body(idx_sm, dest_sm, hl_hbm, gate_ref, wg_ref, wu_ref, wd_ref, z_hbm, xbuf, xb, acc, sem, **kw)


def _ffn_kernel_alias(idx_sm, dest_sm, hl_hbm, gate_ref, wg_ref, wu_ref, wd_ref, z_in, z_hbm, xbuf, xb, acc, sem, **kw):
    del z_in
    _ffn_body(idx_sm, dest_sm, hl_hbm, gate_ref, wg_ref, wu_ref, wd_ref, z_hbm, xbuf, xb, acc, sem, **kw)


def _expert_ffn(idx, dest, hl, gate, w_gate, w_up, w_down, z_rows, z=None):
    ne, d, ff = w_gate.shape
    cap = gate.shape[1]
    tm = min(cap, 1024)
    tf = 256
    any_spec = pl.BlockSpec(memory_space=pl.ANY)
    in_specs = [any_spec,
                pl.BlockSpec((1, tm, 1), lambda e, m, f, *_: (e, m, 0)),
                pl.BlockSpec((1, d, tf), lambda e, m, f, *_: (e, 0, f)),
                pl.BlockSpec((1, d, tf), lambda e, m, f, *_: (e, 0, f)),
                pl.BlockSpec((1, tf, d), lambda e, m, f, *_: (e, f, 0))]
    args = [idx, dest, hl, gate, w_gate, w_up, w_down]
    kw = dict(tm=tm, cap=cap)
    if z is None:
        kern, aliases = functools.partial(_ffn_kernel_new, **kw), {}
    else:
        kern, aliases = functools.partial(_ffn_kernel_alias, **kw), {len(args): 0}
        in_specs.append(any_spec)
        args.append(z)
    return pl.pallas_call(
        kern,
        grid_spec=pltpu.PrefetchScalarGridSpec(
            num_scalar_prefetch=2,
            grid=(ne, cap // tm, ff // tf),
            in_specs=in_specs,
            out_specs=any_spec,
            scratch_shapes=[pltpu.VMEM((tm, d), F32), pltpu.VMEM((tm, d), BF16), pltpu.VMEM((tm, d), F32),
                            pltpu.SemaphoreType.DMA]),
        out_shape=jax.ShapeDtypeStruct((z_rows, d), F32),
        input_output_aliases=aliases,
        compiler_params=_cparams(("arbitrary", "arbitrary", "arbitrary")),
        name="expert_ffn",
    )(*args)


def _combine_kernel(maxc_sm, z_ref, cnt_ref, x_ref, mod_ref, g_ref, o_ref, acc, *, n_lat_tiles, d):
    i = pl.program_id(0)
    k = pl.program_id(1)
    nk = pl.num_programs(1)

    @pl.when(k == 0)
    def _():
        acc[...] = jnp.zeros(acc.shape, F32)

    @pl.when(k < maxc_sm[i])
    def _():
        acc[...] += jnp.where(cnt_ref[...] > k, z_ref[...], 0.0)

    @pl.when(k == nk - 1)
    def _():
        r = jnp.where(i >= n_lat_tiles, 1, 0)
        o_ref[...] = x_ref[...] + _mod_rows(mod_ref, r, 5, d) * _rms(acc[...], g_ref[...])


def _combine(z, cnt, x, mod, g_post, n_rows, z_stride, n_lat_tiles_tb):
    d = x.shape[1]
    tb = 256
    nb = n_rows // tb
    zb = z_stride // tb
    maxc = jnp.max(cnt[:n_rows].reshape(nb, tb), axis=1)

    def z_map(i, k, maxc_sm):
        return (jnp.maximum(jnp.minimum(k, maxc_sm[i] - 1), 0) * zb + i, 0)

    return pl.pallas_call(
        functools.partial(_combine_kernel, n_lat_tiles=n_lat_tiles_tb, d=d),
        grid_spec=pltpu.PrefetchScalarGridSpec(
            num_scalar_prefetch=1,
            grid=(nb, N_EXPERTS),
            in_specs=[pl.BlockSpec((tb, d), z_map),
                      pl.BlockSpec((tb, 1), lambda i, k, s: (i, 0)),
                      pl.BlockSpec((tb, d), lambda i, k, s: (i, 0)),
                      pl.BlockSpec((8, N_MOD * d), lambda i, k, s: (0, 0)),
                      pl.BlockSpec((1, d), lambda i, k, s: (0, 0))],
            out_specs=pl.BlockSpec((tb, d), lambda i, k, s: (i, 0)),
            scratch_shapes=[pltpu.VMEM((tb, d), F32)]),
        out_shape=jax.ShapeDtypeStruct((n_rows, d), F32),
        compiler_params=_cparams(("parallel", "arbitrary")),
        name="moe_combine",
    )(maxc, z, cnt.reshape(-1, 1), x, mod, g_post.reshape(1, d))


def _moe(hl, aff_t, x_mid, mod, p, n_lat, n_ctx, s_pad, with_ctx):
    cap_l = EC_CAPACITY_FACTOR * n_lat // N_EXPERTS
    idx, gate, dest, cnt = _route(aff_t[:, :n_lat], cap_l, s_pad, 0)
    z = _expert_ffn(idx, dest, hl, gate, p['w_gate'], p['w_up'], p['w_down'], N_EXPERTS * s_pad)
    n_rows = n_lat
    if with_ctx:
        cap_c = EC_CAPACITY_FACTOR * n_ctx // N_EXPERTS
        idx_c, gate_c, dest_c, cnt_c = _route(aff_t[:, n_lat:n_lat + n_ctx], cap_c, s_pad, n_lat)
        z = _expert_ffn(idx_c, dest_c, hl, gate_c, p['w_gate'], p['w_up'], p['w_down'], N_EXPERTS * s_pad, z)
        cnt = jnp.concatenate([cnt, cnt_c, jnp.zeros((s_pad - n_lat - n_ctx,), jnp.int32)])
        n_rows = s_pad
    return _combine(z, cnt, x_mid, mod, p['g_post_ffn'], n_rows, s_pad, n_lat // 256)


def _proj1_kernel(x_ref, mod_ref, gpre_ref, w_ref, wg_ref, bg_ref, o_ref, g_ref, h_scr, *, n_lat_tiles, d):
    i = pl.program_id(0)
    j = pl.program_id(1)

    @pl.when(j == 0)
    def _():
        r = jnp.where(i >= n_lat_tiles, 1, 0)
        h = _premix(x_ref, mod_ref, gpre_ref, r, d)
        h_scr[...] = h.astype(BF16)
        g_ref[...] = lax.dot_general(h, wg_ref[...], (((1,), (0,)), ((), ())), precision=HIGHEST,
                                     preferred_element_type=F32) + bg_ref[...]

    o_ref[...] = _dot(h_scr[...], w_ref[...]).astype(BF16)


def _proj1(xs, mod, g_pre, w_main, w_gates, b_gate, n_lat_tiles):
    s_pad, d = xs.shape
    n_out = w_main.shape[1]
    ng = w_gates.shape[1]
    tm, tn = ROW_TILE, 512
    row = lambda i, j: (i, 0)
    const = lambda i, j: (0, 0)
    return pl.pallas_call(
        functools.partial(_proj1_kernel, n_lat_tiles=n_lat_tiles, d=d),
        grid=(s_pad // tm, n_out // tn),
        in_specs=[pl.BlockSpec((tm, d), row),
                  pl.BlockSpec((8, N_MOD * d), const),
                  pl.BlockSpec((1, d), const),
                  pl.BlockSpec((d, tn), lambda i, j: (0, j)),
                  pl.BlockSpec((d, ng), const),
                  pl.BlockSpec((1, ng), const)],
        out_specs=[pl.BlockSpec((tm, tn), lambda i, j: (i, j)), pl.BlockSpec((tm, ng), row)],
        out_shape=[jax.ShapeDtypeStruct((s_pad, n_out), BF16), jax.ShapeDtypeStruct((s_pad, ng), F32)],
        scratch_shapes=[pltpu.VMEM((tm, d), BF16)],
        compiler_params=_cparams(("parallel", "arbitrary")),
        name="proj1",
    )(xs, mod, g_pre.reshape(1, d), w_main, w_gates, b_gate.reshape(1, ng))


def _conv_kernel(cur_ref, prev_ref, next_ref, w_ref, b_ref, o_ref, *, n_lat, n_ctx, k_cols0, k_scale):
    i = pl.program_id(0)
    j = pl.program_id(1)
    tm, tc = cur_ref.shape
    hr = prev_ref.shape[0]
    cur = cur_ref[...].astype(F32)
    row = lax.broadcasted_iota(jnp.int32, (tm, 1), 0)
    g = i * tm + row
    prev_row = prev_ref[hr - 1:hr, :].astype(F32)
    next_row = next_ref[0:1, :].astype(F32)
    before = jnp.where(row == 0, prev_row, pltpu.roll(cur, 1, 0))
    after = jnp.where(row == tm - 1, next_row, pltpu.roll(cur, tm - 1, 0))
    before = jnp.where(jnp.logical_or(g == 0, g == n_lat), 0.0, before)
    after = jnp.where(jnp.logical_or(g == n_lat - 1, g == n_lat + n_ctx - 1), 0.0, after)
    y = before * w_ref[0:1, :] + cur * w_ref[1:2, :] + after * w_ref[2:3, :] + b_ref[...]
    y = y * jax.nn.sigmoid(y)
    o_ref[...] = (y * jnp.where(j * tc >= k_cols0, k_scale, 1.0)).astype(BF16)


def _conv_silu(u, w_conv, b_conv, n_cols, n_lat, n_ctx):
    s_pad = u.shape[0]
    tm, tc, hr = ROW_TILE, 1024, 16
    rb = tm // hr
    last = s_pad // hr - 1
    return pl.pallas_call(
        functools.partial(_conv_kernel, n_lat=n_lat, n_ctx=n_ctx, k_cols0=n_cols // 2, k_scale=MLSTM_DIM ** -0.5),
        grid=(s_pad // tm, n_cols // tc),
        in_specs=[pl.BlockSpec((tm, tc), lambda i, j: (i, j)),
                  pl.BlockSpec((hr, tc), lambda i, j: (jnp.maximum(i * rb - 1, 0), j)),
                  pl.BlockSpec((hr, tc), lambda i, j: (jnp.minimum((i + 1) * rb, last), j)),
                  pl.BlockSpec((8, tc), lambda i, j: (0, j)),
                  pl.BlockSpec((1, tc), lambda i, j: (0, j))],
        out_specs=pl.BlockSpec((tm, tc), lambda i, j: (i, j)),
        out_shape=jax.ShapeDtypeStruct((s_pad, n_cols), BF16),
        compiler_params=_cparams(("parallel", "parallel")),
        name="conv_silu",
    )(u, u, u, jnp.pad(w_conv, ((0, 8 - w_conv.shape[0]), (0, 0))), b_conv.reshape(1, n_cols))


def _log_sigmoid(x):
    return -(jnp.maximum(-x, 0.0) + jnp.log1p(jnp.exp(-jnp.abs(x))))


def _mlstm_dir(q, k, v, ig_c, fg_c, ig_r, fg_r, c_ref, n_ref, m_ref, o_ref, reverse):
    L = q.shape[0]
    lf_c = _log_sigmoid(fg_c)
    lf_r = _log_sigmoid(fg_r)
    t_i = lax.broadcasted_iota(jnp.int32, (L, L), 0)
    s_i = lax.broadcasted_iota(jnp.int32, (L, L), 1)
    seen = (s_i >= t_i) if reverse else (s_i <= t_i)
    seen_f = jnp.where(seen, 1.0, 0.0)
    b_c = jnp.sum(seen_f * lf_r, axis=1, keepdims=True)
    b_r = jnp.sum(jnp.where(seen, 0.0, 1.0) * lf_c, axis=0, keepdims=True) + lf_r
    m_prev = m_ref[...]
    log_d = jnp.where(seen, b_c - b_r + ig_r, -jnp.inf)
    log_inter = b_c + m_prev
    m_t = jnp.maximum(log_inter, jnp.max(log_d, axis=1, keepdims=True))
    w_intra = jnp.exp(log_d - m_t) * _dot_nt(q, k)
    w_inter = jnp.exp(log_inter - m_t)
    c_prev = c_ref[...]
    n_prev = n_ref[...]
    qf = q.astype(F32)
    num = w_inter * _dot_nt(q, c_prev.astype(BF16)) + _dot(w_intra.astype(BF16), v)
    den = w_inter * jnp.sum(qf * n_prev, axis=1, keepdims=True) + jnp.sum(w_intra, axis=1, keepdims=True)
    o_ref[...] = (num / jnp.maximum(jnp.abs(den), jnp.exp(-m_t))).astype(o_ref.dtype)
    b_end = jnp.sum(lf_c, axis=0, keepdims=True)
    log_w = b_end - b_c + ig_c
    m_new = jnp.maximum(b_end + m_prev, jnp.max(log_w, axis=0, keepdims=True))
    w = jnp.exp(log_w - m_new)
    decay = jnp.exp(b_end + m_prev - m_new)
    wv = (w * v.astype(F32)).astype(BF16)
    c_ref[...] = decay * c_prev + lax.dot_general(wv, k, (((0,), (0,)), ((), ())), preferred_element_type=F32)
    n_ref[...] = decay * n_prev + jnp.sum(w * k.astype(F32), axis=0, keepdims=True)
    m_ref[...] = m_new


def _mlstm_kernel(qf_ref, kf_ref, vf_ref, gcf_ref, grf_ref, qb_ref, kb_ref, vb_ref, gcb_ref, grb_ref,
                  hf_ref, hb_ref, c_scr, n_scr, m_scr):
    @pl.when(pl.program_id(1) == 0)
    def _():
        c_scr[...] = jnp.zeros(c_scr.shape, F32)
        n_scr[...] = jnp.zeros(n_scr.shape, F32)
        m_scr[...] = jnp.zeros(m_scr.shape, F32)

    gc = gcf_ref[0]
    gr = grf_ref[0]
    _mlstm_dir(qf_ref[...], kf_ref[...], vf_ref[...], gc[:, 0:1], gc[:, 1:2], gr[0:1, :], gr[1:2, :],
               c_scr.at[0], n_scr.at[0], m_scr.at[0], hf_ref, False)
    gc = gcb_ref[0]
    gr = grb_ref[0]
    _mlstm_dir(qb_ref[...], kb_ref[...], vb_ref[...], gc[:, 2:3], gc[:, 3:4], gr[2:3, :], gr[3:4, :],
               c_scr.at[1], n_scr.at[1], m_scr.at[1], hb_ref, True)


def _mlstm_scan(qk, u, gates, n_lat, n_ctx):
    s_pad = qk.shape[0]
    L, dh, nh = MLSTM_CHUNK, MLSTM_DIM, MLSTM_HEADS
    nlc, ncc = n_lat // L, n_ctx // L
    steps = nlc + ncc
    g4 = gates.reshape(s_pad, 2, 2, nh).transpose(3, 0, 1, 2).reshape(nh, s_pad, 4)
    g4t = g4.transpose(0, 2, 1)

    def fwd(c):
        return jnp.where(c < ncc, nlc + c, c - ncc)

    def bwd(c):
        return jnp.where(c < ncc, nlc + ncc - 1 - c, nlc - 1 - (c - ncc))

    v_blk0 = 2 * nh
    def specs(pos, qkv_from):
        return [pl.BlockSpec((L, dh), lambda h, c: (pos(c), h)),
                pl.BlockSpec((L, dh), lambda h, c: (pos(c), nh + h)),
                pl.BlockSpec((L, dh), lambda h, c: (pos(c), v_blk0 + h)),
                pl.BlockSpec((1, L, 4), lambda h, c: (h, pos(c), 0)),
                pl.BlockSpec((1, 4, L), lambda h, c: (h, 0, pos(c)))]

    out_shape = jax.ShapeDtypeStruct((steps * L, nh * dh), BF16)
    return pl.pallas_call(
        _mlstm_kernel,
        grid=(nh, steps),
        in_specs=specs(fwd, None) + specs(bwd, None),
        out_specs=[pl.BlockSpec((L, dh), lambda h, c: (fwd(c), h)), pl.BlockSpec((L, dh), lambda h, c: (bwd(c), h))],
        out_shape=[out_shape, out_shape],
        scratch_shapes=[pltpu.VMEM((2, dh, dh), F32), pltpu.VMEM((2, 1, dh), F32), pltpu.VMEM((2, 1, 1), F32)],
        compiler_params=_cparams(("parallel", "arbitrary")),
        name="mlstm_scan",
    )(qk, qk, u, g4, g4t, qk, qk, u, g4, g4t)


def kernel(x, c, ctx, c_ctx, w_mod_0, b_mod_0, g_pre_mix_0, g_post_mix_0, w_in_0, lambda_q1_0, lambda_k1_0, lambda_q2_0, lambda_k2_0, g_subln_0, g_q_0, g_k_0, w_out_0, g_pre_ffn_0, g_post_ffn_0, w_router_0, w_gate_0, w_up_0, w_down_0, w_mod_1, b_mod_1, g_pre_mix_1, g_post_mix_1, w_in_1, b_gate_1, w_conv_1, b_conv_1, g_head_1, w_out_1, g_pre_ffn_1, g_post_ffn_1, w_router_1, w_gate_1, w_up_1, w_down_1):
    b, n_lat, d = x.shape
    n_ctx = ctx.shape[1]
    assert b == 1 and n_lat % ROW_TILE == 0 and n_ctx % MLSTM_CHUNK == 0 and n_ctx <= ROW_TILE
    s_valid = n_lat + n_ctx
    s_pad = n_lat + ROW_TILE
    n_lat_tiles = n_lat // ROW_TILE

    xs = jnp.concatenate([x[0], ctx[0], jnp.zeros((s_pad - s_valid, d), F32)], axis=0)
    cc = jnp.concatenate([c, c_ctx[None, :], jnp.zeros((6, d), F32)], axis=0)

    mod0 = _modulation(cc, w_mod_0, b_mod_0)
    tabs = _rope_tables(n_lat, n_ctx, s_pad)
    p0 = _proj0(xs, mod0, g_pre_mix_0, w_in_0.astype(BF16), tabs, g_q_0, g_k_0, n_lat_tiles)
    lam_init = 0.8 - 0.6 * math.exp(-0.3 * 0)
    lam4 = jnp.stack([lambda_q1_0, lambda_k1_0, lambda_q2_0, lambda_k2_0])
    o_a = _attn_diff(p0, lam4, g_subln_0, n_lat_tiles, n_lat, s_valid, lam_init)
    o_b = _attn_gqa(p0, n_lat_tiles, n_lat, s_valid)
    ka = o_a.shape[1]
    x_mid, hl, aff_t = _mixer_out(
        _out0_kernel, (o_a, o_b),
        lambda tm: [pl.BlockSpec((tm, ka), lambda i: (i, 0)), pl.BlockSpec((tm, o_b.shape[1]), lambda i: (i, 0))],
        xs, mod0, w_out_0.astype(BF16), g_post_mix_0, g_pre_ffn_0, w_router_0.T, s_pad, n_lat // 256,
        lambda tm: [], "attn_out")
    p_moe0 = dict(w_gate=w_gate_0, w_up=w_up_0, w_down=w_down_0, g_post_ffn=g_post_ffn_0)
    xs = _moe(hl, aff_t, x_mid, mod0, p_moe0, n_lat, n_ctx, s_pad, True)

    mod1 = _modulation(cc, w_mod_1, b_mod_1)
    inner = MLSTM_HEADS * MLSTM_DIM
    u, gates = _proj1(xs, mod1, g_pre_mix_1, w_in_1[:, :4 * inner].astype(BF16), w_in_1[:, 4 * inner:], b_gate_1,
                      n_lat_tiles)
    qk = _conv_silu(u, w_conv_1, b_conv_1, 2 * inner, n_lat, n_ctx)
    hf, hb = _mlstm_scan(qk, u, gates, n_lat, n_ctx)
    og_blk = 3 * inner // inner
    x_mid, hl, aff_t = _mixer_out(
        _out1_kernel, (hf, hb, u, g_head_1.reshape(1, inner)),
        lambda tm: [pl.BlockSpec((tm, inner), lambda i: (i, 0)), pl.BlockSpec((tm, inner), lambda i: (i, 0)),
                    pl.BlockSpec((tm, inner), lambda i: (i, og_blk)), pl.BlockSpec((1, inner), lambda i: (0, 0))],
        xs, mod1, w_out_1.astype(BF16), g_post_mix_1, g_pre_ffn_1, w_router_1.T, n_lat, n_lat // 256,
        lambda tm: [pltpu.VMEM((tm, inner), BF16)], "mlstm_out")
    p_moe1 = dict(w_gate=w_gate_1, w_up=w_up_1, w_down=w_down_1, g_post_ffn=g_post_ffn_1)
    out = _moe(hl, aff_t, x_mid, mod1, p_moe1, n_lat, n_ctx, s_pad, False)
    return out[None]
```

```python
import functools
import math

import jax
import jax.numpy as jnp
from jax import lax
from jax.experimental import pallas as pl
from jax.experimental.pallas import tpu as pltpu

F32 = jnp.float32
BF16 = jnp.bfloat16
HIGHEST = lax.Precision.HIGHEST

GRID_W = 64
ROPE_THETA = 10000.0
RMS_EPS = 1e-6
N_MOD = 6
DIFF_HEADS = 8
DIFF_DIM = 64
GQA_HEADS = 8
GQA_KV_HEADS = 2
GQA_DIM = 128
MLSTM_HEADS = 8
MLSTM_DIM = 256
MLSTM_CHUNK = 128
N_EXPERTS = 16
EC_CAPACITY_FACTOR = 2

LANES = 128
ROW_TILE = 512
ATTN_TK = 512
ATTN_ROW_GROUP = 256
LOG2_E = math.log2(math.e)
TOPK_BLOCKS = 128
VMEM_LIMIT = 56 * 1024 * 1024


def _cparams(sem):
    return pltpu.CompilerParams(dimension_semantics=sem, vmem_limit_bytes=VMEM_LIMIT)


def _rms(x, g):
    return x * lax.rsqrt(jnp.mean(x * x, axis=-1, keepdims=True) + RMS_EPS) * g


def _dot(a, b):
    return jnp.dot(a, b, preferred_element_type=F32)


def _dot_nt(a, b, precision=None):
    return lax.dot_general(a, b, (((1,), (1,)), ((), ())), precision=precision, preferred_element_type=F32)


def _mod_kernel(c_ref, w_ref, b_ref, o_ref):
    c = c_ref[...]
    s = c * jax.nn.sigmoid(c)
    o_ref[...] = lax.dot_general(s, w_ref[...], (((1,), (0,)), ((), ())), precision=HIGHEST,
                                 preferred_element_type=F32) + b_ref[...]


def _modulation(cc, w_mod, b_mod):
    d, n = w_mod.shape
    tn = 1024
    return pl.pallas_call(
        _mod_kernel,
        grid=(n // tn,),
        in_specs=[pl.BlockSpec((8, d), lambda j: (0, 0)),
                  pl.BlockSpec((d, tn), lambda j: (0, j)),
                  pl.BlockSpec((1, tn), lambda j: (0, j))],
        out_specs=pl.BlockSpec((8, tn), lambda j: (0, j)),
        out_shape=jax.ShapeDtypeStruct((8, n), F32),
        compiler_params=_cparams(("arbitrary",)),
        name="modulation",
    )(cc, w_mod, b_mod.reshape(1, n))


def _mod_rows(mod_ref, r, k, d):
    return mod_ref[pl.ds(r, 1), k * d:(k + 1) * d]


def _premix(x_ref, mod_ref, g_ref, r, d):
    x = x_ref[...]
    return _rms(x, g_ref[...]) * (1.0 + _mod_rows(mod_ref, r, 1, d)) + _mod_rows(mod_ref, r, 0, d)


def _rope_a(x, ca, sa, sb):
    return x * ca + pltpu.roll(x, LANES - DIFF_DIM // 2, 1) * sa + pltpu.roll(x, DIFF_DIM // 2, 1) * sb


def _rope_b(x, cb, sb):
    return x * cb + pltpu.roll(x, GQA_DIM // 2, 1) * sb


def _proj0_kernel(x_ref, mod_ref, gpre_ref, w_ref, ca_ref, saa_ref, sab_ref, cb_ref, sbb_ref, gq_ref, gk_ref,
                  o_ref, h_scr, *, n_lat_tiles, d, tn):
    i = pl.program_id(0)
    j = pl.program_id(1)
    ng = tn // LANES

    @pl.when(j == 0)
    def _():
        r = jnp.where(i >= n_lat_tiles, 1, 0)
        h_scr[...] = _premix(x_ref, mod_ref, gpre_ref, r, d).astype(BF16)

    acc = _dot(h_scr[...], w_ref[...])

    def groups(fn):
        for g in range(ng):
            o_ref[:, g * LANES:(g + 1) * LANES] = fn(acc[:, g * LANES:(g + 1) * LANES], g).astype(BF16)

    qa_tiles = DIFF_HEADS * 2 * DIFF_DIM // tn
    qb0 = 3 * qa_tiles
    qb_tiles = GQA_HEADS * GQA_DIM // tn

    @pl.when(j < 2 * qa_tiles)
    def _():
        sc = jnp.where(j < qa_tiles, DIFF_DIM ** -0.5 * LOG2_E, 1.0)
        groups(lambda a, g: _rope_a(a, ca_ref[...], saa_ref[...], sab_ref[...]) * sc)

    @pl.when(jnp.logical_and(j >= 2 * qa_tiles, j < qb0))
    def _():
        groups(lambda a, g: a)

    @pl.when(jnp.logical_and(j >= qb0, j < qb0 + qb_tiles))
    def _():
        groups(lambda a, g: _rope_b(_rms(a, gq_ref[...]), cb_ref[...], sbb_ref[...]) * (GQA_DIM ** -0.5 * LOG2_E))

    @pl.when(j == qb0 + qb_tiles)
    def _():
        def fn(a, g):
            if g < GQA_KV_HEADS:
                return _rope_b(_rms(a, gk_ref[...]), cb_ref[...], sbb_ref[...])
            return a
        groups(fn)


def _proj0(xs, mod, g_pre, w_in, tabs, g_q, g_k, n_lat_tiles):
    s_pad, d = xs.shape
    n_out = w_in.shape[1]
    tm, tn = ROW_TILE, 512
    assert GQA_KV_HEADS * GQA_DIM * 2 == tn
    row = lambda i, j: (i, 0)
    const = lambda i, j: (0, 0)
    tab_spec = pl.BlockSpec((tm, LANES), row)
    return pl.pallas_call(
        functools.partial(_proj0_kernel, n_lat_tiles=n_lat_tiles, d=d, tn=tn),
        grid=(s_pad // tm, n_out // tn),
        in_specs=[pl.BlockSpec((tm, d), row),
                  pl.BlockSpec((8, N_MOD * d), const),
                  pl.BlockSpec((1, d), const),
                  pl.BlockSpec((d, tn), lambda i, j: (0, j)),
                  tab_spec, tab_spec, tab_spec, tab_spec, tab_spec,
                  pl.BlockSpec((1, LANES), const),
                  pl.BlockSpec((1, LANES), const)],
        out_specs=pl.BlockSpec((tm, tn), lambda i, j: (i, j)),
        out_shape=jax.ShapeDtypeStruct((s_pad, n_out), BF16),
        scratch_shapes=[pltpu.VMEM((tm, d), BF16)],
        compiler_params=_cparams(("parallel", "arbitrary")),
        name="proj0",
    )(xs, mod, g_pre.reshape(1, d), w_in, *tabs, g_q.reshape(1, LANES), g_k.reshape(1, LANES))


def _rope_tables(n_lat, n_ctx, s_pad):
    rows = jnp.repeat(jnp.arange(n_lat // GRID_W, dtype=F32), GRID_W)
    cols = jnp.tile(jnp.arange(GRID_W, dtype=F32), n_lat // GRID_W)

    def angles(dim):
        axis_dim = dim // 2
        inv_freq = ROPE_THETA ** (-jnp.arange(0, axis_dim, 2, dtype=F32) / axis_dim)
        ang = jnp.concatenate([rows[:, None] * inv_freq, cols[:, None] * inv_freq], axis=-1)
        return jnp.cos(ang), jnp.sin(ang)

    def pad(t, fill):
        return jnp.concatenate([t, jnp.full((s_pad - n_lat, LANES), fill, F32)], axis=0)

    cos_a, sin_a = angles(DIFF_DIM)
    zero = jnp.zeros_like(sin_a)
    ca = jnp.tile(cos_a, (1, 4))
    saa = jnp.tile(jnp.concatenate([-sin_a, zero], axis=1), (1, 2))
    sab = jnp.tile(jnp.concatenate([zero, sin_a], axis=1), (1, 2))
    cos_b, sin_b = angles(GQA_DIM)
    cb = jnp.tile(cos_b, (1, 2))
    sbb = jnp.concatenate([-sin_b, sin_b], axis=1)
    return pad(ca, 1.0), pad(saa, 0.0), pad(sab, 0.0), pad(cb, 1.0), pad(sbb, 0.0)


def _flash_rows(q, k_ref, v_ref, m_scr, acc_scr, s_scr, is_ctx_q, tk, s_valid):
    n_chunks = k_ref.shape[0] // tk
    last = n_chunks - 1
    groups = [slice(g * ATTN_ROW_GROUP, (g + 1) * ATTN_ROW_GROUP) for g in range(q.shape[0] // ATTN_ROW_GROUP)]
    m_scr[...] = jnp.full(m_scr.shape, -jnp.inf, F32)
    acc_scr[...] = jnp.zeros(acc_scr.shape, F32)

    def rows_of(c):
        return pl.ds(c * tk if isinstance(c, int) else pl.multiple_of(c * tk, tk), tk)

    def scores(c, buf, rows):
        s_scr[buf, rows] = _dot_nt(q[rows], k_ref[rows_of(c), :])

    def consume(c, buf, rows, v1, masked):
        s = s_scr[buf, rows]
        if masked:
            kpos = c * tk + lax.broadcasted_iota(jnp.int32, (1, tk), 1)
            s = jnp.where(kpos < s_valid, s, -jnp.inf)
        m_prev = m_scr[rows]
        m_new = jnp.maximum(m_prev, jnp.max(s, axis=1, keepdims=True))
        p = jnp.exp2(s - m_new).astype(BF16)
        acc_scr[rows] = jnp.exp2(m_prev - m_new) * acc_scr[rows] + _dot(p, v1)
        m_scr[rows] = m_new

    def values(c):
        vc = v_ref[rows_of(c), :]
        return jnp.concatenate([vc, jnp.ones_like(vc)], axis=1)

    assert last % 2 == 0
    lo = jnp.where(is_ctx_q, last, 0)
    for rows in groups:
        scores(lo, 0, rows)

    def body(i, carry):
        c = lo + 2 * i
        for buf in (0, 1):
            v1 = values(c + buf)
            for rows in groups:
                scores(c + buf + 1, 1 - buf, rows)
                consume(c + buf, buf, rows, v1, False)
        return carry

    lax.fori_loop(0, (last - lo) // 2, body, 0)
    v1 = values(last)
    for rows in groups:
        consume(last, 0, rows, v1, True)


def _attn_diff_kernel(q_ref, k_ref, v_ref, lam_ref, gsub_ref, o_ref, m_scr, acc_scr, s_scr,
                      *, n_lat_tiles, tk, s_valid, lam_init):
    q = q_ref[...]
    tq, hd = q.shape
    first = lax.broadcasted_iota(jnp.int32, (1, hd), 1) < DIFF_DIM
    zero = jnp.zeros_like(q)
    qq = jnp.concatenate([jnp.where(first, q, zero), jnp.where(first, zero, q)], axis=0)
    _flash_rows(qq, k_ref, v_ref, m_scr, acc_scr, s_scr, pl.program_id(1) >= n_lat_tiles, tk, s_valid)
    lam = (jnp.exp(jnp.sum(lam_ref[0:1, :] * lam_ref[1:2, :], axis=1, keepdims=True))
           - jnp.exp(jnp.sum(lam_ref[2:3, :] * lam_ref[3:4, :], axis=1, keepdims=True)) + lam_init)
    o = (acc_scr[0:tq, 0:hd] / acc_scr[0:tq, hd:hd + 1]
         - lam * (acc_scr[tq:2 * tq, 0:hd] / acc_scr[tq:2 * tq, hd:hd + 1]))
    o_ref[...] = (_rms(o, gsub_ref[...]) * (1.0 - lam_init)).astype(BF16)


def _attn_diff(p0, lam4, g_subln, n_lat_tiles, s_valid, lam_init):
    s_pad = p0.shape[0]
    tq = ROW_TILE
    hd = 2 * DIFF_DIM
    k0 = DIFF_HEADS
    v0 = 2 * DIFF_HEADS
    return pl.pallas_call(
        functools.partial(_attn_diff_kernel, n_lat_tiles=n_lat_tiles, tk=ATTN_TK, s_valid=s_valid, lam_init=lam_init),
        grid=(DIFF_HEADS, s_pad // tq),
        in_specs=[pl.BlockSpec((tq, hd), lambda h, iq: (iq, h)),
                  pl.BlockSpec((s_pad, hd), lambda h, iq: (0, k0 + h)),
                  pl.BlockSpec((s_pad, hd), lambda h, iq: (0, v0 + h)),
                  pl.BlockSpec((4, DIFF_DIM), lambda h, iq: (0, 0)),
                  pl.BlockSpec((1, hd), lambda h, iq: (0, 0))],
        out_specs=pl.BlockSpec((tq, hd), lambda h, iq: (iq, h)),
        out_shape=jax.ShapeDtypeStruct((s_pad, DIFF_HEADS * hd), BF16),
        scratch_shapes=[pltpu.VMEM((2 * tq, 1), F32), pltpu.VMEM((2 * tq, 2 * hd), F32),
                        pltpu.VMEM((2, 2 * tq, ATTN_TK), F32)],
        compiler_params=_cparams(("parallel", "arbitrary")),
        name="attn_diff",
    )(p0, p0, p0, lam4, g_subln.reshape(1, hd))


def _attn_gqa_kernel(q_ref, k_ref, v_ref, o_ref, m_scr, acc_scr, s_scr, *, n_lat_tiles, tk, s_valid, group):
    tq = q_ref.shape[0]
    dh = GQA_DIM
    qq = jnp.concatenate([q_ref[:, g * dh:(g + 1) * dh] for g in range(group)], axis=0)
    _flash_rows(qq, k_ref, v_ref, m_scr, acc_scr, s_scr, pl.program_id(1) >= n_lat_tiles, tk, s_valid)
    for g in range(group):
        rows = slice(g * tq, (g + 1) * tq)
        o_ref[:, g * dh:(g + 1) * dh] = (acc_scr[rows, 0:dh] / acc_scr[rows, dh:dh + 1]).astype(BF16)


def _attn_gqa(p0, n_lat, s_valid):
    s_pad = p0.shape[0]
    tq = 256
    group = GQA_HEADS // GQA_KV_HEADS
    qw = group * GQA_DIM
    q0 = 3 * DIFF_HEADS * 2 * DIFF_DIM // qw
    k0 = (3 * DIFF_HEADS * 2 * DIFF_DIM + GQA_HEADS * GQA_DIM) // GQA_DIM
    v0 = k0 + GQA_KV_HEADS
    return pl.pallas_call(
        functools.partial(_attn_gqa_kernel, n_lat_tiles=n_lat // tq, tk=ATTN_TK, s_valid=s_valid, group=group),
        grid=(GQA_KV_HEADS, s_pad // tq),
        in_specs=[pl.BlockSpec((tq, qw), lambda h, iq: (iq, q0 + h)),
                  pl.BlockSpec((s_pad, GQA_DIM), lambda h, iq: (0, k0 + h)),
                  pl.BlockSpec((s_pad, GQA_DIM), lambda h, iq: (0, v0 + h))],
        out_specs=pl.BlockSpec((tq, qw), lambda h, iq: (iq, h)),
        out_shape=jax.ShapeDtypeStruct((s_pad, GQA_HEADS * GQA_DIM), BF16),
        scratch_shapes=[pltpu.VMEM((group * tq, 1), F32), pltpu.VMEM((group * tq, 2 * GQA_DIM), F32),
                        pltpu.VMEM((2, group * tq, ATTN_TK), F32)],
        compiler_params=_cparams(("parallel", "arbitrary")),
        name="attn_gqa",
    )(p0, p0, p0)


def _post_mix(y, x_ref, mod_ref, r, gpost_ref, gffn_ref, wr_ref, xo_ref, hl_ref, aff_ref, d):
    xn = x_ref[...] + _mod_rows(mod_ref, r, 2, d) * _rms(y, gpost_ref[...])
    xo_ref[...] = xn
    hl = _rms(xn, gffn_ref[...]) * (1.0 + _mod_rows(mod_ref, r, 4, d)) + _mod_rows(mod_ref, r, 3, d)
    hl_ref[...] = hl
    logits = _dot_nt(wr_ref[...], hl, precision=HIGHEST)
    e = jnp.exp(logits - jnp.max(logits, axis=0, keepdims=True))
    aff_ref[...] = e / jnp.sum(e, axis=0, keepdims=True)


def _out0_kernel(oa_ref, ob_ref, x_ref, mod_ref, w_ref, gpost_ref, gffn_ref, wr_ref, xo_ref, hl_ref, aff_ref,
                 *, n_lat_tiles, d):
    r = jnp.where(pl.program_id(0) >= n_lat_tiles, 1, 0)
    ka = oa_ref.shape[1]
    y = _dot(oa_ref[...], w_ref[0:ka, :]) + _dot(ob_ref[...], w_ref[ka:, :])
    _post_mix(y, x_ref, mod_ref, r, gpost_ref, gffn_ref, wr_ref, xo_ref, hl_ref, aff_ref, d)


def _out1_kernel(hf_ref, hb_ref, og_ref, ghead_ref, x_ref, mod_ref, w_ref, gpost_ref, gffn_ref, wr_ref,
                 xo_ref, hl_ref, aff_ref, y_scr, *, n_lat_tiles, d):
    r = jnp.where(pl.program_id(0) >= n_lat_tiles, 1, 0)
    for h in range(MLSTM_HEADS):
        sl = slice(h * MLSTM_DIM, (h + 1) * MLSTM_DIM)
        hs = hf_ref[:, sl].astype(F32) + hb_ref[:, sl].astype(F32)
        hn = _rms(hs, ghead_ref[:, sl])
        y_scr[:, sl] = (jax.nn.sigmoid(og_ref[:, sl].astype(F32)) * hn).astype(BF16)
    y = _dot(y_scr[...], w_ref[...])
    _post_mix(y, x_ref, mod_ref, r, gpost_ref, gffn_ref, wr_ref, xo_ref, hl_ref, aff_ref, d)


def _mixer_out(kernel_fn, mix_inputs, mix_specs, x, mod, w_out, g_post, g_ffn, w_router_t, n_rows, n_lat_tiles_tm,
               scratch, name):
    d = x.shape[1]
    tm = 256
    row = lambda i: (i, 0)
    const = lambda i: (0, 0)
    ne = w_router_t.shape[0]
    return pl.pallas_call(
        functools.partial(kernel_fn, n_lat_tiles=n_lat_tiles_tm, d=d),
        grid=(n_rows // tm,),
        in_specs=mix_specs(tm) + [pl.BlockSpec((tm, d), row),
                                  pl.BlockSpec((8, N_MOD * d), const),
                                  pl.BlockSpec(w_out.shape, const),
                                  pl.BlockSpec((1, d), const),
                                  pl.BlockSpec((1, d), const),
                                  pl.BlockSpec((ne, d), const)],
        out_specs=[pl.BlockSpec((tm, d), row), pl.BlockSpec((tm, d), row), pl.BlockSpec((ne, tm), lambda i: (0, i))],
        out_shape=[jax.ShapeDtypeStruct((n_rows, d), F32), jax.ShapeDtypeStruct((n_rows, d), F32),
                   jax.ShapeDtypeStruct((ne, n_rows), F32)],
        scratch_shapes=scratch(tm),
        compiler_params=_cparams(("parallel",)),
        name=name,
    )(*mix_inputs, x, mod, w_out, g_post.reshape(1, d), g_ffn.reshape(1, d), w_router_t)


def _incl_cumsum(maskf, upper, strict_lower):
    mb = maskf.astype(BF16)
    local = _dot(mb, upper)
    start = jnp.sum(_dot(strict_lower, mb), axis=1, keepdims=True)
    return local, start


def _topk_kernel(aff_ref, idx_ref, gate_ref, dest_ref, cnt_ref, cnt_acc, *, cap, dest_stride, row_off):
    e = pl.program_id(0)
    ne = pl.num_programs(0)
    nb = aff_ref.shape[1]
    a = aff_ref[0]
    bits = pltpu.bitcast(a, jnp.int32)

    thr = jnp.zeros((1, 1), jnp.int32)
    for bit in range(30, -1, -1):
        cand = thr | (1 << bit)
        c = jnp.sum(jnp.sum(jnp.where(bits >= cand, 1.0, 0.0), axis=1, keepdims=True), axis=0, keepdims=True)
        thr = jnp.where(c >= cap, cand, thr)

    r_i = lax.broadcasted_iota(jnp.int32, (LANES, LANES), 0)
    c_i = lax.broadcasted_iota(jnp.int32, (LANES, LANES), 1)
    upper = jnp.where(r_i <= c_i, 1.0, 0.0).astype(BF16)
    rb = lax.broadcasted_iota(jnp.int32, (nb, nb), 0)
    cb = lax.broadcasted_iota(jnp.int32, (nb, nb), 1)
    strict_lower = jnp.where(cb < rb, 1.0, 0.0).astype(BF16)

    gt = bits > thr
    eq = bits == thr
    n_gt = jnp.sum(jnp.sum(jnp.where(gt, 1.0, 0.0), axis=1, keepdims=True), axis=0, keepdims=True)
    eq_local, eq_start = _incl_cumsum(jnp.where(eq, 1.0, 0.0), upper, strict_lower)
    sel = jnp.logical_or(gt, jnp.logical_and(eq, eq_local + eq_start <= cap - n_gt))
    self_ = jnp.where(sel, 1.0, 0.0)
    sel_local, sel_start = _incl_cumsum(self_, upper, strict_lower)
    sel_tot = jnp.sum(self_, axis=1, keepdims=True)

    @pl.when(e == 0)
    def _():
        cnt_acc[...] = jnp.zeros(cnt_acc.shape, F32)

    earlier = cnt_acc[...]
    cnt_acc[...] = earlier + self_

    lane = lax.broadcasted_iota(jnp.int32, (nb, LANES), 1)
    cols = jnp.where(lane == 0, sel_start, jnp.where(lane == 1, sel_tot, 0.0))
    rows = cols.T
    start_row = rows[0:1, 0:nb]
    tot_row = rows[1:2, 0:nb]

    p = lax.broadcasted_iota(jnp.int32, (cap, 1), 0).astype(F32)
    onehot = jnp.where(jnp.logical_and(start_row <= p, p < start_row + tot_row), 1.0, 0.0)
    blk = lax.broadcasted_iota(jnp.int32, (1, nb), 1).astype(F32)
    p_start = jnp.sum(onehot * start_row, axis=1, keepdims=True)
    p_blk = jnp.sum(onehot * blk, axis=1, keepdims=True)
    ohb = onehot.astype(BF16)
    local_rows = _dot(ohb, sel_local.astype(BF16))
    sel_rows = _dot(ohb, self_.astype(BF16))
    earlier_rows = _dot(ohb, earlier.astype(BF16))
    aff_rows = lax.dot_general(onehot, a, (((1,), (0,)), ((), ())), precision=HIGHEST, preferred_element_type=F32)
    hit = jnp.where(jnp.logical_and(local_rows == p - p_start + 1.0, sel_rows > 0.5), 1.0, 0.0)
    lane_f = lax.broadcasted_iota(jnp.int32, (1, LANES), 1).astype(F32)
    tok = p_blk * LANES + jnp.sum(hit * lane_f, axis=1, keepdims=True)
    gate = jnp.sum(hit * aff_rows, axis=1, keepdims=True)
    rank = jnp.sum(hit * earlier_rows, axis=1, keepdims=True)
    idx_ref[0] = tok.astype(jnp.int32) + row_off
    gate_ref[0] = gate
    dest_ref[0] = (rank * dest_stride + tok).astype(jnp.int32) + row_off

    @pl.when(e == ne - 1)
    def _():
        cnt_ref[...] = cnt_acc[...].astype(jnp.int32)


def _route(aff_t, cap, dest_stride, row_off):
    ne, n = aff_t.shape
    n_pad = TOPK_BLOCKS * LANES
    assert n <= n_pad
    a = jnp.pad(aff_t, ((0, 0), (0, n_pad - n)), constant_values=-1.0).reshape(ne, TOPK_BLOCKS, LANES)
    slot = pl.BlockSpec((1, cap, 1), lambda e: (e, 0, 0))
    idx, gate, dest, cnt = pl.pallas_call(
        functools.partial(_topk_kernel, cap=cap, dest_stride=dest_stride, row_off=row_off),
        grid=(ne,),
        in_specs=[pl.BlockSpec((1, TOPK_BLOCKS, LANES), lambda e: (e, 0, 0))],
        out_specs=[slot, slot, slot, pl.BlockSpec((TOPK_BLOCKS, LANES), lambda e: (0, 0))],
        out_shape=[jax.ShapeDtypeStruct((ne, cap, 1), jnp.int32), jax.ShapeDtypeStruct((ne, cap, 1), F32),
                   jax.ShapeDtypeStruct((ne, cap, 1), jnp.int32), jax.ShapeDtypeStruct((TOPK_BLOCKS, LANES), jnp.int32)],
        scratch_shapes=[pltpu.VMEM((TOPK_BLOCKS, LANES), F32)],
        compiler_params=_cparams(("arbitrary",)),
        name="route_topk",
    )(a)
    return idx.reshape(ne * cap), gate, dest.reshape(ne * cap), cnt.reshape(n_pad)[:n]


def _row_copy(src, s, dst, t, sem):
    return pltpu.make_async_copy(src.at[pl.ds(s, 1)], dst.at[pl.ds(t, 1)], sem)


def _ffn_body(idx_sm, dest_sm, hl_hbm, gate_ref, wg_ref, wu_ref, wd_ref, z_hbm, xbuf, xb, acc, sem, *, tm, cap):
    e = pl.program_id(0)
    m = pl.program_id(1)
    f = pl.program_id(2)
    nf = pl.num_programs(2)
    base = e * cap + m * tm

    @pl.when(f == 0)
    def _():
        def start(r, c):
            _row_copy(hl_hbm, idx_sm[base + r], xbuf, r, sem).start()
            return c
        lax.fori_loop(0, tm, start, 0)

        def wait(r, c):
            _row_copy(hl_hbm, idx_sm[base + r], xbuf, r, sem).wait()
            return c
        lax.fori_loop(0, tm, wait, 0)
        xb[...] = xbuf[...].astype(BF16)
        acc[...] = jnp.zeros(acc.shape, F32)

    x = xb[...]
    g = _dot(x, wg_ref[0].astype(BF16))
    u = _dot(x, wu_ref[0].astype(BF16))
    hid = (g * jax.nn.sigmoid(g) * u).astype(BF16)
    acc[...] += _dot(hid, wd_ref[0].astype(BF16))

    @pl.when(f == nf - 1)
    def _():
        xbuf[...] = acc[...] * gate_ref[0]

        def start(r, c):
            _row_copy(xbuf, r, z_hbm, dest_sm[base + r], sem).start()
            return c
        lax.fori_loop(0, tm, start, 0)

        def wait(r, c):
            _row_copy(xbuf, r, z_hbm, dest_sm[base + r], sem).wait()
            return c
        lax.fori_loop(0, tm, wait, 0)


def _ffn_kernel_new(idx_sm, dest_sm, hl_hbm, gate_ref, wg_ref, wu_ref, wd_ref, z_hbm, xbuf, xb, acc, sem, **kw):
    _ffn_body(idx_sm, dest_sm, hl_hbm, gate_ref, wg_ref, wu_ref, wd_ref, z_hbm, xbuf, xb, acc, sem, **kw)


def _ffn_kernel_alias(idx_sm, dest_sm, hl_hbm, gate_ref, wg_ref, wu_ref, wd_ref, z_in, z_hbm, xbuf, xb, acc, sem, **kw):
    del z_in
    _ffn_body(idx_sm, dest_sm, hl_hbm, gate_ref, wg_ref, wu_ref, wd_ref, z_hbm, xbuf, xb, acc, sem, **kw)


def _expert_ffn(idx, dest, hl, gate, w_gate, w_up, w_down, z_rows, z=None):
    ne, d, ff = w_gate.shape
    cap = gate.shape[1]
    tm = min(cap, 1024)
    tf = 256
    any_spec = pl.BlockSpec(memory_space=pl.ANY)
    in_specs = [any_spec,
                pl.BlockSpec((1, tm, 1), lambda e, m, f, *_: (e, m, 0)),
                pl.BlockSpec((1, d, tf), lambda e, m, f, *_: (e, 0, f)),
                pl.BlockSpec((1, d, tf), lambda e, m, f, *_: (e, 0, f)),
                pl.BlockSpec((1, tf, d), lambda e, m, f, *_: (e, f, 0))]
    args = [idx, dest, hl, gate, w_gate, w_up, w_down]
    kw = dict(tm=tm, cap=cap)
    if z is None:
        kern, aliases = functools.partial(_ffn_kernel_new, **kw), {}
    else:
        kern, aliases = functools.partial(_ffn_kernel_alias, **kw), {len(args): 0}
        in_specs.append(any_spec)
        args.append(z)
    return pl.pallas_call(
        kern,
        grid_spec=pltpu.PrefetchScalarGridSpec(
            num_scalar_prefetch=2,
            grid=(ne, cap // tm, ff // tf),
            in_specs=in_specs,
            out_specs=any_spec,
            scratch_shapes=[pltpu.VMEM((tm, d), F32), pltpu.VMEM((tm, d), BF16), pltpu.VMEM((tm, d), F32),
                            pltpu.SemaphoreType.DMA]),
        out_shape=jax.ShapeDtypeStruct((z_rows, d), F32),
        input_output_aliases=aliases,
        compiler_params=_cparams(("arbitrary", "arbitrary", "arbitrary")),
        name="expert_ffn",
    )(*args)


def _combine_kernel(maxc_sm, z_ref, cnt_ref, x_ref, mod_ref, g_ref, o_ref, acc, *, n_lat_tiles, d):
    i = pl.program_id(0)
    k = pl.program_id(1)
    nk = pl.num_programs(1)

    @pl.when(k == 0)
    def _():
        acc[...] = jnp.zeros(acc.shape, F32)

    @pl.when(k < maxc_sm[i])
    def _():
        acc[...] += jnp.where(cnt_ref[...] > k, z_ref[...], 0.0)

    @pl.when(k == nk - 1)
    def _():
        r = jnp.where(i >= n_lat_tiles, 1, 0)
        o_ref[...] = x_ref[...] + _mod_rows(mod_ref, r, 5, d) * _rms(acc[...], g_ref[...])


def _combine(z, cnt, x, mod, g_post, n_rows, z_stride, n_lat_tiles_tb):
    d = x.shape[1]
    tb = 256
    nb = n_rows // tb
    zb = z_stride // tb
    maxc = jnp.max(cnt[:n_rows].reshape(nb, tb), axis=1)

    def z_map(i, k, maxc_sm):
        return (jnp.maximum(jnp.minimum(k, maxc_sm[i] - 1), 0) * zb + i, 0)

    return pl.pallas_call(
        functools.partial(_combine_kernel, n_lat_tiles=n_lat_tiles_tb, d=d),
        grid_spec=pltpu.PrefetchScalarGridSpec(
            num_scalar_prefetch=1,
            grid=(nb, N_EXPERTS),
            in_specs=[pl.BlockSpec((tb, d), z_map),
                      pl.BlockSpec((tb, 1), lambda i, k, s: (i, 0)),
                      pl.BlockSpec((tb, d), lambda i, k, s: (i, 0)),
                      pl.BlockSpec((8, N_MOD * d), lambda i, k, s: (0, 0)),
                      pl.BlockSpec((1, d), lambda i, k, s: (0, 0))],
            out_specs=pl.BlockSpec((tb, d), lambda i, k, s: (i, 0)),
            scratch_shapes=[pltpu.VMEM((tb, d), F32)]),
        out_shape=jax.ShapeDtypeStruct((n_rows, d), F32),
        compiler_params=_cparams(("parallel", "arbitrary")),
        name="moe_combine",
    )(maxc, z, cnt.reshape(-1, 1), x, mod, g_post.reshape(1, d))


def _moe(hl, aff_t, x_mid, mod, p, n_lat, n_ctx, s_pad, with_ctx):
    cap_l = EC_CAPACITY_FACTOR * n_lat // N_EXPERTS
    idx, gate, dest, cnt = _route(aff_t[:, :n_lat], cap_l, s_pad, 0)
    z = _expert_ffn(idx, dest, hl, gate, p['w_gate'], p['w_up'], p['w_down'], N_EXPERTS * s_pad)
    n_rows = n_lat
    if with_ctx:
        cap_c = EC_CAPACITY_FACTOR * n_ctx // N_EXPERTS
        idx_c, gate_c, dest_c, cnt_c = _route(aff_t[:, n_lat:n_lat + n_ctx], cap_c, s_pad, n_lat)
        z = _expert_ffn(idx_c, dest_c, hl, gate_c, p['w_gate'], p['w_up'], p['w_down'], N_EXPERTS * s_pad, z)
        cnt = jnp.concatenate([cnt, cnt_c, jnp.zeros((s_pad - n_lat - n_ctx,), jnp.int32)])
        n_rows = s_pad
    return _combine(z, cnt, x_mid, mod, p['g_post_ffn'], n_rows, s_pad, n_lat // 256)


def _proj1_kernel(x_ref, mod_ref, gpre_ref, w_ref, wg_ref, bg_ref, o_ref, g_ref, h_scr, *, n_lat_tiles, d):
    i = pl.program_id(0)
    j = pl.program_id(1)

    @pl.when(j == 0)
    def _():
        r = jnp.where(i >= n_lat_tiles, 1, 0)
        h = _premix(x_ref, mod_ref, gpre_ref, r, d)
        h_scr[...] = h.astype(BF16)
        g_ref[...] = lax.dot_general(h, wg_ref[...], (((1,), (0,)), ((), ())), precision=HIGHEST,
                                     preferred_element_type=F32) + bg_ref[...]

    o_ref[...] = _dot(h_scr[...], w_ref[...]).astype(BF16)


def _proj1(xs, mod, g_pre, w_main, w_gates, b_gate, n_lat_tiles):
    s_pad, d = xs.shape
    n_out = w_main.shape[1]
    ng = w_gates.shape[1]
    tm, tn = ROW_TILE, 512
    row = lambda i, j: (i, 0)
    const = lambda i, j: (0, 0)
    return pl.pallas_call(
        functools.partial(_proj1_kernel, n_lat_tiles=n_lat_tiles, d=d),
        grid=(s_pad // tm, n_out // tn),
        in_specs=[pl.BlockSpec((tm, d), row),
                  pl.BlockSpec((8, N_MOD * d), const),
                  pl.BlockSpec((1, d), const),
                  pl.BlockSpec((d, tn), lambda i, j: (0, j)),
                  pl.BlockSpec((d, ng), const),
                  pl.BlockSpec((1, ng), const)],
        out_specs=[pl.BlockSpec((tm, tn), lambda i, j: (i, j)), pl.BlockSpec((tm, ng), row)],
        out_shape=[jax.ShapeDtypeStruct((s_pad, n_out), BF16), jax.ShapeDtypeStruct((s_pad, ng), F32)],
        scratch_shapes=[pltpu.VMEM((tm, d), BF16)],
        compiler_params=_cparams(("parallel", "arbitrary")),
        name="proj1",
    )(xs, mod, g_pre.reshape(1, d), w_main, w_gates, b_gate.reshape(1, ng))


def _conv_kernel(cur_ref, prev_ref, next_ref, w_ref, b_ref, o_ref, *, n_lat, n_ctx, k_cols0, k_scale):
    i = pl.program_id(0)
    j = pl.program_id(1)
    tm, tc = cur_ref.shape
    hr = prev_ref.shape[0]
    cur = cur_ref[...].astype(F32)
    row = lax.broadcasted_iota(jnp.int32, (tm, 1), 0)
    g = i * tm + row
    prev_row = prev_ref[hr - 1:hr, :].astype(F32)
    next_row = next_ref[0:1, :].astype(F32)
    before = jnp.where(row == 0, prev_row, pltpu.roll(cur, 1, 0))
    after = jnp.where(row == tm - 1, next_row, pltpu.roll(cur, tm - 1, 0))
    before = jnp.where(jnp.logical_or(g == 0, g == n_lat), 0.0, before)
    after = jnp.where(jnp.logical_or(g == n_lat - 1, g == n_lat + n_ctx - 1), 0.0, after)
    y = before * w_ref[0:1, :] + cur * w_ref[1:2, :] + after * w_ref[2:3, :] + b_ref[...]
    y = y * jax.nn.sigmoid(y)
    o_ref[...] = (y * jnp.where(j * tc >= k_cols0, k_scale, 1.0)).astype(BF16)


def _conv_silu(u, w_conv, b_conv, n_cols, n_lat, n_ctx):
    s_pad = u.shape[0]
    tm, tc, hr = ROW_TILE, 1024, 16
    rb = tm // hr
    last = s_pad // hr - 1
    return pl.pallas_call(
        functools.partial(_conv_kernel, n_lat=n_lat, n_ctx=n_ctx, k_cols0=n_cols // 2, k_scale=MLSTM_DIM ** -0.5),
        grid=(s_pad // tm, n_cols // tc),
        in_specs=[pl.BlockSpec((tm, tc), lambda i, j: (i, j)),
                  pl.BlockSpec((hr, tc), lambda i, j: (jnp.maximum(i * rb - 1, 0), j)),
                  pl.BlockSpec((hr, tc), lambda i, j: (jnp.minimum((i + 1) * rb, last), j)),
                  pl.BlockSpec((8, tc), lambda i, j: (0, j)),
                  pl.BlockSpec((1, tc), lambda i, j: (0, j))],
        out_specs=pl.BlockSpec((tm, tc), lambda i, j: (i, j)),
        out_shape=jax.ShapeDtypeStruct((s_pad, n_cols), BF16),
        compiler_params=_cparams(("parallel", "parallel")),
        name="conv_silu",
    )(u, u, u, jnp.pad(w_conv, ((0, 8 - w_conv.shape[0]), (0, 0))), b_conv.reshape(1, n_cols))


def _log_sigmoid(x):
    return -(jnp.maximum(-x, 0.0) + jnp.log1p(jnp.exp(-jnp.abs(x))))


def _mlstm_dir(q, k, v, ig_c, fg_c, ig_r, fg_r, c_ref, n_ref, m_ref, o_ref, reverse):
    L = q.shape[0]
    lf_c = _log_sigmoid(fg_c)
    lf_r = _log_sigmoid(fg_r)
    t_i = lax.broadcasted_iota(jnp.int32, (L, L), 0)
    s_i = lax.broadcasted_iota(jnp.int32, (L, L), 1)
    seen = (s_i >= t_i) if reverse else (s_i <= t_i)
    seen_f = jnp.where(seen, 1.0, 0.0)
    b_c = jnp.sum(seen_f * lf_r, axis=1, keepdims=True)
    b_r = jnp.sum(jnp.where(seen, 0.0, 1.0) * lf_c, axis=0, keepdims=True) + lf_r
    m_prev = m_ref[...]
    log_d = jnp.where(seen, b_c - b_r + ig_r, -jnp.inf)
    log_inter = b_c + m_prev
    m_t = jnp.maximum(log_inter, jnp.max(log_d, axis=1, keepdims=True))
    w_intra = jnp.exp(log_d - m_t) * _dot_nt(q, k)
    w_inter = jnp.exp(log_inter - m_t)
    c_prev = c_ref[...]
    n_prev = n_ref[...]
    qf = q.astype(F32)
    num = w_inter * _dot_nt(q, c_prev.astype(BF16)) + _dot(w_intra.astype(BF16), v)
    den = w_inter * jnp.sum(qf * n_prev, axis=1, keepdims=True) + jnp.sum(w_intra, axis=1, keepdims=True)
    o_ref[...] = (num / jnp.maximum(jnp.abs(den), jnp.exp(-m_t))).astype(o_ref.dtype)
    b_end = jnp.sum(lf_c, axis=0, keepdims=True)
    log_w = b_end - b_c + ig_c
    m_new = jnp.maximum(b_end + m_prev, jnp.max(log_w, axis=0, keepdims=True))
    w = jnp.exp(log_w - m_new)
    decay = jnp.exp(b_end + m_prev - m_new)
    wv = (w * v.astype(F32)).astype(BF16)
    c_ref[...] = decay * c_prev + lax.dot_general(wv, k, (((0,), (0,)), ((), ())), preferred_element_type=F32)
    n_ref[...] = decay * n_prev + jnp.sum(w * k.astype(F32), axis=0, keepdims=True)
    m_ref[...] = m_new


def _mlstm_kernel(qf_ref, kf_ref, vf_ref, gcf_ref, grf_ref, qb_ref, kb_ref, vb_ref, gcb_ref, grb_ref,
                  hf_ref, hb_ref, c_scr, n_scr, m_scr):
    @pl.when(pl.program_id(1) == 0)
    def _():
        c_scr[...] = jnp.zeros(c_scr.shape, F32)
        n_scr[...] = jnp.zeros(n_scr.shape, F32)
        m_scr[...] = jnp.zeros(m_scr.shape, F32)

    gc = gcf_ref[0]
    gr = grf_ref[0]
    _mlstm_dir(qf_ref[...], kf_ref[...], vf_ref[...], gc[:, 0:1], gc[:, 1:2], gr[0:1, :], gr[1:2, :],
               c_scr.at[0], n_scr.at[0], m_scr.at[0], hf_ref, False)
    gc = gcb_ref[0]
    gr = grb_ref[0]
    _mlstm_dir(qb_ref[...], kb_ref[...], vb_ref[...], gc[:, 2:3], gc[:, 3:4], gr[2:3, :], gr[3:4, :],
               c_scr.at[1], n_scr.at[1], m_scr.at[1], hb_ref, True)


def _mlstm_scan(qk, u, gates, n_lat, n_ctx):
    s_pad = qk.shape[0]
    L, dh, nh = MLSTM_CHUNK, MLSTM_DIM, MLSTM_HEADS
    nlc, ncc = n_lat // L, n_ctx // L
    steps = nlc + ncc
    g4 = gates.reshape(s_pad, 2, 2, nh).transpose(3, 0, 1, 2).reshape(nh, s_pad, 4)
    g4t = g4.transpose(0, 2, 1)

    def fwd(c):
        return jnp.where(c < ncc, nlc + c, c - ncc)

    def bwd(c):
        return jnp.where(c < ncc, nlc + ncc - 1 - c, nlc - 1 - (c - ncc))

    v_blk0 = 2 * nh
    def specs(pos, qkv_from):
        return [pl.BlockSpec((L, dh), lambda h, c: (pos(c), h)),
                pl.BlockSpec((L, dh), lambda h, c: (pos(c), nh + h)),
                pl.BlockSpec((L, dh), lambda h, c: (pos(c), v_blk0 + h)),
                pl.BlockSpec((1, L, 4), lambda h, c: (h, pos(c), 0)),
                pl.BlockSpec((1, 4, L), lambda h, c: (h, 0, pos(c)))]

    out_shape = jax.ShapeDtypeStruct((steps * L, nh * dh), BF16)
    return pl.pallas_call(
        _mlstm_kernel,
        grid=(nh, steps),
        in_specs=specs(fwd, None) + specs(bwd, None),
        out_specs=[pl.BlockSpec((L, dh), lambda h, c: (fwd(c), h)), pl.BlockSpec((L, dh), lambda h, c: (bwd(c), h))],
        out_shape=[out_shape, out_shape],
        scratch_shapes=[pltpu.VMEM((2, dh, dh), F32), pltpu.VMEM((2, 1, dh), F32), pltpu.VMEM((2, 1, 1), F32)],
        compiler_params=_cparams(("parallel", "arbitrary")),
        name="mlstm_scan",
    )(qk, qk, u, g4, g4t, qk, qk, u, g4, g4t)


def kernel(x, c, ctx, c_ctx, w_mod_0, b_mod_0, g_pre_mix_0, g_post_mix_0, w_in_0, lambda_q1_0, lambda_k1_0, lambda_q2_0, lambda_k2_0, g_subln_0, g_q_0, g_k_0, w_out_0, g_pre_ffn_0, g_post_ffn_0, w_router_0, w_gate_0, w_up_0, w_down_0, w_mod_1, b_mod_1, g_pre_mix_1, g_post_mix_1, w_in_1, b_gate_1, w_conv_1, b_conv_1, g_head_1, w_out_1, g_pre_ffn_1, g_post_ffn_1, w_router_1, w_gate_1, w_up_1, w_down_1):
    b, n_lat, d = x.shape
    n_ctx = ctx.shape[1]
    assert b == 1 and n_lat % ROW_TILE == 0 and n_ctx % MLSTM_CHUNK == 0 and n_ctx <= ROW_TILE
    s_valid = n_lat + n_ctx
    s_pad = n_lat + ROW_TILE
    n_lat_tiles = n_lat // ROW_TILE

    xs = jnp.concatenate([x[0], ctx[0], jnp.zeros((s_pad - s_valid, d), F32)], axis=0)
    cc = jnp.concatenate([c, c_ctx[None, :], jnp.zeros((6, d), F32)], axis=0)

    mod0 = _modulation(cc, w_mod_0, b_mod_0)
    tabs = _rope_tables(n_lat, n_ctx, s_pad)
    p0 = _proj0(xs, mod0, g_pre_mix_0, w_in_0.astype(BF16), tabs, g_q_0, g_k_0, n_lat_tiles)
    lam_init = 0.8 - 0.6 * math.exp(-0.3 * 0)
    lam4 = jnp.stack([lambda_q1_0, lambda_k1_0, lambda_q2_0, lambda_k2_0])
    o_a = _attn_diff(p0, lam4, g_subln_0, n_lat_tiles, s_valid, lam_init)
    o_b = _attn_gqa(p0, n_lat, s_valid)
    ka = o_a.shape[1]
    x_mid, hl, aff_t = _mixer_out(
        _out0_kernel, (o_a, o_b),
        lambda tm: [pl.BlockSpec((tm, ka), lambda i: (i, 0)), pl.BlockSpec((tm, o_b.shape[1]), lambda i: (i, 0))],
        xs, mod0, w_out_0.astype(BF16), g_post_mix_0, g_pre_ffn_0, w_router_0.T, s_pad, n_lat // 256,
        lambda tm: [], "attn_out")
    p_moe0 = dict(w_gate=w_gate_0, w_up=w_up_0, w_down=w_down_0, g_post_ffn=g_post_ffn_0)
    xs = _moe(hl, aff_t, x_mid, mod0, p_moe0, n_lat, n_ctx, s_pad, True)

    mod1 = _modulation(cc, w_mod_1, b_mod_1)
    inner = MLSTM_HEADS * MLSTM_DIM
    u, gates = _proj1(xs, mod1, g_pre_mix_1, w_in_1[:, :4 * inner].astype(BF16), w_in_1[:, 4 * inner:], b_gate_1,
                      n_lat_tiles)
    qk = _conv_silu(u, w_conv_1, b_conv_1, 2 * inner, n_lat, n_ctx)
    hf, hb = _mlstm_scan(qk, u, gates, n_lat, n_ctx)
    og_blk = 3 * inner // inner
    x_mid, hl, aff_t = _mixer_out(
        _out1_kernel, (hf, hb, u, g_head_1.reshape(1, inner)),
        lambda tm: [pl.BlockSpec((tm, inner), lambda i: (i, 0)), pl.BlockSpec((tm, inner), lambda i: (i, 0)),
                    pl.BlockSpec((tm, inner), lambda i: (i, og_blk)), pl.BlockSpec((1, inner), lambda i: (0, 0))],
        xs, mod1, w_out_1.astype(BF16), g_post_mix_1, g_pre_ffn_1, w_router_1.T, n_lat, n_lat // 256,
        lambda tm: [pltpu.VMEM((tm, inner), BF16)], "mlstm_out")
    p_moe1 = dict(w_gate=w_gate_1, w_up=w_up_1, w_down=w_down_1, g_post_ffn=g_post_ffn_1)
    out = _moe(hl, aff_t, x_mid, mod1, p_moe1, n_lat, n_ctx, s_pad, False)
    return out[None]
```

```python
import functools
import math

import jax
import jax.numpy as jnp
from jax import lax
from jax.experimental import pallas as pl
from jax.experimental.pallas import tpu as pltpu

F32 = jnp.float32
BF16 = jnp.bfloat16
HIGHEST = lax.Precision.HIGHEST

GRID_W = 64
ROPE_THETA = 10000.0
RMS_EPS = 1e-6
N_MOD = 6
DIFF_HEADS = 8
DIFF_DIM = 64
GQA_HEADS = 8
GQA_KV_HEADS = 2
GQA_DIM = 128
MLSTM_HEADS = 8
MLSTM_DIM = 256
MLSTM_CHUNK = 128
N_EXPERTS = 16
EC_CAPACITY_FACTOR = 2

LANES = 128
ROW_TILE = 512
FFN_STEPS = 8
ATTN_TK = 512
ATTN_ROW_GROUP = 256
LOG2_E = math.log2(math.e)
TOPK_BLOCKS = 128
VMEM_LIMIT = 56 * 1024 * 1024


def _cparams(sem):
    return pltpu.CompilerParams(dimension_semantics=sem, vmem_limit_bytes=VMEM_LIMIT)


def _rms(x, g):
    return x * lax.rsqrt(jnp.mean(x * x, axis=-1, keepdims=True) + RMS_EPS) * g


def _dot(a, b):
    return jnp.dot(a, b, preferred_element_type=F32)


def _dot_nt(a, b, precision=None):
    return lax.dot_general(a, b, (((1,), (1,)), ((), ())), precision=precision, preferred_element_type=F32)


def _mod_kernel(c_ref, w_ref, b_ref, o_ref):
    c = c_ref[...]
    s = c * jax.nn.sigmoid(c)
    o_ref[...] = lax.dot_general(s, w_ref[...], (((1,), (0,)), ((), ())), precision=HIGHEST,
                                 preferred_element_type=F32) + b_ref[...]


def _modulation(cc, w_mod, b_mod):
    d, n = w_mod.shape
    tn = 1024
    return pl.pallas_call(
        _mod_kernel,
        grid=(n // tn,),
        in_specs=[pl.BlockSpec((8, d), lambda j: (0, 0)),
                  pl.BlockSpec((d, tn), lambda j: (0, j)),
                  pl.BlockSpec((1, tn), lambda j: (0, j))],
        out_specs=pl.BlockSpec((8, tn), lambda j: (0, j)),
        out_shape=jax.ShapeDtypeStruct((8, n), F32),
        compiler_params=_cparams(("arbitrary",)),
        name="modulation",
    )(cc, w_mod, b_mod.reshape(1, n))


def _mod_rows(mod_ref, r, k, d):
    return mod_ref[pl.ds(r, 1), k * d:(k + 1) * d]


def _premix(x_ref, mod_ref, g_ref, r, d):
    x = x_ref[...]
    return _rms(x, g_ref[...]) * (1.0 + _mod_rows(mod_ref, r, 1, d)) + _mod_rows(mod_ref, r, 0, d)


def _rope_a(x, ca, sa, sb):
    return x * ca + pltpu.roll(x, LANES - DIFF_DIM // 2, 1) * sa + pltpu.roll(x, DIFF_DIM // 2, 1) * sb


def _rope_b(x, cb, sb):
    return x * cb + pltpu.roll(x, GQA_DIM // 2, 1) * sb


def _proj0_kernel(x_ref, mod_ref, gpre_ref, w_ref, ca_ref, saa_ref, sab_ref, cb_ref, sbb_ref, gq_ref, gk_ref,
                  o_ref, h_scr, *, n_lat_tiles, d, tn):
    i = pl.program_id(0)
    j = pl.program_id(1)
    ng = tn // LANES

    @pl.when(j == 0)
    def _():
        r = jnp.where(i >= n_lat_tiles, 1, 0)
        h_scr[...] = _premix(x_ref, mod_ref, gpre_ref, r, d).astype(BF16)

    acc = _dot(h_scr[...], w_ref[...])

    def groups(fn):
        for g in range(ng):
            o_ref[:, g * LANES:(g + 1) * LANES] = fn(acc[:, g * LANES:(g + 1) * LANES], g).astype(BF16)

    qa_tiles = DIFF_HEADS * 2 * DIFF_DIM // tn
    qb0 = 3 * qa_tiles
    qb_tiles = GQA_HEADS * GQA_DIM // tn

    @pl.when(j < 2 * qa_tiles)
    def _():
        sc = jnp.where(j < qa_tiles, DIFF_DIM ** -0.5 * LOG2_E, 1.0)
        groups(lambda a, g: _rope_a(a, ca_ref[...], saa_ref[...], sab_ref[...]) * sc)

    @pl.when(jnp.logical_and(j >= 2 * qa_tiles, j < qb0))
    def _():
        groups(lambda a, g: a)

    @pl.when(jnp.logical_and(j >= qb0, j < qb0 + qb_tiles))
    def _():
        groups(lambda a, g: _rope_b(_rms(a, gq_ref[...]), cb_ref[...], sbb_ref[...]) * (GQA_DIM ** -0.5 * LOG2_E))

    @pl.when(j == qb0 + qb_tiles)
    def _():
        def fn(a, g):
            if g < GQA_KV_HEADS:
                return _rope_b(_rms(a, gk_ref[...]), cb_ref[...], sbb_ref[...])
            return a
        groups(fn)


def _proj0(xs, mod, g_pre, w_in, tabs, g_q, g_k, n_lat_tiles):
    s_pad, d = xs.shape
    n_out = w_in.shape[1]
    tm, tn = ROW_TILE, 512
    assert GQA_KV_HEADS * GQA_DIM * 2 == tn
    row = lambda i, j: (i, 0)
    const = lambda i, j: (0, 0)
    tab_spec = pl.BlockSpec((tm, LANES), row)
    return pl.pallas_call(
        functools.partial(_proj0_kernel, n_lat_tiles=n_lat_tiles, d=d, tn=tn),
        grid=(s_pad // tm, n_out // tn),
        in_specs=[pl.BlockSpec((tm, d), row),
                  pl.BlockSpec((8, N_MOD * d), const),
                  pl.BlockSpec((1, d), const),
                  pl.BlockSpec((d, tn), lambda i, j: (0, j)),
                  tab_spec, tab_spec, tab_spec, tab_spec, tab_spec,
                  pl.BlockSpec((1, LANES), const),
                  pl.BlockSpec((1, LANES), const)],
        out_specs=pl.BlockSpec((tm, tn), lambda i, j: (i, j)),
        out_shape=jax.ShapeDtypeStruct((s_pad, n_out), BF16),
        scratch_shapes=[pltpu.VMEM((tm, d), BF16)],
        compiler_params=_cparams(("parallel", "arbitrary")),
        name="proj0",
    )(xs, mod, g_pre.reshape(1, d), w_in, *tabs, g_q.reshape(1, LANES), g_k.reshape(1, LANES))


def _rope_tables(n_lat, n_ctx, s_pad):
    rows = jnp.repeat(jnp.arange(n_lat // GRID_W, dtype=F32), GRID_W)
    cols = jnp.tile(jnp.arange(GRID_W, dtype=F32), n_lat // GRID_W)

    def angles(dim):
        axis_dim = dim // 2
        inv_freq = ROPE_THETA ** (-jnp.arange(0, axis_dim, 2, dtype=F32) / axis_dim)
        ang = jnp.concatenate([rows[:, None] * inv_freq, cols[:, None] * inv_freq], axis=-1)
        return jnp.cos(ang), jnp.sin(ang)

    def pad(t, fill):
        return jnp.concatenate([t, jnp.full((s_pad - n_lat, LANES), fill, F32)], axis=0)

    cos_a, sin_a = angles(DIFF_DIM)
    zero = jnp.zeros_like(sin_a)
    ca = jnp.tile(cos_a, (1, 4))
    saa = jnp.tile(jnp.concatenate([-sin_a, zero], axis=1), (1, 2))
    sab = jnp.tile(jnp.concatenate([zero, sin_a], axis=1), (1, 2))
    cos_b, sin_b = angles(GQA_DIM)
    cb = jnp.tile(cos_b, (1, 2))
    sbb = jnp.concatenate([-sin_b, sin_b], axis=1)
    return pad(ca, 1.0), pad(saa, 0.0), pad(sab, 0.0), pad(cb, 1.0), pad(sbb, 0.0)


def _flash_rows(q, k_ref, v_ref, m_scr, acc_scr, s_scr, is_ctx_q, tk, s_valid):
    n_chunks = k_ref.shape[0] // tk
    last = n_chunks - 1
    groups = [slice(g * ATTN_ROW_GROUP, (g + 1) * ATTN_ROW_GROUP) for g in range(q.shape[0] // ATTN_ROW_GROUP)]
    m_scr[...] = jnp.full(m_scr.shape, -jnp.inf, F32)
    acc_scr[...] = jnp.zeros(acc_scr.shape, F32)

    def rows_of(c):
        return pl.ds(c * tk if isinstance(c, int) else pl.multiple_of(c * tk, tk), tk)

    def scores(c, buf, rows):
        s_scr[buf, rows] = _dot_nt(q[rows], k_ref[rows_of(c), :])

    def consume(c, buf, rows, v1, masked):
        s = s_scr[buf, rows]
        if masked:
            kpos = c * tk + lax.broadcasted_iota(jnp.int32, (1, tk), 1)
            s = jnp.where(kpos < s_valid, s, -jnp.inf)
        m_prev = m_scr[rows]
        m_new = jnp.maximum(m_prev, jnp.max(s, axis=1, keepdims=True))
        p = jnp.exp2(s - m_new).astype(BF16)
        acc_scr[rows] = jnp.exp2(m_prev - m_new) * acc_scr[rows] + _dot(p, v1)
        m_scr[rows] = m_new

    def values(c):
        vc = v_ref[rows_of(c), :]
        return jnp.concatenate([vc, jnp.ones_like(vc)], axis=1)

    unroll = max(u for u in (2, 4, 8) if last % u == 0)
    lo = jnp.where(is_ctx_q, last, 0)
    for rows in groups:
        scores(lo, 0, rows)

    def body(i, carry):
        c = lo + unroll * i
        for j in range(unroll):
            v1 = values(c + j)
            for rows in groups:
                scores(c + j + 1, (j + 1) % 2, rows)
                consume(c + j, j % 2, rows, v1, False)
        return carry

    lax.fori_loop(0, (last - lo) // unroll, body, 0)
    v1 = values(last)
    for rows in groups:
        consume(last, 0, rows, v1, True)


def _attn_diff_kernel(q_ref, k_ref, v_ref, lam_ref, gsub_ref, o_ref, m_scr, acc_scr, s_scr,
                      *, n_lat_tiles, tk, s_valid, lam_init):
    q = q_ref[...]
    tq, hd = q.shape
    first = lax.broadcasted_iota(jnp.int32, (1, hd), 1) < DIFF_DIM
    zero = jnp.zeros_like(q)
    qq = jnp.concatenate([jnp.where(first, q, zero), jnp.where(first, zero, q)], axis=0)
    _flash_rows(qq, k_ref, v_ref, m_scr, acc_scr, s_scr, pl.program_id(1) >= n_lat_tiles, tk, s_valid)
    lam = (jnp.exp(jnp.sum(lam_ref[0:1, :] * lam_ref[1:2, :], axis=1, keepdims=True))
           - jnp.exp(jnp.sum(lam_ref[2:3, :] * lam_ref[3:4, :], axis=1, keepdims=True)) + lam_init)
    o = (acc_scr[0:tq, 0:hd] / acc_scr[0:tq, hd:hd + 1]
         - lam * (acc_scr[tq:2 * tq, 0:hd] / acc_scr[tq:2 * tq, hd:hd + 1]))
    o_ref[...] = (_rms(o, gsub_ref[...]) * (1.0 - lam_init)).astype(BF16)


def _attn_diff(p0, lam4, g_subln, n_lat_tiles, s_valid, lam_init):
    s_pad = p0.shape[0]
    tq = ROW_TILE
    hd = 2 * DIFF_DIM
    k0 = DIFF_HEADS
    v0 = 2 * DIFF_HEADS
    return pl.pallas_call(
        functools.partial(_attn_diff_kernel, n_lat_tiles=n_lat_tiles, tk=ATTN_TK, s_valid=s_valid, lam_init=lam_init),
        grid=(DIFF_HEADS, s_pad // tq),
        in_specs=[pl.BlockSpec((tq, hd), lambda h, iq: (iq, h)),
                  pl.BlockSpec((s_pad, hd), lambda h, iq: (0, k0 + h)),
                  pl.BlockSpec((s_pad, hd), lambda h, iq: (0, v0 + h)),
                  pl.BlockSpec((4, DIFF_DIM), lambda h, iq: (0, 0)),
                  pl.BlockSpec((1, hd), lambda h, iq: (0, 0))],
        out_specs=pl.BlockSpec((tq, hd), lambda h, iq: (iq, h)),
        out_shape=jax.ShapeDtypeStruct((s_pad, DIFF_HEADS * hd), BF16),
        scratch_shapes=[pltpu.VMEM((2 * tq, 1), F32), pltpu.VMEM((2 * tq, 2 * hd), F32),
                        pltpu.VMEM((2, 2 * tq, ATTN_TK), F32)],
        compiler_params=_cparams(("parallel", "arbitrary")),
        name="attn_diff",
    )(p0, p0, p0, lam4, g_subln.reshape(1, hd))


def _attn_gqa_kernel(q_ref, k_ref, v_ref, o_ref, m_scr, acc_scr, s_scr, *, n_lat_tiles, tk, s_valid, group):
    tq = q_ref.shape[0]
    dh = GQA_DIM
    qq = jnp.concatenate([q_ref[:, g * dh:(g + 1) * dh] for g in range(group)], axis=0)
    _flash_rows(qq, k_ref, v_ref, m_scr, acc_scr, s_scr, pl.program_id(1) >= n_lat_tiles, tk, s_valid)
    for g in range(group):
        rows = slice(g * tq, (g + 1) * tq)
        o_ref[:, g * dh:(g + 1) * dh] = (acc_scr[rows, 0:dh] / acc_scr[rows, dh:dh + 1]).astype(BF16)


def _attn_gqa(p0, n_lat, s_valid):
    s_pad = p0.shape[0]
    tq = 256
    group = GQA_HEADS // GQA_KV_HEADS
    qw = group * GQA_DIM
    q0 = 3 * DIFF_HEADS * 2 * DIFF_DIM // qw
    k0 = (3 * DIFF_HEADS * 2 * DIFF_DIM + GQA_HEADS * GQA_DIM) // GQA_DIM
    v0 = k0 + GQA_KV_HEADS
    return pl.pallas_call(
        functools.partial(_attn_gqa_kernel, n_lat_tiles=n_lat // tq, tk=ATTN_TK, s_valid=s_valid, group=group),
        grid=(GQA_KV_HEADS, s_pad // tq),
        in_specs=[pl.BlockSpec((tq, qw), lambda h, iq: (iq, q0 + h)),
                  pl.BlockSpec((s_pad, GQA_DIM), lambda h, iq: (0, k0 + h)),
                  pl.BlockSpec((s_pad, GQA_DIM), lambda h, iq: (0, v0 + h))],
        out_specs=pl.BlockSpec((tq, qw), lambda h, iq: (iq, h)),
        out_shape=jax.ShapeDtypeStruct((s_pad, GQA_HEADS * GQA_DIM), BF16),
        scratch_shapes=[pltpu.VMEM((group * tq, 1), F32), pltpu.VMEM((group * tq, 2 * GQA_DIM), F32),
                        pltpu.VMEM((2, group * tq, ATTN_TK), F32)],
        compiler_params=_cparams(("parallel", "arbitrary")),
        name="attn_gqa",
    )(p0, p0, p0)


def _post_mix(y, x_ref, mod_ref, r, gpost_ref, gffn_ref, wr_ref, xo_ref, hl_ref, aff_ref, d):
    xn = x_ref[...] + _mod_rows(mod_ref, r, 2, d) * _rms(y, gpost_ref[...])
    xo_ref[...] = xn
    hl = _rms(xn, gffn_ref[...]) * (1.0 + _mod_rows(mod_ref, r, 4, d)) + _mod_rows(mod_ref, r, 3, d)
    hl_ref[...] = hl
    wr = wr_ref[...]
    h_hi = hl.astype(BF16)
    h_lo = (hl - h_hi.astype(F32)).astype(BF16)
    w_hi = wr.astype(BF16)
    w_lo = (wr - w_hi.astype(F32)).astype(BF16)
    logits = _dot(h_hi, w_hi) + (_dot(h_hi, w_lo) + _dot(h_lo, w_hi))
    e = jnp.exp(logits - jnp.max(logits, axis=1, keepdims=True))
    aff_ref[...] = e / jnp.sum(e, axis=1, keepdims=True)


def _out0_kernel(oa_ref, ob_ref, x_ref, mod_ref, w_ref, gpost_ref, gffn_ref, wr_ref, xo_ref, hl_ref, aff_ref,
                 *, n_lat_tiles, d):
    r = jnp.where(pl.program_id(0) >= n_lat_tiles, 1, 0)
    ka = oa_ref.shape[1]
    y = _dot(oa_ref[...], w_ref[0:ka, :]) + _dot(ob_ref[...], w_ref[ka:, :])
    _post_mix(y, x_ref, mod_ref, r, gpost_ref, gffn_ref, wr_ref, xo_ref, hl_ref, aff_ref, d)


def _out1_kernel(hf_ref, hb_ref, og_ref, ghead_ref, x_ref, mod_ref, w_ref, gpost_ref, gffn_ref, wr_ref,
                 xo_ref, hl_ref, aff_ref, y_scr, *, n_lat_tiles, d):
    r = jnp.where(pl.program_id(0) >= n_lat_tiles, 1, 0)
    for h in range(MLSTM_HEADS):
        sl = slice(h * MLSTM_DIM, (h + 1) * MLSTM_DIM)
        hs = hf_ref[:, sl].astype(F32) + hb_ref[:, sl].astype(F32)
        hn = _rms(hs, ghead_ref[:, sl])
        y_scr[:, sl] = (jax.nn.sigmoid(og_ref[:, sl].astype(F32)) * hn).astype(BF16)
    y = _dot(y_scr[...], w_ref[...])
    _post_mix(y, x_ref, mod_ref, r, gpost_ref, gffn_ref, wr_ref, xo_ref, hl_ref, aff_ref, d)


def _mixer_out(kernel_fn, mix_inputs, mix_specs, x, mod, w_out, g_post, g_ffn, w_router, n_rows, n_lat_tiles_tm,
               scratch, name):
    d = x.shape[1]
    tm = 256
    row = lambda i: (i, 0)
    const = lambda i: (0, 0)
    ne = w_router.shape[1]
    return pl.pallas_call(
        functools.partial(kernel_fn, n_lat_tiles=n_lat_tiles_tm, d=d),
        grid=(n_rows // tm,),
        in_specs=mix_specs(tm) + [pl.BlockSpec((tm, d), row),
                                  pl.BlockSpec((8, N_MOD * d), const),
                                  pl.BlockSpec(w_out.shape, const),
                                  pl.BlockSpec((1, d), const),
                                  pl.BlockSpec((1, d), const),
                                  pl.BlockSpec((d, ne), const)],
        out_specs=[pl.BlockSpec((tm, d), row), pl.BlockSpec((tm, d), row), pl.BlockSpec((tm, ne), row)],
        out_shape=[jax.ShapeDtypeStruct((n_rows, d), F32), jax.ShapeDtypeStruct((n_rows, d), F32),
                   jax.ShapeDtypeStruct((n_rows, ne), F32)],
        scratch_shapes=scratch(tm),
        compiler_params=_cparams(("parallel",)),
        name=name,
    )(*mix_inputs, x, mod, w_out, g_post.reshape(1, d), g_ffn.reshape(1, d), w_router)


def _incl_cumsum(maskf, upper, strict_lower):
    mb = maskf.astype(BF16)
    local = _dot(mb, upper)
    start = jnp.sum(_dot(strict_lower, mb), axis=1, keepdims=True)
    return local, start


def _topk_kernel(aff_ref, idx_ref, gate_ref, dest_ref, cnt_ref, cnt_acc, *, cap, dest_stride, row_off):
    e = pl.program_id(0)
    ne = pl.num_programs(0)
    nb = aff_ref.shape[1]
    a = aff_ref[0]
    bits = pltpu.bitcast(a, jnp.int32)

    thr = jnp.zeros((1, 1), jnp.int32)
    for bit in range(30, -1, -1):
        cand = thr | (1 << bit)
        c = jnp.sum(jnp.sum(jnp.where(bits >= cand, 1.0, 0.0), axis=1, keepdims=True), axis=0, keepdims=True)
        thr = jnp.where(c >= cap, cand, thr)

    r_i = lax.broadcasted_iota(jnp.int32, (LANES, LANES), 0)
    c_i = lax.broadcasted_iota(jnp.int32, (LANES, LANES), 1)
    upper = jnp.where(r_i <= c_i, 1.0, 0.0).astype(BF16)
    rb = lax.broadcasted_iota(jnp.int32, (nb, nb), 0)
    cb = lax.broadcasted_iota(jnp.int32, (nb, nb), 1)
    strict_lower = jnp.where(cb < rb, 1.0, 0.0).astype(BF16)

    gt = bits > thr
    eq = bits == thr
    n_gt = jnp.sum(jnp.sum(jnp.where(gt, 1.0, 0.0), axis=1, keepdims=True), axis=0, keepdims=True)
    eq_local, eq_start = _incl_cumsum(jnp.where(eq, 1.0, 0.0), upper, strict_lower)
    sel = jnp.logical_or(gt, jnp.logical_and(eq, eq_local + eq_start <= cap - n_gt))
    self_ = jnp.where(sel, 1.0, 0.0)
    sel_local, sel_start = _incl_cumsum(self_, upper, strict_lower)
    sel_tot = jnp.sum(self_, axis=1, keepdims=True)

    @pl.when(e == 0)
    def _():
        cnt_acc[...] = jnp.zeros(cnt_acc.shape, F32)

    earlier = cnt_acc[...]
    cnt_acc[...] = earlier + self_

    lane = lax.broadcasted_iota(jnp.int32, (nb, LANES), 1)
    cols = jnp.where(lane == 0, sel_start, jnp.where(lane == 1, sel_tot, 0.0))
    rows = cols.T
    start_row = rows[0:1, 0:nb]
    tot_row = rows[1:2, 0:nb]

    p = lax.broadcasted_iota(jnp.int32, (cap, 1), 0).astype(F32)
    onehot = jnp.where(jnp.logical_and(start_row <= p, p < start_row + tot_row), 1.0, 0.0)
    blk = lax.broadcasted_iota(jnp.int32, (1, nb), 1).astype(F32)
    p_start = jnp.sum(onehot * start_row, axis=1, keepdims=True)
    p_blk = jnp.sum(onehot * blk, axis=1, keepdims=True)
    ohb = onehot.astype(BF16)
    local_rows = _dot(ohb, sel_local.astype(BF16))
    sel_rows = _dot(ohb, self_.astype(BF16))
    earlier_rows = _dot(ohb, earlier.astype(BF16))
    aff_rows = lax.dot_general(onehot, a, (((1,), (0,)), ((), ())), precision=HIGHEST, preferred_element_type=F32)
    hit = jnp.where(jnp.logical_and(local_rows == p - p_start + 1.0, sel_rows > 0.5), 1.0, 0.0)
    lane_f = lax.broadcasted_iota(jnp.int32, (1, LANES), 1).astype(F32)
    tok = p_blk * LANES + jnp.sum(hit * lane_f, axis=1, keepdims=True)
    gate = jnp.sum(hit * aff_rows, axis=1, keepdims=True)
    rank = jnp.sum(hit * earlier_rows, axis=1, keepdims=True)
    idx_ref[0] = tok.astype(jnp.int32) + row_off
    gate_ref[0] = gate
    dest_ref[0] = (rank * dest_stride + tok).astype(jnp.int32) + row_off

    @pl.when(e == ne - 1)
    def _():
        cnt_ref[...] = cnt_acc[...].astype(jnp.int32)


def _route(aff_t, cap, dest_stride, row_off):
    ne, n = aff_t.shape
    n_pad = TOPK_BLOCKS * LANES
    assert n <= n_pad
    a = jnp.pad(aff_t, ((0, 0), (0, n_pad - n)), constant_values=-1.0).reshape(ne, TOPK_BLOCKS, LANES)
    slot = pl.BlockSpec((1, cap, 1), lambda e: (e, 0, 0))
    idx, gate, dest, cnt = pl.pallas_call(
        functools.partial(_topk_kernel, cap=cap, dest_stride=dest_stride, row_off=row_off),
        grid=(ne,),
        in_specs=[pl.BlockSpec((1, TOPK_BLOCKS, LANES), lambda e: (e, 0, 0))],
        out_specs=[slot, slot, slot, pl.BlockSpec((TOPK_BLOCKS, LANES), lambda e: (0, 0))],
        out_shape=[jax.ShapeDtypeStruct((ne, cap, 1), jnp.int32), jax.ShapeDtypeStruct((ne, cap, 1), F32),
                   jax.ShapeDtypeStruct((ne, cap, 1), jnp.int32), jax.ShapeDtypeStruct((TOPK_BLOCKS, LANES), jnp.int32)],
        scratch_shapes=[pltpu.VMEM((TOPK_BLOCKS, LANES), F32)],
        compiler_params=_cparams(("arbitrary",)),
        name="route_topk",
    )(a)
    return idx.reshape(ne * cap), gate, dest.reshape(ne * cap), cnt.reshape(n_pad)[:n]


def _row_copy(src, s, dst, t, sem):
    return pltpu.make_async_copy(src.at[pl.ds(s, 1)], dst.at[pl.ds(t, 1)], sem)


def _ffn_body(idx_sm, dest_sm, hl_hbm, gate_ref, wg_ref, wu_ref, wd_ref, z_hbm, xbuf, xb, acc, gsem, ssem, *, tm):
    f = pl.program_id(2)
    nm = pl.num_programs(1)
    t = pl.program_id(0) * nm + pl.program_id(1)
    n_tiles = pl.num_programs(0) * nm
    cur = t % 2
    nxt = 1 - cur
    chunk = tm // FFN_STEPS
    last_step = f == pl.num_programs(2) - 1
    t_next = jnp.minimum(t + 1, n_tiles - 1)
    t_prev = jnp.maximum(t - 1, 0)

    def gather(tile, slot, r):
        return _row_copy(hl_hbm, idx_sm[tile * tm + r], xbuf.at[slot], r, gsem.at[slot])

    def scatter(tile, slot, r):
        return _row_copy(acc.at[slot], r, z_hbm, dest_sm[tile * tm + r], ssem.at[slot])

    def for_rows(fn):
        def body(r, c):
            fn(r)
            return c
        lax.fori_loop(0, tm, body, 0, unroll=8)

    @pl.when(jnp.logical_and(t == 0, f == 0))
    def _():
        for_rows(lambda r: gather(0, 0, r).start())
        acc[1] = jnp.zeros(acc.shape[1:], F32)

    @pl.when(f == 0)
    def _():
        for_rows(lambda r: gather(t, cur, r).wait())
        xb[...] = xbuf[cur].astype(BF16)

        @pl.when(t >= 1)
        def _():
            for_rows(lambda r: scatter(t_prev, cur, r).wait())
        acc[cur] = jnp.zeros(acc.shape[1:], F32)

    for r in range(chunk):
        gather(t_next, nxt, f * chunk + r).start()
        scatter(t_prev, nxt, f * chunk + r).start()

    x = xb[...]
    g = _dot(x, wg_ref[0].astype(BF16))
    u = _dot(x, wu_ref[0].astype(BF16))
    hid = (g * jax.nn.sigmoid(g) * u).astype(BF16)
    acc[cur] += _dot(hid, wd_ref[0].astype(BF16))

    @pl.when(last_step)
    def _():
        acc[cur] = acc[cur] * gate_ref[0]

    @pl.when(jnp.logical_and(t == n_tiles - 1, last_step))
    def _():
        for_rows(lambda r: gather(t, nxt, r).wait())
        for_rows(lambda r: scatter(t_prev, nxt, r).wait())
        for_rows(lambda r: scatter(t, cur, r).start())
        for_rows(lambda r: scatter(t, cur, r).wait())


def _ffn_kernel_new(idx_sm, dest_sm, hl_hbm, gate_ref, wg_ref, wu_ref, wd_ref, z_hbm, *scratch, tm):
    _ffn_body(idx_sm, dest_sm, hl_hbm, gate_ref, wg_ref, wu_ref, wd_ref, z_hbm, *scratch, tm=tm)


def _ffn_kernel_alias(idx_sm, dest_sm, hl_hbm, gate_ref, wg_ref, wu_ref, wd_ref, z_in, z_hbm, *scratch, tm):
    del z_in
    _ffn_body(idx_sm, dest_sm, hl_hbm, gate_ref, wg_ref, wu_ref, wd_ref, z_hbm, *scratch, tm=tm)


def _expert_ffn(idx, dest, hl, gate, w_gate, w_up, w_down, z_rows, z=None):
    ne, d, ff = w_gate.shape
    cap = gate.shape[1]
    tm = min(cap, 1024)
    tf = ff // FFN_STEPS
    any_spec = pl.BlockSpec(memory_space=pl.ANY)
    in_specs = [any_spec,
                pl.BlockSpec((1, tm, 1), lambda e, m, f, *_: (e, m, 0)),
                pl.BlockSpec((1, d, tf), lambda e, m, f, *_: (e, 0, f)),
                pl.BlockSpec((1, d, tf), lambda e, m, f, *_: (e, 0, f)),
                pl.BlockSpec((1, tf, d), lambda e, m, f, *_: (e, f, 0))]
    args = [idx, dest, hl, gate, w_gate, w_up, w_down]
    kw = dict(tm=tm)
    if z is None:
        kern, aliases = functools.partial(_ffn_kernel_new, **kw), {}
    else:
        kern, aliases = functools.partial(_ffn_kernel_alias, **kw), {len(args): 0}
        in_specs.append(any_spec)
        args.append(z)
    return pl.pallas_call(
        kern,
        grid_spec=pltpu.PrefetchScalarGridSpec(
            num_scalar_prefetch=2,
            grid=(ne, cap // tm, ff // tf),
            in_specs=in_specs,
            out_specs=any_spec,
            scratch_shapes=[pltpu.VMEM((2, tm, d), F32), pltpu.VMEM((tm, d), BF16), pltpu.VMEM((2, tm, d), F32),
                            pltpu.SemaphoreType.DMA((2,)), pltpu.SemaphoreType.DMA((2,))]),
        out_shape=jax.ShapeDtypeStruct((z_rows, d), F32),
        input_output_aliases=aliases,
        compiler_params=_cparams(("arbitrary", "arbitrary", "arbitrary")),
        name="expert_ffn",
    )(*args)


def _combine_kernel(maxc_sm, z_ref, cnt_ref, x_ref, mod_ref, g_ref, o_ref, acc, *, n_lat_tiles, d):
    i = pl.program_id(0)
    k = pl.program_id(1)
    nk = pl.num_programs(1)

    @pl.when(k == 0)
    def _():
        acc[...] = jnp.zeros(acc.shape, F32)

    @pl.when(k < maxc_sm[i])
    def _():
        acc[...] += jnp.where(cnt_ref[...] > k, z_ref[...], 0.0)

    @pl.when(k == nk - 1)
    def _():
        r = jnp.where(i >= n_lat_tiles, 1, 0)
        o_ref[...] = x_ref[...] + _mod_rows(mod_ref, r, 5, d) * _rms(acc[...], g_ref[...])


def _combine(z, cnt, x, mod, g_post, n_rows, z_stride, n_lat_tiles_tb):
    d = x.shape[1]
    tb = 256
    nb = n_rows // tb
    zb = z_stride // tb
    maxc = jnp.max(cnt[:n_rows].reshape(nb, tb), axis=1)

    def z_map(i, k, maxc_sm):
        return (jnp.maximum(jnp.minimum(k, maxc_sm[i] - 1), 0) * zb + i, 0)

    return pl.pallas_call(
        functools.partial(_combine_kernel, n_lat_tiles=n_lat_tiles_tb, d=d),
        grid_spec=pltpu.PrefetchScalarGridSpec(
            num_scalar_prefetch=1,
            grid=(nb, N_EXPERTS),
            in_specs=[pl.BlockSpec((tb, d), z_map),
                      pl.BlockSpec((tb, 1), lambda i, k, s: (i, 0)),
                      pl.BlockSpec((tb, d), lambda i, k, s: (i, 0)),
                      pl.BlockSpec((8, N_MOD * d), lambda i, k, s: (0, 0)),
                      pl.BlockSpec((1, d), lambda i, k, s: (0, 0))],
            out_specs=pl.BlockSpec((tb, d), lambda i, k, s: (i, 0)),
            scratch_shapes=[pltpu.VMEM((tb, d), F32)]),
        out_shape=jax.ShapeDtypeStruct((n_rows, d), F32),
        compiler_params=_cparams(("parallel", "arbitrary")),
        name="moe_combine",
    )(maxc, z, cnt.reshape(-1, 1), x, mod, g_post.reshape(1, d))


def _moe(hl, aff, x_mid, mod, p, n_lat, n_ctx, s_pad, with_ctx):
    aff_t = aff.T
    cap_l = EC_CAPACITY_FACTOR * n_lat // N_EXPERTS
    idx, gate, dest, cnt = _route(aff_t[:, :n_lat], cap_l, s_pad, 0)
    z = _expert_ffn(idx, dest, hl, gate, p['w_gate'], p['w_up'], p['w_down'], N_EXPERTS * s_pad)
    n_rows = n_lat
    if with_ctx:
        cap_c = EC_CAPACITY_FACTOR * n_ctx // N_EXPERTS
        idx_c, gate_c, dest_c, cnt_c = _route(aff_t[:, n_lat:n_lat + n_ctx], cap_c, s_pad, n_lat)
        z = _expert_ffn(idx_c, dest_c, hl, gate_c, p['w_gate'], p['w_up'], p['w_down'], N_EXPERTS * s_pad, z)
        cnt = jnp.concatenate([cnt, cnt_c, jnp.zeros((s_pad - n_lat - n_ctx,), jnp.int32)])
        n_rows = s_pad
    return _combine(z, cnt, x_mid, mod, p['g_post_ffn'], n_rows, s_pad, n_lat // 256)


def _proj1_kernel(x_ref, mod_ref, gpre_ref, w_ref, wg_ref, bg_ref, o_ref, g_ref, h_scr, *, n_lat_tiles, d):
    i = pl.program_id(0)
    j = pl.program_id(1)

    @pl.when(j == 0)
    def _():
        r = jnp.where(i >= n_lat_tiles, 1, 0)
        h = _premix(x_ref, mod_ref, gpre_ref, r, d)
        h_scr[...] = h.astype(BF16)
        g_ref[...] = lax.dot_general(h, wg_ref[...], (((1,), (0,)), ((), ())), precision=HIGHEST,
                                     preferred_element_type=F32) + bg_ref[...]

    o_ref[...] = _dot(h_scr[...], w_ref[...]).astype(BF16)


def _proj1(xs, mod, g_pre, w_main, w_gates, b_gate, n_lat_tiles):
    s_pad, d = xs.shape
    n_out = w_main.shape[1]
    ng = w_gates.shape[1]
    tm, tn = ROW_TILE, 512
    row = lambda i, j: (i, 0)
    const = lambda i, j: (0, 0)
    return pl.pallas_call(
        functools.partial(_proj1_kernel, n_lat_tiles=n_lat_tiles, d=d),
        grid=(s_pad // tm, n_out // tn),
        in_specs=[pl.BlockSpec((tm, d), row),
                  pl.BlockSpec((8, N_MOD * d), const),
                  pl.BlockSpec((1, d), const),
                  pl.BlockSpec((d, tn), lambda i, j: (0, j)),
                  pl.BlockSpec((d, ng), const),
                  pl.BlockSpec((1, ng), const)],
        out_specs=[pl.BlockSpec((tm, tn), lambda i, j: (i, j)), pl.BlockSpec((tm, ng), row)],
        out_shape=[jax.ShapeDtypeStruct((s_pad, n_out), BF16), jax.ShapeDtypeStruct((s_pad, ng), F32)],
        scratch_shapes=[pltpu.VMEM((tm, d), BF16)],
        compiler_params=_cparams(("parallel", "arbitrary")),
        name="proj1",
    )(xs, mod, g_pre.reshape(1, d), w_main, w_gates, b_gate.reshape(1, ng))


def _conv_kernel(cur_ref, prev_ref, next_ref, w_ref, b_ref, o_ref, *, n_lat, n_ctx, k_cols0, k_scale):
    i = pl.program_id(0)
    j = pl.program_id(1)
    tm, tc = cur_ref.shape
    hr = prev_ref.shape[0]
    cur = cur_ref[...].astype(F32)
    row = lax.broadcasted_iota(jnp.int32, (tm, 1), 0)
    g = i * tm + row
    prev_row = prev_ref[hr - 1:hr, :].astype(F32)
    next_row = next_ref[0:1, :].astype(F32)
    before = jnp.where(row == 0, prev_row, pltpu.roll(cur, 1, 0))
    after = jnp.where(row == tm - 1, next_row, pltpu.roll(cur, tm - 1, 0))
    before = jnp.where(jnp.logical_or(g == 0, g == n_lat), 0.0, before)
    after = jnp.where(jnp.logical_or(g == n_lat - 1, g == n_lat + n_ctx - 1), 0.0, after)
    y = before * w_ref[0:1, :] + cur * w_ref[1:2, :] + after * w_ref[2:3, :] + b_ref[...]
    y = y * jax.nn.sigmoid(y)
    o_ref[...] = (y * jnp.where(j * tc >= k_cols0, k_scale, 1.0)).astype(BF16)


def _conv_silu(u, w_conv, b_conv, n_cols, n_lat, n_ctx):
    s_pad = u.shape[0]
    tm, tc, hr = ROW_TILE, 1024, 16
    rb = tm // hr
    last = s_pad // hr - 1
    return pl.pallas_call(
        functools.partial(_conv_kernel, n_lat=n_lat, n_ctx=n_ctx, k_cols0=n_cols // 2, k_scale=MLSTM_DIM ** -0.5),
        grid=(s_pad // tm, n_cols // tc),
        in_specs=[pl.BlockSpec((tm, tc), lambda i, j: (i, j)),
                  pl.BlockSpec((hr, tc), lambda i, j: (jnp.maximum(i * rb - 1, 0), j)),
                  pl.BlockSpec((hr, tc), lambda i, j: (jnp.minimum((i + 1) * rb, last), j)),
                  pl.BlockSpec((8, tc), lambda i, j: (0, j)),
                  pl.BlockSpec((1, tc), lambda i, j: (0, j))],
        out_specs=pl.BlockSpec((tm, tc), lambda i, j: (i, j)),
        out_shape=jax.ShapeDtypeStruct((s_pad, n_cols), BF16),
        compiler_params=_cparams(("parallel", "parallel")),
        name="conv_silu",
    )(u, u, u, jnp.pad(w_conv, ((0, 8 - w_conv.shape[0]), (0, 0))), b_conv.reshape(1, n_cols))


def _log_sigmoid(x):
    return -(jnp.maximum(-x, 0.0) + jnp.log1p(jnp.exp(-jnp.abs(x))))


def _mlstm_dir(q, k, v, ig_c, fg_c, ig_r, fg_r, c_ref, n_ref, m_ref, o_ref, reverse):
    L = q.shape[0]
    lf_c = _log_sigmoid(fg_c)
    lf_r = _log_sigmoid(fg_r)
    t_i = lax.broadcasted_iota(jnp.int32, (L, L), 0)
    s_i = lax.broadcasted_iota(jnp.int32, (L, L), 1)
    seen = (s_i >= t_i) if reverse else (s_i <= t_i)
    seen_f = jnp.where(seen, 1.0, 0.0)
    b_c = jnp.sum(seen_f * lf_r, axis=1, keepdims=True)
    b_r = jnp.sum(jnp.where(seen, 0.0, 1.0) * lf_c, axis=0, keepdims=True) + lf_r
    m_prev = m_ref[...]
    log_d = jnp.where(seen, b_c - b_r + ig_r, -jnp.inf)
    log_inter = b_c + m_prev
    m_t = jnp.maximum(log_inter, jnp.max(log_d, axis=1, keepdims=True))
    w_intra = jnp.exp(log_d - m_t) * _dot_nt(q, k)
    w_inter = jnp.exp(log_inter - m_t)
    c_prev = c_ref[...]
    n_prev = n_ref[...]
    qf = q.astype(F32)
    num = w_inter * _dot_nt(q, c_prev.astype(BF16)) + _dot(w_intra.astype(BF16), v)
    den = w_inter * jnp.sum(qf * n_prev, axis=1, keepdims=True) + jnp.sum(w_intra, axis=1, keepdims=True)
    o_ref[...] = (num / jnp.maximum(jnp.abs(den), jnp.exp(-m_t))).astype(o_ref.dtype)
    b_end = jnp.sum(lf_c, axis=0, keepdims=True)
    log_w = b_end - b_c + ig_c
    m_new = jnp.maximum(b_end + m_prev, jnp.max(log_w, axis=0, keepdims=True))
    w = jnp.exp(log_w - m_new)
    decay = jnp.exp(b_end + m_prev - m_new)
    wv = (w * v.astype(F32)).astype(BF16)
    c_ref[...] = decay * c_prev + lax.dot_general(wv, k, (((0,), (0,)), ((), ())), preferred_element_type=F32)
    n_ref[...] = decay * n_prev + jnp.sum(w * k.astype(F32), axis=0, keepdims=True)
    m_ref[...] = m_new


def _mlstm_kernel(qf_ref, kf_ref, vf_ref, gcf_ref, grf_ref, qb_ref, kb_ref, vb_ref, gcb_ref, grb_ref,
                  hf_ref, hb_ref, c_scr, n_scr, m_scr):
    @pl.when(pl.program_id(1) == 0)
    def _():
        c_scr[...] = jnp.zeros(c_scr.shape, F32)
        n_scr[...] = jnp.zeros(n_scr.shape, F32)
        m_scr[...] = jnp.zeros(m_scr.shape, F32)

    gc = gcf_ref[0]
    gr = grf_ref[0]
    _mlstm_dir(qf_ref[...], kf_ref[...], vf_ref[...], gc[:, 0:1], gc[:, 1:2], gr[0:1, :], gr[1:2, :],
               c_scr.at[0], n_scr.at[0], m_scr.at[0], hf_ref, False)
    gc = gcb_ref[0]
    gr = grb_ref[0]
    _mlstm_dir(qb_ref[...], kb_ref[...], vb_ref[...], gc[:, 2:3], gc[:, 3:4], gr[2:3, :], gr[3:4, :],
               c_scr.at[1], n_scr.at[1], m_scr.at[1], hb_ref, True)


def _mlstm_scan(qk, u, gates, n_lat, n_ctx):
    s_pad = qk.shape[0]
    L, dh, nh = MLSTM_CHUNK, MLSTM_DIM, MLSTM_HEADS
    nlc, ncc = n_lat // L, n_ctx // L
    steps = nlc + ncc
    g4 = gates.reshape(s_pad, 2, 2, nh).transpose(3, 0, 1, 2).reshape(nh, s_pad, 4)
    g4t = g4.transpose(0, 2, 1)

    def fwd(c):
        return jnp.where(c < ncc, nlc + c, c - ncc)

    def bwd(c):
        return jnp.where(c < ncc, nlc + ncc - 1 - c, nlc - 1 - (c - ncc))

    v_blk0 = 2 * nh
    def specs(pos, qkv_from):
        return [pl.BlockSpec((L, dh), lambda h, c: (pos(c), h)),
                pl.BlockSpec((L, dh), lambda h, c: (pos(c), nh + h)),
                pl.BlockSpec((L, dh), lambda h, c: (pos(c), v_blk0 + h)),
                pl.BlockSpec((1, L, 4), lambda h, c: (h, pos(c), 0)),
                pl.BlockSpec((1, 4, L), lambda h, c: (h, 0, pos(c)))]

    out_shape = jax.ShapeDtypeStruct((steps * L, nh * dh), BF16)
    return pl.pallas_call(
        _mlstm_kernel,
        grid=(nh, steps),
        in_specs=specs(fwd, None) + specs(bwd, None),
        out_specs=[pl.BlockSpec((L, dh), lambda h, c: (fwd(c), h)), pl.BlockSpec((L, dh), lambda h, c: (bwd(c), h))],
        out_shape=[out_shape, out_shape],
        scratch_shapes=[pltpu.VMEM((2, dh, dh), F32), pltpu.VMEM((2, 1, dh), F32), pltpu.VMEM((2, 1, 1), F32)],
        compiler_params=_cparams(("parallel", "arbitrary")),
        name="mlstm_scan",
    )(qk, qk, u, g4, g4t, qk, qk, u, g4, g4t)


def kernel(x, c, ctx, c_ctx, w_mod_0, b_mod_0, g_pre_mix_0, g_post_mix_0, w_in_0, lambda_q1_0, lambda_k1_0, lambda_q2_0, lambda_k2_0, g_subln_0, g_q_0, g_k_0, w_out_0, g_pre_ffn_0, g_post_ffn_0, w_router_0, w_gate_0, w_up_0, w_down_0, w_mod_1, b_mod_1, g_pre_mix_1, g_post_mix_1, w_in_1, b_gate_1, w_conv_1, b_conv_1, g_head_1, w_out_1, g_pre_ffn_1, g_post_ffn_1, w_router_1, w_gate_1, w_up_1, w_down_1):
    b, n_lat, d = x.shape
    n_ctx = ctx.shape[1]
    assert b == 1 and n_lat % ROW_TILE == 0 and n_ctx % MLSTM_CHUNK == 0 and n_ctx <= ROW_TILE
    s_valid = n_lat + n_ctx
    s_pad = n_lat + ROW_TILE
    n_lat_tiles = n_lat // ROW_TILE

    xs = jnp.concatenate([x[0], ctx[0], jnp.zeros((s_pad - s_valid, d), F32)], axis=0)
    cc = jnp.concatenate([c, c_ctx[None, :], jnp.zeros((6, d), F32)], axis=0)

    mod0 = _modulation(cc, w_mod_0, b_mod_0)
    tabs = _rope_tables(n_lat, n_ctx, s_pad)
    p0 = _proj0(xs, mod0, g_pre_mix_0, w_in_0.astype(BF16), tabs, g_q_0, g_k_0, n_lat_tiles)
    lam_init = 0.8 - 0.6 * math.exp(-0.3 * 0)
    lam4 = jnp.stack([lambda_q1_0, lambda_k1_0, lambda_q2_0, lambda_k2_0])
    o_a = _attn_diff(p0, lam4, g_subln_0, n_lat_tiles, s_valid, lam_init)
    o_b = _attn_gqa(p0, n_lat, s_valid)
    ka = o_a.shape[1]
    x_mid, hl, aff = _mixer_out(
        _out0_kernel, (o_a, o_b),
        lambda tm: [pl.BlockSpec((tm, ka), lambda i: (i, 0)), pl.BlockSpec((tm, o_b.shape[1]), lambda i: (i, 0))],
        xs, mod0, w_out_0.astype(BF16), g_post_mix_0, g_pre_ffn_0, w_router_0, s_pad, n_lat // 256,
        lambda tm: [], "attn_out")
    p_moe0 = dict(w_gate=w_gate_0, w_up=w_up_0, w_down=w_down_0, g_post_ffn=g_post_ffn_0)
    xs = _moe(hl, aff, x_mid, mod0, p_moe0, n_lat, n_ctx, s_pad, True)

    mod1 = _modulation(cc, w_mod_1, b_mod_1)
    inner = MLSTM_HEADS * MLSTM_DIM
    u, gates = _proj1(xs, mod1, g_pre_mix_1, w_in_1[:, :4 * inner].astype(BF16), w_in_1[:, 4 * inner:], b_gate_1,
                      n_lat_tiles)
    qk = _conv_silu(u, w_conv_1, b_conv_1, 2 * inner, n_lat, n_ctx)
    hf, hb = _mlstm_scan(qk, u, gates, n_lat, n_ctx)
    og_blk = 3 * inner // inner
    x_mid, hl, aff = _mixer_out(
        _out1_kernel, (hf, hb, u, g_head_1.reshape(1, inner)),
        lambda tm: [pl.BlockSpec((tm, inner), lambda i: (i, 0)), pl.BlockSpec((tm, inner), lambda i: (i, 0)),
                    pl.BlockSpec((tm, inner), lambda i: (i, og_blk)), pl.BlockSpec((1, inner), lambda i: (0, 0))],
        xs, mod1, w_out_1.astype(BF16), g_post_mix_1, g_pre_ffn_1, w_router_1, n_lat, n_lat // 256,
        lambda tm: [pltpu.VMEM((tm, inner), BF16)], "mlstm_out")
    p_moe1 = dict(w_gate=w_gate_1, w_up=w_up_1, w_down=w_down_1, g_post_ffn=g_post_ffn_1)
    out = _moe(hl, aff, x_mid, mod1, p_moe1, n_lat, n_ctx, s_pad, False)
    return out[None]
```

```python
import functools
import math

import jax
import jax.numpy as jnp
from jax import lax
from jax.experimental import pallas as pl
from jax.experimental.pallas import tpu as pltpu

F32 = jnp.float32
BF16 = jnp.bfloat16
HIGHEST = lax.Precision.HIGHEST

GRID_W = 64
ROPE_THETA = 10000.0
RMS_EPS = 1e-6
N_MOD = 6
DIFF_HEADS = 8
DIFF_DIM = 64
GQA_HEADS = 8
GQA_KV_HEADS = 2
GQA_DIM = 128
MLSTM_HEADS = 8
MLSTM_DIM = 256
MLSTM_CHUNK = 128
N_EXPERTS = 16
EC_CAPACITY_FACTOR = 2

LANES = 128
ROW_TILE = 512
MLSTM_HEADS_PER_STEP = 4
FFN_STEPS = 8
ATTN_TK = 512
ATTN_ROW_GROUP = 256
LOG2_E = math.log2(math.e)
TOPK_BLOCKS = 128
VMEM_LIMIT = 56 * 1024 * 1024


def _cparams(sem):
    return pltpu.CompilerParams(dimension_semantics=sem, vmem_limit_bytes=VMEM_LIMIT)


def _rms(x, g):
    return x * lax.rsqrt(jnp.mean(x * x, axis=-1, keepdims=True) + RMS_EPS) * g


def _dot(a, b):
    return jnp.dot(a, b, preferred_element_type=F32)


def _dot_nt(a, b, precision=None):
    return lax.dot_general(a, b, (((1,), (1,)), ((), ())), precision=precision, preferred_element_type=F32)


def _mod_kernel(c_ref, w_ref, b_ref, o_ref):
    c = c_ref[...]
    s = c * jax.nn.sigmoid(c)
    o_ref[...] = lax.dot_general(s, w_ref[...], (((1,), (0,)), ((), ())), precision=HIGHEST,
                                 preferred_element_type=F32) + b_ref[...]


def _modulation(cc, w_mod, b_mod):
    d, n = w_mod.shape
    tn = 1024
    return pl.pallas_call(
        _mod_kernel,
        grid=(n // tn,),
        in_specs=[pl.BlockSpec((8, d), lambda j: (0, 0)),
                  pl.BlockSpec((d, tn), lambda j: (0, j)),
                  pl.BlockSpec((1, tn), lambda j: (0, j))],
        out_specs=pl.BlockSpec((8, tn), lambda j: (0, j)),
        out_shape=jax.ShapeDtypeStruct((8, n), F32),
        compiler_params=_cparams(("arbitrary",)),
        name="modulation",
    )(cc, w_mod, b_mod.reshape(1, n))


def _mod_rows(mod_ref, r, k, d):
    return mod_ref[pl.ds(r, 1), k * d:(k + 1) * d]


def _premix(x_ref, mod_ref, g_ref, r, d):
    x = x_ref[...]
    return _rms(x, g_ref[...]) * (1.0 + _mod_rows(mod_ref, r, 1, d)) + _mod_rows(mod_ref, r, 0, d)


def _rope_a(x, ca, sa, sb):
    return x * ca + pltpu.roll(x, LANES - DIFF_DIM // 2, 1) * sa + pltpu.roll(x, DIFF_DIM // 2, 1) * sb


def _rope_b(x, cb, sb):
    return x * cb + pltpu.roll(x, GQA_DIM // 2, 1) * sb


def _proj0_kernel(x_ref, mod_ref, gpre_ref, w_ref, ca_ref, saa_ref, sab_ref, cb_ref, sbb_ref, gq_ref, gk_ref,
                  o_ref, h_scr, *, n_lat_tiles, d, tn):
    i = pl.program_id(0)
    j = pl.program_id(1)
    ng = tn // LANES

    @pl.when(j == 0)
    def _():
        r = jnp.where(i >= n_lat_tiles, 1, 0)
        h_scr[...] = _premix(x_ref, mod_ref, gpre_ref, r, d).astype(BF16)

    acc = _dot(h_scr[...], w_ref[...])

    def groups(fn):
        for g in range(ng):
            o_ref[:, g * LANES:(g + 1) * LANES] = fn(acc[:, g * LANES:(g + 1) * LANES], g).astype(BF16)

    qa_tiles = DIFF_HEADS * 2 * DIFF_DIM // tn
    qb0 = 3 * qa_tiles
    qb_tiles = GQA_HEADS * GQA_DIM // tn

    @pl.when(j < 2 * qa_tiles)
    def _():
        sc = jnp.where(j < qa_tiles, DIFF_DIM ** -0.5 * LOG2_E, 1.0)
        groups(lambda a, g: _rope_a(a, ca_ref[...], saa_ref[...], sab_ref[...]) * sc)

    @pl.when(jnp.logical_and(j >= 2 * qa_tiles, j < qb0))
    def _():
        groups(lambda a, g: a)

    @pl.when(jnp.logical_and(j >= qb0, j < qb0 + qb_tiles))
    def _():
        groups(lambda a, g: _rope_b(_rms(a, gq_ref[...]), cb_ref[...], sbb_ref[...]) * (GQA_DIM ** -0.5 * LOG2_E))

    @pl.when(j == qb0 + qb_tiles)
    def _():
        def fn(a, g):
            if g < GQA_KV_HEADS:
                return _rope_b(_rms(a, gk_ref[...]), cb_ref[...], sbb_ref[...])
            return a
        groups(fn)


def _proj0(xs, mod, g_pre, w_in, tabs, g_q, g_k, n_lat_tiles):
    s_pad, d = xs.shape
    n_out = w_in.shape[1]
    tm, tn = ROW_TILE, 512
    assert GQA_KV_HEADS * GQA_DIM * 2 == tn
    row = lambda i, j: (i, 0)
    const = lambda i, j: (0, 0)
    tab_spec = pl.BlockSpec((tm, LANES), row)
    return pl.pallas_call(
        functools.partial(_proj0_kernel, n_lat_tiles=n_lat_tiles, d=d, tn=tn),
        grid=(s_pad // tm, n_out // tn),
        in_specs=[pl.BlockSpec((tm, d), row),
                  pl.BlockSpec((8, N_MOD * d), const),
                  pl.BlockSpec((1, d), const),
                  pl.BlockSpec((d, tn), lambda i, j: (0, j)),
                  tab_spec, tab_spec, tab_spec, tab_spec, tab_spec,
                  pl.BlockSpec((1, LANES), const),
                  pl.BlockSpec((1, LANES), const)],
        out_specs=pl.BlockSpec((tm, tn), lambda i, j: (i, j)),
        out_shape=jax.ShapeDtypeStruct((s_pad, n_out), BF16),
        scratch_shapes=[pltpu.VMEM((tm, d), BF16)],
        compiler_params=_cparams(("parallel", "arbitrary")),
        name="proj0",
    )(xs, mod, g_pre.reshape(1, d), w_in, *tabs, g_q.reshape(1, LANES), g_k.reshape(1, LANES))


def _rope_tables(n_lat, n_ctx, s_pad):
    rows = jnp.repeat(jnp.arange(n_lat // GRID_W, dtype=F32), GRID_W)
    cols = jnp.tile(jnp.arange(GRID_W, dtype=F32), n_lat // GRID_W)

    def angles(dim):
        axis_dim = dim // 2
        inv_freq = ROPE_THETA ** (-jnp.arange(0, axis_dim, 2, dtype=F32) / axis_dim)
        ang = jnp.concatenate([rows[:, None] * inv_freq, cols[:, None] * inv_freq], axis=-1)
        return jnp.cos(ang), jnp.sin(ang)

    def pad(t, fill):
        return jnp.concatenate([t, jnp.full((s_pad - n_lat, LANES), fill, F32)], axis=0)

    cos_a, sin_a = angles(DIFF_DIM)
    zero = jnp.zeros_like(sin_a)
    ca = jnp.tile(cos_a, (1, 4))
    saa = jnp.tile(jnp.concatenate([-sin_a, zero], axis=1), (1, 2))
    sab = jnp.tile(jnp.concatenate([zero, sin_a], axis=1), (1, 2))
    cos_b, sin_b = angles(GQA_DIM)
    cb = jnp.tile(cos_b, (1, 2))
    sbb = jnp.concatenate([-sin_b, sin_b], axis=1)
    return pad(ca, 1.0), pad(saa, 0.0), pad(sab, 0.0), pad(cb, 1.0), pad(sbb, 0.0)


def _flash_rows(q, k_ref, v_ref, m_scr, acc_scr, s_scr, is_ctx_q, tk, s_valid):
    n_chunks = k_ref.shape[0] // tk
    last = n_chunks - 1
    groups = [slice(g * ATTN_ROW_GROUP, (g + 1) * ATTN_ROW_GROUP) for g in range(q.shape[0] // ATTN_ROW_GROUP)]
    m_scr[...] = jnp.full(m_scr.shape, -jnp.inf, F32)
    acc_scr[...] = jnp.zeros(acc_scr.shape, F32)

    def rows_of(c):
        return pl.ds(c * tk if isinstance(c, int) else pl.multiple_of(c * tk, tk), tk)

    def scores(c, buf, rows):
        s_scr[buf, rows] = _dot_nt(q[rows], k_ref[rows_of(c), :])

    def consume(c, buf, rows, v1, masked):
        s = s_scr[buf, rows]
        if masked:
            kpos = c * tk + lax.broadcasted_iota(jnp.int32, (1, tk), 1)
            s = jnp.where(kpos < s_valid, s, -jnp.inf)
        m_prev = m_scr[rows]
        m_new = jnp.maximum(m_prev, jnp.max(s, axis=1, keepdims=True))
        p = jnp.exp2(s - m_new).astype(BF16)
        acc_scr[rows] = jnp.exp2(m_prev - m_new) * acc_scr[rows] + _dot(p, v1)
        m_scr[rows] = m_new

    def values(c):
        vc = v_ref[rows_of(c), :]
        return jnp.concatenate([vc, jnp.ones_like(vc)], axis=1)

    unroll = max(u for u in (2, 4, 8) if last % u == 0)
    lo = jnp.where(is_ctx_q, last, 0)
    for rows in groups:
        scores(lo, 0, rows)

    def body(i, carry):
        c = lo + unroll * i
        for j in range(unroll):
            v1 = values(c + j)
            for rows in groups:
                scores(c + j + 1, (j + 1) % 2, rows)
                consume(c + j, j % 2, rows, v1, False)
        return carry

    lax.fori_loop(0, (last - lo) // unroll, body, 0)
    v1 = values(last)
    for rows in groups:
        consume(last, 0, rows, v1, True)


def _attn_diff_kernel(q_ref, k_ref, v_ref, lam_ref, gsub_ref, o_ref, m_scr, acc_scr, s_scr,
                      *, n_lat_tiles, tk, s_valid, lam_init):
    q = q_ref[...]
    tq, hd = q.shape
    first = lax.broadcasted_iota(jnp.int32, (1, hd), 1) < DIFF_DIM
    zero = jnp.zeros_like(q)
    qq = jnp.concatenate([jnp.where(first, q, zero), jnp.where(first, zero, q)], axis=0)
    _flash_rows(qq, k_ref, v_ref, m_scr, acc_scr, s_scr, pl.program_id(1) >= n_lat_tiles, tk, s_valid)
    lam = (jnp.exp(jnp.sum(lam_ref[0:1, :] * lam_ref[1:2, :], axis=1, keepdims=True))
           - jnp.exp(jnp.sum(lam_ref[2:3, :] * lam_ref[3:4, :], axis=1, keepdims=True)) + lam_init)
    o = (acc_scr[0:tq, 0:hd] / acc_scr[0:tq, hd:hd + 1]
         - lam * (acc_scr[tq:2 * tq, 0:hd] / acc_scr[tq:2 * tq, hd:hd + 1]))
    o_ref[...] = (_rms(o, gsub_ref[...]) * (1.0 - lam_init)).astype(BF16)


def _attn_diff(p0, lam4, g_subln, n_lat_tiles, s_valid, lam_init):
    s_pad = p0.shape[0]
    tq = ROW_TILE
    hd = 2 * DIFF_DIM
    k0 = DIFF_HEADS
    v0 = 2 * DIFF_HEADS
    return pl.pallas_call(
        functools.partial(_attn_diff_kernel, n_lat_tiles=n_lat_tiles, tk=ATTN_TK, s_valid=s_valid, lam_init=lam_init),
        grid=(DIFF_HEADS, s_pad // tq),
        in_specs=[pl.BlockSpec((tq, hd), lambda h, iq: (iq, h)),
                  pl.BlockSpec((s_pad, hd), lambda h, iq: (0, k0 + h)),
                  pl.BlockSpec((s_pad, hd), lambda h, iq: (0, v0 + h)),
                  pl.BlockSpec((4, DIFF_DIM), lambda h, iq: (0, 0)),
                  pl.BlockSpec((1, hd), lambda h, iq: (0, 0))],
        out_specs=pl.BlockSpec((tq, hd), lambda h, iq: (iq, h)),
        out_shape=jax.ShapeDtypeStruct((s_pad, DIFF_HEADS * hd), BF16),
        scratch_shapes=[pltpu.VMEM((2 * tq, 1), F32), pltpu.VMEM((2 * tq, 2 * hd), F32),
                        pltpu.VMEM((2, 2 * tq, ATTN_TK), F32)],
        compiler_params=_cparams(("parallel", "arbitrary")),
        name="attn_diff",
    )(p0, p0, p0, lam4, g_subln.reshape(1, hd))


def _attn_gqa_kernel(q_ref, k_ref, v_ref, o_ref, m_scr, acc_scr, s_scr, *, n_lat_tiles, tk, s_valid, group):
    tq = q_ref.shape[0]
    dh = GQA_DIM
    qq = jnp.concatenate([q_ref[:, g * dh:(g + 1) * dh] for g in range(group)], axis=0)
    _flash_rows(qq, k_ref, v_ref, m_scr, acc_scr, s_scr, pl.program_id(1) >= n_lat_tiles, tk, s_valid)
    for g in range(group):
        rows = slice(g * tq, (g + 1) * tq)
        o_ref[:, g * dh:(g + 1) * dh] = (acc_scr[rows, 0:dh] / acc_scr[rows, dh:dh + 1]).astype(BF16)


def _attn_gqa(p0, n_lat, s_valid):
    s_pad = p0.shape[0]
    tq = 256
    group = GQA_HEADS // GQA_KV_HEADS
    qw = group * GQA_DIM
    q0 = 3 * DIFF_HEADS * 2 * DIFF_DIM // qw
    k0 = (3 * DIFF_HEADS * 2 * DIFF_DIM + GQA_HEADS * GQA_DIM) // GQA_DIM
    v0 = k0 + GQA_KV_HEADS
    return pl.pallas_call(
        functools.partial(_attn_gqa_kernel, n_lat_tiles=n_lat // tq, tk=ATTN_TK, s_valid=s_valid, group=group),
        grid=(GQA_KV_HEADS, s_pad // tq),
        in_specs=[pl.BlockSpec((tq, qw), lambda h, iq: (iq, q0 + h)),
                  pl.BlockSpec((s_pad, GQA_DIM), lambda h, iq: (0, k0 + h)),
                  pl.BlockSpec((s_pad, GQA_DIM), lambda h, iq: (0, v0 + h))],
        out_specs=pl.BlockSpec((tq, qw), lambda h, iq: (iq, h)),
        out_shape=jax.ShapeDtypeStruct((s_pad, GQA_HEADS * GQA_DIM), BF16),
        scratch_shapes=[pltpu.VMEM((group * tq, 1), F32), pltpu.VMEM((group * tq, 2 * GQA_DIM), F32),
                        pltpu.VMEM((2, group * tq, ATTN_TK), F32)],
        compiler_params=_cparams(("parallel", "arbitrary")),
        name="attn_gqa",
    )(p0, p0, p0)


def _post_mix(y, x_ref, mod_ref, r, gpost_ref, gffn_ref, wr_ref, xo_ref, hl_ref, aff_ref, d):
    xn = x_ref[...] + _mod_rows(mod_ref, r, 2, d) * _rms(y, gpost_ref[...])
    xo_ref[...] = xn
    hl = _rms(xn, gffn_ref[...]) * (1.0 + _mod_rows(mod_ref, r, 4, d)) + _mod_rows(mod_ref, r, 3, d)
    hl_ref[...] = hl
    wr = wr_ref[...]
    h_hi = hl.astype(BF16)
    h_lo = (hl - h_hi.astype(F32)).astype(BF16)
    w_hi = wr.astype(BF16)
    w_lo = (wr - w_hi.astype(F32)).astype(BF16)
    logits = _dot(h_hi, w_hi) + (_dot(h_hi, w_lo) + _dot(h_lo, w_hi))
    e = jnp.exp(logits - jnp.max(logits, axis=1, keepdims=True))
    aff_ref[...] = e / jnp.sum(e, axis=1, keepdims=True)


def _out0_kernel(oa_ref, ob_ref, x_ref, mod_ref, w_ref, gpost_ref, gffn_ref, wr_ref, xo_ref, hl_ref, aff_ref,
                 *, n_lat_tiles, d):
    r = jnp.where(pl.program_id(0) >= n_lat_tiles, 1, 0)
    ka = oa_ref.shape[1]
    y = _dot(oa_ref[...], w_ref[0:ka, :]) + _dot(ob_ref[...], w_ref[ka:, :])
    _post_mix(y, x_ref, mod_ref, r, gpost_ref, gffn_ref, wr_ref, xo_ref, hl_ref, aff_ref, d)


def _out1_kernel(hf_ref, hb_ref, og_ref, ghead_ref, x_ref, mod_ref, w_ref, gpost_ref, gffn_ref, wr_ref,
                 xo_ref, hl_ref, aff_ref, y_scr, *, n_lat_tiles, d):
    r = jnp.where(pl.program_id(0) >= n_lat_tiles, 1, 0)
    for h in range(MLSTM_HEADS):
        sl = slice(h * MLSTM_DIM, (h + 1) * MLSTM_DIM)
        hs = hf_ref[:, sl].astype(F32) + hb_ref[:, sl].astype(F32)
        hn = _rms(hs, ghead_ref[:, sl])
        y_scr[:, sl] = (jax.nn.sigmoid(og_ref[:, sl].astype(F32)) * hn).astype(BF16)
    y = _dot(y_scr[...], w_ref[...])
    _post_mix(y, x_ref, mod_ref, r, gpost_ref, gffn_ref, wr_ref, xo_ref, hl_ref, aff_ref, d)


def _mixer_out(kernel_fn, mix_inputs, mix_specs, x, mod, w_out, g_post, g_ffn, w_router, n_rows, n_lat_tiles_tm,
               scratch, name):
    d = x.shape[1]
    tm = 256
    row = lambda i: (i, 0)
    const = lambda i: (0, 0)
    ne = w_router.shape[1]
    return pl.pallas_call(
        functools.partial(kernel_fn, n_lat_tiles=n_lat_tiles_tm, d=d),
        grid=(n_rows // tm,),
        in_specs=mix_specs(tm) + [pl.BlockSpec((tm, d), row),
                                  pl.BlockSpec((8, N_MOD * d), const),
                                  pl.BlockSpec(w_out.shape, const),
                                  pl.BlockSpec((1, d), const),
                                  pl.BlockSpec((1, d), const),
                                  pl.BlockSpec((d, ne), const)],
        out_specs=[pl.BlockSpec((tm, d), row), pl.BlockSpec((tm, d), row), pl.BlockSpec((tm, ne), row)],
        out_shape=[jax.ShapeDtypeStruct((n_rows, d), F32), jax.ShapeDtypeStruct((n_rows, d), F32),
                   jax.ShapeDtypeStruct((n_rows, ne), F32)],
        scratch_shapes=scratch(tm),
        compiler_params=_cparams(("parallel",)),
        name=name,
    )(*mix_inputs, x, mod, w_out, g_post.reshape(1, d), g_ffn.reshape(1, d), w_router)


def _incl_cumsum(maskf, upper, strict_lower):
    mb = maskf.astype(BF16)
    local = _dot(mb, upper)
    start = jnp.sum(_dot(strict_lower, mb), axis=1, keepdims=True)
    return local, start


def _topk_kernel(aff_ref, idx_ref, gate_ref, dest_ref, cnt_ref, cnt_acc, *, cap, dest_stride, row_off):
    e = pl.program_id(0)
    ne = pl.num_programs(0)
    nb = aff_ref.shape[1]
    a = aff_ref[0]
    bits = pltpu.bitcast(a, jnp.int32)

    thr = jnp.zeros((1, 1), jnp.int32)
    for bit in range(30, -1, -1):
        cand = thr | (1 << bit)
        c = jnp.sum(jnp.sum(jnp.where(bits >= cand, 1.0, 0.0), axis=1, keepdims=True), axis=0, keepdims=True)
        thr = jnp.where(c >= cap, cand, thr)

    r_i = lax.broadcasted_iota(jnp.int32, (LANES, LANES), 0)
    c_i = lax.broadcasted_iota(jnp.int32, (LANES, LANES), 1)
    upper = jnp.where(r_i <= c_i, 1.0, 0.0).astype(BF16)
    rb = lax.broadcasted_iota(jnp.int32, (nb, nb), 0)
    cb = lax.broadcasted_iota(jnp.int32, (nb, nb), 1)
    strict_lower = jnp.where(cb < rb, 1.0, 0.0).astype(BF16)

    gt = bits > thr
    eq = bits == thr
    n_gt = jnp.sum(jnp.sum(jnp.where(gt, 1.0, 0.0), axis=1, keepdims=True), axis=0, keepdims=True)
    eq_local, eq_start = _incl_cumsum(jnp.where(eq, 1.0, 0.0), upper, strict_lower)
    sel = jnp.logical_or(gt, jnp.logical_and(eq, eq_local + eq_start <= cap - n_gt))
    self_ = jnp.where(sel, 1.0, 0.0)
    sel_local, sel_start = _incl_cumsum(self_, upper, strict_lower)
    sel_tot = jnp.sum(self_, axis=1, keepdims=True)

    @pl.when(e == 0)
    def _():
        cnt_acc[...] = jnp.zeros(cnt_acc.shape, F32)

    earlier = cnt_acc[...]
    cnt_acc[...] = earlier + self_

    lane = lax.broadcasted_iota(jnp.int32, (nb, LANES), 1)
    cols = jnp.where(lane == 0, sel_start, jnp.where(lane == 1, sel_tot, 0.0))
    rows = cols.T
    start_row = rows[0:1, 0:nb]
    tot_row = rows[1:2, 0:nb]

    p = lax.broadcasted_iota(jnp.int32, (cap, 1), 0).astype(F32)
    onehot = jnp.where(jnp.logical_and(start_row <= p, p < start_row + tot_row), 1.0, 0.0)
    blk = lax.broadcasted_iota(jnp.int32, (1, nb), 1).astype(F32)
    p_start = jnp.sum(onehot * start_row, axis=1, keepdims=True)
    p_blk = jnp.sum(onehot * blk, axis=1, keepdims=True)
    ohb = onehot.astype(BF16)
    local_rows = _dot(ohb, sel_local.astype(BF16))
    sel_rows = _dot(ohb, self_.astype(BF16))
    earlier_rows = _dot(ohb, earlier.astype(BF16))
    aff_rows = lax.dot_general(onehot, a, (((1,), (0,)), ((), ())), precision=HIGHEST, preferred_element_type=F32)
    hit = jnp.where(jnp.logical_and(local_rows == p - p_start + 1.0, sel_rows > 0.5), 1.0, 0.0)
    lane_f = lax.broadcasted_iota(jnp.int32, (1, LANES), 1).astype(F32)
    tok = p_blk * LANES + jnp.sum(hit * lane_f, axis=1, keepdims=True)
    gate = jnp.sum(hit * aff_rows, axis=1, keepdims=True)
    rank = jnp.sum(hit * earlier_rows, axis=1, keepdims=True)
    idx_ref[0] = tok.astype(jnp.int32) + row_off
    gate_ref[0] = gate
    dest_ref[0] = (rank * dest_stride + tok).astype(jnp.int32) + row_off

    @pl.when(e == ne - 1)
    def _():
        cnt_ref[...] = cnt_acc[...].astype(jnp.int32)


def _route(aff_t, cap, dest_stride, row_off):
    ne, n = aff_t.shape
    n_pad = TOPK_BLOCKS * LANES
    assert n <= n_pad
    a = jnp.pad(aff_t, ((0, 0), (0, n_pad - n)), constant_values=-1.0).reshape(ne, TOPK_BLOCKS, LANES)
    slot = pl.BlockSpec((1, cap, 1), lambda e: (e, 0, 0))
    idx, gate, dest, cnt = pl.pallas_call(
        functools.partial(_topk_kernel, cap=cap, dest_stride=dest_stride, row_off=row_off),
        grid=(ne,),
        in_specs=[pl.BlockSpec((1, TOPK_BLOCKS, LANES), lambda e: (e, 0, 0))],
        out_specs=[slot, slot, slot, pl.BlockSpec((TOPK_BLOCKS, LANES), lambda e: (0, 0))],
        out_shape=[jax.ShapeDtypeStruct((ne, cap, 1), jnp.int32), jax.ShapeDtypeStruct((ne, cap, 1), F32),
                   jax.ShapeDtypeStruct((ne, cap, 1), jnp.int32), jax.ShapeDtypeStruct((TOPK_BLOCKS, LANES), jnp.int32)],
        scratch_shapes=[pltpu.VMEM((TOPK_BLOCKS, LANES), F32)],
        compiler_params=_cparams(("arbitrary",)),
        name="route_topk",
    )(a)
    return idx.reshape(ne * cap), gate, dest.reshape(ne * cap), cnt.reshape(n_pad)[:n]


def _row_copy(src, s, dst, t, sem):
    return pltpu.make_async_copy(src.at[pl.ds(s, 1)], dst.at[pl.ds(t, 1)], sem)


def _ffn_body(idx_sm, dest_sm, hl_hbm, gate_ref, wg_ref, wu_ref, wd_ref, z_hbm, xbuf, xb, acc, gsem, ssem, *, tm):
    f = pl.program_id(2)
    nm = pl.num_programs(1)
    t = pl.program_id(0) * nm + pl.program_id(1)
    n_tiles = pl.num_programs(0) * nm
    cur = t % 2
    nxt = 1 - cur
    chunk = tm // FFN_STEPS
    last_step = f == pl.num_programs(2) - 1
    t_next = jnp.minimum(t + 1, n_tiles - 1)
    t_prev = jnp.maximum(t - 1, 0)

    def gather(tile, slot, r):
        return _row_copy(hl_hbm, idx_sm[tile * tm + r], xbuf.at[slot], r, gsem.at[slot])

    def scatter(tile, slot, r):
        return _row_copy(acc.at[slot], r, z_hbm, dest_sm[tile * tm + r], ssem.at[slot])

    def for_rows(fn):
        def body(r, c):
            fn(r)
            return c
        lax.fori_loop(0, tm, body, 0, unroll=8)

    @pl.when(jnp.logical_and(t == 0, f == 0))
    def _():
        for_rows(lambda r: gather(0, 0, r).start())
        acc[1] = jnp.zeros(acc.shape[1:], F32)

    @pl.when(f == 0)
    def _():
        for_rows(lambda r: gather(t, cur, r).wait())
        xb[...] = xbuf[cur].astype(BF16)

        @pl.when(t >= 1)
        def _():
            for_rows(lambda r: scatter(t_prev, cur, r).wait())
        acc[cur] = jnp.zeros(acc.shape[1:], F32)

    for r in range(chunk):
        gather(t_next, nxt, f * chunk + r).start()
        scatter(t_prev, nxt, f * chunk + r).start()

    x = xb[...]
    g = _dot(x, wg_ref[0].astype(BF16))
    u = _dot(x, wu_ref[0].astype(BF16))
    hid = (g * jax.nn.sigmoid(g) * u).astype(BF16)
    acc[cur] += _dot(hid, wd_ref[0].astype(BF16))

    @pl.when(last_step)
    def _():
        acc[cur] = acc[cur] * gate_ref[0]

    @pl.when(jnp.logical_and(t == n_tiles - 1, last_step))
    def _():
        for_rows(lambda r: gather(t, nxt, r).wait())
        for_rows(lambda r: scatter(t_prev, nxt, r).wait())
        for_rows(lambda r: scatter(t, cur, r).start())
        for_rows(lambda r: scatter(t, cur, r).wait())


def _ffn_kernel_new(idx_sm, dest_sm, hl_hbm, gate_ref, wg_ref, wu_ref, wd_ref, z_hbm, *scratch, tm):
    _ffn_body(idx_sm, dest_sm, hl_hbm, gate_ref, wg_ref, wu_ref, wd_ref, z_hbm, *scratch, tm=tm)


def _ffn_kernel_alias(idx_sm, dest_sm, hl_hbm, gate_ref, wg_ref, wu_ref, wd_ref, z_in, z_hbm, *scratch, tm):
    del z_in
    _ffn_body(idx_sm, dest_sm, hl_hbm, gate_ref, wg_ref, wu_ref, wd_ref, z_hbm, *scratch, tm=tm)


def _expert_ffn(idx, dest, hl, gate, w_gate, w_up, w_down, z_rows, z=None):
    ne, d, ff = w_gate.shape
    cap = gate.shape[1]
    tm = min(cap, 1024)
    tf = ff // FFN_STEPS
    any_spec = pl.BlockSpec(memory_space=pl.ANY)
    in_specs = [any_spec,
                pl.BlockSpec((1, tm, 1), lambda e, m, f, *_: (e, m, 0)),
                pl.BlockSpec((1, d, tf), lambda e, m, f, *_: (e, 0, f)),
                pl.BlockSpec((1, d, tf), lambda e, m, f, *_: (e, 0, f)),
                pl.BlockSpec((1, tf, d), lambda e, m, f, *_: (e, f, 0))]
    args = [idx, dest, hl, gate, w_gate, w_up, w_down]
    kw = dict(tm=tm)
    if z is None:
        kern, aliases = functools.partial(_ffn_kernel_new, **kw), {}
    else:
        kern, aliases = functools.partial(_ffn_kernel_alias, **kw), {len(args): 0}
        in_specs.append(any_spec)
        args.append(z)
    return pl.pallas_call(
        kern,
        grid_spec=pltpu.PrefetchScalarGridSpec(
            num_scalar_prefetch=2,
            grid=(ne, cap // tm, ff // tf),
            in_specs=in_specs,
            out_specs=any_spec,
            scratch_shapes=[pltpu.VMEM((2, tm, d), F32), pltpu.VMEM((tm, d), BF16), pltpu.VMEM((2, tm, d), F32),
                            pltpu.SemaphoreType.DMA((2,)), pltpu.SemaphoreType.DMA((2,))]),
        out_shape=jax.ShapeDtypeStruct((z_rows, d), F32),
        input_output_aliases=aliases,
        compiler_params=_cparams(("arbitrary", "arbitrary", "arbitrary")),
        name="expert_ffn",
    )(*args)


def _combine_kernel(maxc_sm, z_ref, cnt_ref, x_ref, mod_ref, g_ref, o_ref, acc, *, n_lat_tiles, d):
    i = pl.program_id(0)
    k = pl.program_id(1)
    nk = pl.num_programs(1)

    @pl.when(k == 0)
    def _():
        acc[...] = jnp.zeros(acc.shape, F32)

    @pl.when(k < maxc_sm[i])
    def _():
        acc[...] += jnp.where(cnt_ref[...] > k, z_ref[...], 0.0)

    @pl.when(k == nk - 1)
    def _():
        r = jnp.where(i >= n_lat_tiles, 1, 0)
        o_ref[...] = x_ref[...] + _mod_rows(mod_ref, r, 5, d) * _rms(acc[...], g_ref[...])


def _combine(z, cnt, x, mod, g_post, n_rows, z_stride, n_lat_tiles_tb):
    d = x.shape[1]
    tb = ROW_TILE
    nb = n_rows // tb
    zb = z_stride // tb
    maxc = jnp.max(cnt[:n_rows].reshape(nb, tb), axis=1)

    def z_map(i, k, maxc_sm):
        return (jnp.maximum(jnp.minimum(k, maxc_sm[i] - 1), 0) * zb + i, 0)

    return pl.pallas_call(
        functools.partial(_combine_kernel, n_lat_tiles=n_lat_tiles_tb, d=d),
        grid_spec=pltpu.PrefetchScalarGridSpec(
            num_scalar_prefetch=1,
            grid=(nb, N_EXPERTS),
            in_specs=[pl.BlockSpec((tb, d), z_map),
                      pl.BlockSpec((tb, 1), lambda i, k, s: (i, 0)),
                      pl.BlockSpec((tb, d), lambda i, k, s: (i, 0)),
                      pl.BlockSpec((8, N_MOD * d), lambda i, k, s: (0, 0)),
                      pl.BlockSpec((1, d), lambda i, k, s: (0, 0))],
            out_specs=pl.BlockSpec((tb, d), lambda i, k, s: (i, 0)),
            scratch_shapes=[pltpu.VMEM((tb, d), F32)]),
        out_shape=jax.ShapeDtypeStruct((n_rows, d), F32),
        compiler_params=_cparams(("parallel", "arbitrary")),
        name="moe_combine",
    )(maxc, z, cnt.reshape(-1, 1), x, mod, g_post.reshape(1, d))


def _moe(hl, aff, x_mid, mod, p, n_lat, n_ctx, s_pad, with_ctx):
    aff_t = aff.T
    cap_l = EC_CAPACITY_FACTOR * n_lat // N_EXPERTS
    idx, gate, dest, cnt = _route(aff_t[:, :n_lat], cap_l, s_pad, 0)
    z = _expert_ffn(idx, dest, hl, gate, p['w_gate'], p['w_up'], p['w_down'], N_EXPERTS * s_pad)
    n_rows = n_lat
    if with_ctx:
        cap_c = EC_CAPACITY_FACTOR * n_ctx // N_EXPERTS
        idx_c, gate_c, dest_c, cnt_c = _route(aff_t[:, n_lat:n_lat + n_ctx], cap_c, s_pad, n_lat)
        z = _expert_ffn(idx_c, dest_c, hl, gate_c, p['w_gate'], p['w_up'], p['w_down'], N_EXPERTS * s_pad, z)
        cnt = jnp.concatenate([cnt, cnt_c, jnp.zeros((s_pad - n_lat - n_ctx,), jnp.int32)])
        n_rows = s_pad
    return _combine(z, cnt, x_mid, mod, p['g_post_ffn'], n_rows, s_pad, n_lat // ROW_TILE)


def _proj1_kernel(x_ref, mod_ref, gpre_ref, w_ref, wg_ref, bg_ref, o_ref, g_ref, h_scr, *, n_lat_tiles, d):
    i = pl.program_id(0)
    j = pl.program_id(1)

    @pl.when(j == 0)
    def _():
        r = jnp.where(i >= n_lat_tiles, 1, 0)
        h = _premix(x_ref, mod_ref, gpre_ref, r, d)
        h_scr[...] = h.astype(BF16)
        g_ref[...] = lax.dot_general(h, wg_ref[...], (((1,), (0,)), ((), ())), precision=HIGHEST,
                                     preferred_element_type=F32) + bg_ref[...]

    o_ref[...] = _dot(h_scr[...], w_ref[...]).astype(BF16)


def _proj1(xs, mod, g_pre, w_main, w_gates, b_gate, n_lat_tiles):
    s_pad, d = xs.shape
    n_out = w_main.shape[1]
    ng = w_gates.shape[1]
    tm, tn = ROW_TILE, 512
    row = lambda i, j: (i, 0)
    const = lambda i, j: (0, 0)
    return pl.pallas_call(
        functools.partial(_proj1_kernel, n_lat_tiles=n_lat_tiles, d=d),
        grid=(s_pad // tm, n_out // tn),
        in_specs=[pl.BlockSpec((tm, d), row),
                  pl.BlockSpec((8, N_MOD * d), const),
                  pl.BlockSpec((1, d), const),
                  pl.BlockSpec((d, tn), lambda i, j: (0, j)),
                  pl.BlockSpec((d, ng), const),
                  pl.BlockSpec((1, ng), const)],
        out_specs=[pl.BlockSpec((tm, tn), lambda i, j: (i, j)), pl.BlockSpec((tm, ng), row)],
        out_shape=[jax.ShapeDtypeStruct((s_pad, n_out), BF16), jax.ShapeDtypeStruct((s_pad, ng), F32)],
        scratch_shapes=[pltpu.VMEM((tm, d), BF16)],
        compiler_params=_cparams(("parallel", "arbitrary")),
        name="proj1",
    )(xs, mod, g_pre.reshape(1, d), w_main, w_gates, b_gate.reshape(1, ng))


def _conv_kernel(cur_ref, prev_ref, next_ref, w_ref, b_ref, o_ref, *, n_lat, n_ctx, k_cols0, k_scale):
    i = pl.program_id(0)
    j = pl.program_id(1)
    tm, tc = cur_ref.shape
    hr = prev_ref.shape[0]
    cur = cur_ref[...].astype(F32)
    row = lax.broadcasted_iota(jnp.int32, (tm, 1), 0)
    g = i * tm + row
    prev_row = prev_ref[hr - 1:hr, :].astype(F32)
    next_row = next_ref[0:1, :].astype(F32)
    before = jnp.where(row == 0, prev_row, pltpu.roll(cur, 1, 0))
    after = jnp.where(row == tm - 1, next_row, pltpu.roll(cur, tm - 1, 0))
    before = jnp.where(jnp.logical_or(g == 0, g == n_lat), 0.0, before)
    after = jnp.where(jnp.logical_or(g == n_lat - 1, g == n_lat + n_ctx - 1), 0.0, after)
    y = before * w_ref[0:1, :] + cur * w_ref[1:2, :] + after * w_ref[2:3, :] + b_ref[...]
    y = y * jax.nn.sigmoid(y)
    o_ref[...] = (y * jnp.where(j * tc >= k_cols0, k_scale, 1.0)).astype(BF16)


def _conv_silu(u, w_conv, b_conv, n_cols, n_lat, n_ctx):
    s_pad = u.shape[0]
    tm, tc, hr = ROW_TILE, 1024, 16
    rb = tm // hr
    last = s_pad // hr - 1
    return pl.pallas_call(
        functools.partial(_conv_kernel, n_lat=n_lat, n_ctx=n_ctx, k_cols0=n_cols // 2, k_scale=MLSTM_DIM ** -0.5),
        grid=(s_pad // tm, n_cols // tc),
        in_specs=[pl.BlockSpec((tm, tc), lambda i, j: (i, j)),
                  pl.BlockSpec((hr, tc), lambda i, j: (jnp.maximum(i * rb - 1, 0), j)),
                  pl.BlockSpec((hr, tc), lambda i, j: (jnp.minimum((i + 1) * rb, last), j)),
                  pl.BlockSpec((8, tc), lambda i, j: (0, j)),
                  pl.BlockSpec((1, tc), lambda i, j: (0, j))],
        out_specs=pl.BlockSpec((tm, tc), lambda i, j: (i, j)),
        out_shape=jax.ShapeDtypeStruct((s_pad, n_cols), BF16),
        compiler_params=_cparams(("parallel", "parallel")),
        name="conv_silu",
    )(u, u, u, jnp.pad(w_conv, ((0, 8 - w_conv.shape[0]), (0, 0))), b_conv.reshape(1, n_cols))


def _log_sigmoid(x):
    return -(jnp.maximum(-x, 0.0) + jnp.log1p(jnp.exp(-jnp.abs(x))))


def _mlstm_chains(chains):
    def each(fn, *lists):
        return [fn(*a) for a in zip(*lists)]

    L = chains[0]['q'].shape[0]
    t_i = lax.broadcasted_iota(jnp.int32, (L, L), 0)
    s_i = lax.broadcasted_iota(jnp.int32, (L, L), 1)
    seen = [(s_i >= t_i) if ch['reverse'] else (s_i <= t_i) for ch in chains]
    q = [ch['q'] for ch in chains]
    k = [ch['k'] for ch in chains]
    v = [ch['v'] for ch in chains]
    ig_c = [ch['ig_c'] for ch in chains]
    ig_r = [ch['ig_r'] for ch in chains]
    lf_c = [_log_sigmoid(ch['fg_c']) for ch in chains]
    lf_r = [_log_sigmoid(ch['fg_r']) for ch in chains]
    m_prev = [ch['m'][...] for ch in chains]
    c_prev = [ch['c'][...] for ch in chains]
    n_prev = [ch['n'][...] for ch in chains]
    qk = each(_dot_nt, q, k)
    qc = each(lambda a, c: _dot_nt(a, c.astype(BF16)), q, c_prev)
    b_c = each(lambda sn, r: jnp.sum(jnp.where(sn, 1.0, 0.0) * r, axis=1, keepdims=True), seen, lf_r)
    b_r = each(lambda sn, c, r: jnp.sum(jnp.where(sn, 0.0, 1.0) * c, axis=0, keepdims=True) + r, seen, lf_c, lf_r)
    log_d = each(lambda sn, bc, br, g: jnp.where(sn, bc - br + g, -jnp.inf), seen, b_c, b_r, ig_r)
    log_inter = each(lambda bc, m: bc + m, b_c, m_prev)
    m_t = each(lambda li, ld: jnp.maximum(li, jnp.max(ld, axis=1, keepdims=True)), log_inter, log_d)
    w_intra = each(lambda ld, mt, s: jnp.exp(ld - mt) * s, log_d, m_t, qk)
    w_inter = each(lambda li, mt: jnp.exp(li - mt), log_inter, m_t)
    wiv = each(lambda w, vv: _dot(w.astype(BF16), vv), w_intra, v)
    num = each(lambda wi, a, b: wi * a + b, w_inter, qc, wiv)
    den = each(lambda wi, a, n, w: wi * jnp.sum(a.astype(F32) * n, axis=1, keepdims=True)
               + jnp.sum(w, axis=1, keepdims=True), w_inter, q, n_prev, w_intra)
    for ch, nu, de, mt in zip(chains, num, den, m_t):
        ch['o'][...] = (nu / jnp.maximum(jnp.abs(de), jnp.exp(-mt))).astype(ch['o'].dtype)
    b_end = each(lambda c: jnp.sum(c, axis=0, keepdims=True), lf_c)
    log_w = each(lambda be, bc, g: be - bc + g, b_end, b_c, ig_c)
    m_new = each(lambda be, m, lw: jnp.maximum(be + m, jnp.max(lw, axis=0, keepdims=True)), b_end, m_prev, log_w)
    w = each(lambda lw, mn: jnp.exp(lw - mn), log_w, m_new)
    decay = each(lambda be, m, mn: jnp.exp(be + m - mn), b_end, m_prev, m_new)
    upd = each(lambda ww, vv, kk: lax.dot_general((ww * vv.astype(F32)).astype(BF16), kk, (((0,), (0,)), ((), ())),
                                                  preferred_element_type=F32), w, v, k)
    for ch, dc, cp, u_, np_, ww, kk, mn in zip(chains, decay, c_prev, upd, n_prev, w, k, m_new):
        ch['c'][...] = dc * cp + u_
        ch['n'][...] = dc * np_ + jnp.sum(ww * kk.astype(F32), axis=0, keepdims=True)
        ch['m'][...] = mn


def _mlstm_kernel(qf_ref, kf_ref, vf_ref, gcf_ref, grf_ref, qb_ref, kb_ref, vb_ref, gcb_ref, grb_ref,
                  hf_ref, hb_ref, c_scr, n_scr, m_scr):
    @pl.when(pl.program_id(1) == 0)
    def _():
        c_scr[...] = jnp.zeros(c_scr.shape, F32)
        n_scr[...] = jnp.zeros(n_scr.shape, F32)
        m_scr[...] = jnp.zeros(m_scr.shape, F32)

    dh = MLSTM_DIM
    chains = []
    for i in range(gcf_ref.shape[0]):
        cols = slice(i * dh, (i + 1) * dh)
        for j, (q_ref, k_ref, v_ref, gc_ref, gr_ref, o_ref) in enumerate(
                ((qf_ref, kf_ref, vf_ref, gcf_ref, grf_ref, hf_ref), (qb_ref, kb_ref, vb_ref, gcb_ref, grb_ref, hb_ref))):
            gc = gc_ref[i]
            gr = gr_ref[i]
            s = 2 * i + j
            chains.append(dict(q=q_ref[:, cols], k=k_ref[:, cols], v=v_ref[:, cols],
                               ig_c=gc[:, 2 * j:2 * j + 1], fg_c=gc[:, 2 * j + 1:2 * j + 2],
                               ig_r=gr[2 * j:2 * j + 1, :], fg_r=gr[2 * j + 1:2 * j + 2, :],
                               c=c_scr.at[s], n=n_scr.at[s], m=m_scr.at[s], o=o_ref.at[:, cols], reverse=j == 1))
    _mlstm_chains(chains)


def _mlstm_scan(qk, u, gates, n_lat, n_ctx):
    s_pad = qk.shape[0]
    L, dh, nh, hp = MLSTM_CHUNK, MLSTM_DIM, MLSTM_HEADS, MLSTM_HEADS_PER_STEP
    nlc, ncc = n_lat // L, n_ctx // L
    steps = nlc + ncc
    ng = nh // hp
    g4 = gates.reshape(s_pad, 2, 2, nh).transpose(3, 0, 1, 2).reshape(nh, s_pad, 4)
    g4t = g4.transpose(0, 2, 1)

    def fwd(c):
        return jnp.where(c < ncc, nlc + c, c - ncc)

    def bwd(c):
        return jnp.where(c < ncc, nlc + ncc - 1 - c, nlc - 1 - (c - ncc))

    def specs(pos):
        return [pl.BlockSpec((L, hp * dh), lambda h, c: (pos(c), h)),
                pl.BlockSpec((L, hp * dh), lambda h, c: (pos(c), ng + h)),
                pl.BlockSpec((L, hp * dh), lambda h, c: (pos(c), 2 * ng + h)),
                pl.BlockSpec((hp, L, 4), lambda h, c: (h, pos(c), 0)),
                pl.BlockSpec((hp, 4, L), lambda h, c: (h, 0, pos(c)))]

    out_shape = jax.ShapeDtypeStruct((steps * L, nh * dh), BF16)
    return pl.pallas_call(
        _mlstm_kernel,
        grid=(ng, steps),
        in_specs=specs(fwd) + specs(bwd),
        out_specs=[pl.BlockSpec((L, hp * dh), lambda h, c: (fwd(c), h)),
                   pl.BlockSpec((L, hp * dh), lambda h, c: (bwd(c), h))],
        out_shape=[out_shape, out_shape],
        scratch_shapes=[pltpu.VMEM((2 * hp, dh, dh), F32), pltpu.VMEM((2 * hp, 1, dh), F32),
                        pltpu.VMEM((2 * hp, 1, 1), F32)],
        compiler_params=_cparams(("parallel", "arbitrary")),
        name="mlstm_scan",
    )(qk, qk, u, g4, g4t, qk, qk, u, g4, g4t)


def kernel(x, c, ctx, c_ctx, w_mod_0, b_mod_0, g_pre_mix_0, g_post_mix_0, w_in_0, lambda_q1_0, lambda_k1_0, lambda_q2_0, lambda_k2_0, g_subln_0, g_q_0, g_k_0, w_out_0, g_pre_ffn_0, g_post_ffn_0, w_router_0, w_gate_0, w_up_0, w_down_0, w_mod_1, b_mod_1, g_pre_mix_1, g_post_mix_1, w_in_1, b_gate_1, w_conv_1, b_conv_1, g_head_1, w_out_1, g_pre_ffn_1, g_post_ffn_1, w_router_1, w_gate_1, w_up_1, w_down_1):
    b, n_lat, d = x.shape
    n_ctx = ctx.shape[1]
    assert b == 1 and n_lat % ROW_TILE == 0 and n_ctx % MLSTM_CHUNK == 0 and n_ctx <= ROW_TILE
    s_valid = n_lat + n_ctx
    s_pad = n_lat + ROW_TILE
    n_lat_tiles = n_lat // ROW_TILE

    xs = jnp.concatenate([x[0], ctx[0], jnp.zeros((s_pad - s_valid, d), F32)], axis=0)
    cc = jnp.concatenate([c, c_ctx[None, :], jnp.zeros((6, d), F32)], axis=0)

    mod0 = _modulation(cc, w_mod_0, b_mod_0)
    tabs = _rope_tables(n_lat, n_ctx, s_pad)
    p0 = _proj0(xs, mod0, g_pre_mix_0, w_in_0.astype(BF16), tabs, g_q_0, g_k_0, n_lat_tiles)
    lam_init = 0.8 - 0.6 * math.exp(-0.3 * 0)
    lam4 = jnp.stack([lambda_q1_0, lambda_k1_0, lambda_q2_0, lambda_k2_0])
    o_a = _attn_diff(p0, lam4, g_subln_0, n_lat_tiles, s_valid, lam_init)
    o_b = _attn_gqa(p0, n_lat, s_valid)
    ka = o_a.shape[1]
    x_mid, hl, aff = _mixer_out(
        _out0_kernel, (o_a, o_b),
        lambda tm: [pl.BlockSpec((tm, ka), lambda i: (i, 0)), pl.BlockSpec((tm, o_b.shape[1]), lambda i: (i, 0))],
        xs, mod0, w_out_0.astype(BF16), g_post_mix_0, g_pre_ffn_0, w_router_0, s_pad, n_lat // 256,
        lambda tm: [], "attn_out")
    p_moe0 = dict(w_gate=w_gate_0, w_up=w_up_0, w_down=w_down_0, g_post_ffn=g_post_ffn_0)
    xs = _moe(hl, aff, x_mid, mod0, p_moe0, n_lat, n_ctx, s_pad, True)

    mod1 = _modulation(cc, w_mod_1, b_mod_1)
    inner = MLSTM_HEADS * MLSTM_DIM
    u, gates = _proj1(xs, mod1, g_pre_mix_1, w_in_1[:, :4 * inner].astype(BF16), w_in_1[:, 4 * inner:], b_gate_1,
                      n_lat_tiles)
    qk = _conv_silu(u, w_conv_1, b_conv_1, 2 * inner, n_lat, n_ctx)
    hf, hb = _mlstm_scan(qk, u, gates, n_lat, n_ctx)
    og_blk = 3 * inner // inner
    x_mid, hl, aff = _mixer_out(
        _out1_kernel, (hf, hb, u, g_head_1.reshape(1, inner)),
        lambda tm: [pl.BlockSpec((tm, inner), lambda i: (i, 0)), pl.BlockSpec((tm, inner), lambda i: (i, 0)),
                    pl.BlockSpec((tm, inner), lambda i: (i, og_blk)), pl.BlockSpec((1, inner), lambda i: (0, 0))],
        xs, mod1, w_out_1.astype(BF16), g_post_mix_1, g_pre_ffn_1, w_router_1, n_lat, n_lat // 256,
        lambda tm: [pltpu.VMEM((tm, inner), BF16)], "mlstm_out")
    p_moe1 = dict(w_gate=w_gate_1, w_up=w_up_1, w_down=w_down_1, g_post_ffn=g_post_ffn_1)
    out = _moe(hl, aff, x_mid, mod1, p_moe1, n_lat, n_ctx, s_pad, False)
    return out[None]
```

```python
import functools
import math

import jax
import jax.numpy as jnp
from jax import lax
from jax.experimental import pallas as pl
from jax.experimental.pallas import tpu as pltpu

F32 = jnp.float32
BF16 = jnp.bfloat16
HIGHEST = lax.Precision.HIGHEST

GRID_W = 64
ROPE_THETA = 10000.0
RMS_EPS = 1e-6
N_MOD = 6
DIFF_HEADS = 8
DIFF_DIM = 64
GQA_HEADS = 8
GQA_KV_HEADS = 2
GQA_DIM = 128
MLSTM_HEADS = 8
MLSTM_DIM = 256
MLSTM_CHUNK = 128
N_EXPERTS = 16
EC_CAPACITY_FACTOR = 2

LANES = 128
ROW_TILE = 512
MLSTM_HEADS_PER_STEP = 4
FFN_MAX_TILE_ROWS = 1024
FFN_STEPS = 8
ATTN_TK = 512
ATTN_ROW_GROUP = 256
LOG2_E = math.log2(math.e)
TOPK_BLOCKS = 128
VMEM_LIMIT = 56 * 1024 * 1024


def _cparams(sem):
    return pltpu.CompilerParams(dimension_semantics=sem, vmem_limit_bytes=VMEM_LIMIT)


def _rms(x, g):
    return x * lax.rsqrt(jnp.mean(x * x, axis=-1, keepdims=True) + RMS_EPS) * g


def _dot(a, b):
    return jnp.dot(a, b, preferred_element_type=F32)


def _dot_nt(a, b, precision=None):
    return lax.dot_general(a, b, (((1,), (1,)), ((), ())), precision=precision, preferred_element_type=F32)


def _mod_kernel(c_ref, w_ref, b_ref, o_ref):
    c = c_ref[...]
    s = c * jax.nn.sigmoid(c)
    o_ref[...] = lax.dot_general(s, w_ref[...], (((1,), (0,)), ((), ())), precision=HIGHEST,
                                 preferred_element_type=F32) + b_ref[...]


def _modulation(cc, w_mod, b_mod):
    d, n = w_mod.shape
    tn = 1024
    return pl.pallas_call(
        _mod_kernel,
        grid=(n // tn,),
        in_specs=[pl.BlockSpec((8, d), lambda j: (0, 0)),
                  pl.BlockSpec((d, tn), lambda j: (0, j)),
                  pl.BlockSpec((1, tn), lambda j: (0, j))],
        out_specs=pl.BlockSpec((8, tn), lambda j: (0, j)),
        out_shape=jax.ShapeDtypeStruct((8, n), F32),
        compiler_params=_cparams(("arbitrary",)),
        name="modulation",
    )(cc, w_mod, b_mod.reshape(1, n))


def _mod_rows(mod_ref, r, k, d):
    return mod_ref[pl.ds(r, 1), k * d:(k + 1) * d]


def _premix(x, mod_ref, g_ref, r, d):
    return _rms(x, g_ref[...]) * (1.0 + _mod_rows(mod_ref, r, 1, d)) + _mod_rows(mod_ref, r, 0, d)


def _dot_hilo(a, b):
    a_hi = a.astype(BF16)
    a_lo = (a - a_hi.astype(F32)).astype(BF16)
    b_hi = b.astype(BF16)
    b_lo = (b - b_hi.astype(F32)).astype(BF16)
    return _dot(a_hi, b_hi) + (_dot(a_hi, b_lo) + _dot(a_lo, b_hi))


def _rope_a(x, ca, sa, sb):
    return x * ca + pltpu.roll(x, LANES - DIFF_DIM // 2, 1) * sa + pltpu.roll(x, DIFF_DIM // 2, 1) * sb


def _rope_b(x, cb, sb):
    return x * cb + pltpu.roll(x, GQA_DIM // 2, 1) * sb


def _proj0_kernel(x_ref, c_ref, mod_ref, gpre_ref, w_ref, ca_ref, saa_ref, sab_ref, cb_ref, sbb_ref, gq_ref, gk_ref,
                  o_ref, h_scr, *, n_lat_tiles, d, tn):
    i = pl.program_id(0)
    j = pl.program_id(1)
    ng = tn // LANES

    @pl.when(j == 0)
    def _():
        is_ctx = i >= n_lat_tiles
        x = jnp.where(is_ctx, c_ref[...], x_ref[...])
        h_scr[...] = _premix(x, mod_ref, gpre_ref, jnp.where(is_ctx, 1, 0), d).astype(BF16)

    acc = _dot(h_scr[...], w_ref[...])

    def groups(fn):
        for g in range(ng):
            o_ref[:, g * LANES:(g + 1) * LANES] = fn(acc[:, g * LANES:(g + 1) * LANES], g).astype(BF16)

    qa_tiles = DIFF_HEADS * 2 * DIFF_DIM // tn
    qb0 = 3 * qa_tiles
    qb_tiles = GQA_HEADS * GQA_DIM // tn

    @pl.when(j < 2 * qa_tiles)
    def _():
        sc = jnp.where(j < qa_tiles, DIFF_DIM ** -0.5 * LOG2_E, 1.0)
        groups(lambda a, g: _rope_a(a, ca_ref[...], saa_ref[...], sab_ref[...]) * sc)

    @pl.when(jnp.logical_and(j >= 2 * qa_tiles, j < qb0))
    def _():
        groups(lambda a, g: a)

    @pl.when(jnp.logical_and(j >= qb0, j < qb0 + qb_tiles))
    def _():
        groups(lambda a, g: _rope_b(_rms(a, gq_ref[...]), cb_ref[...], sbb_ref[...]) * (GQA_DIM ** -0.5 * LOG2_E))

    @pl.when(j == qb0 + qb_tiles)
    def _():
        def fn(a, g):
            if g < GQA_KV_HEADS:
                return _rope_b(_rms(a, gk_ref[...]), cb_ref[...], sbb_ref[...])
            return a
        groups(fn)


def _proj0(x, ctx_pad, mod, g_pre, w_in, tabs, g_q, g_k, n_lat_tiles):
    d = x.shape[1]
    s_pad = x.shape[0] + ctx_pad.shape[0]
    n_out = w_in.shape[1]
    tm, tn = ROW_TILE, 512
    assert GQA_KV_HEADS * GQA_DIM * 2 == tn
    row = lambda i, j: (i, 0)
    const = lambda i, j: (0, 0)
    tab_spec = pl.BlockSpec((tm, LANES), row)
    return pl.pallas_call(
        functools.partial(_proj0_kernel, n_lat_tiles=n_lat_tiles, d=d, tn=tn),
        grid=(s_pad // tm, n_out // tn),
        in_specs=[pl.BlockSpec((tm, d), lambda i, j: (jnp.minimum(i, n_lat_tiles - 1), 0)),
                  pl.BlockSpec((tm, d), const),
                  pl.BlockSpec((8, N_MOD * d), const),
                  pl.BlockSpec((1, d), const),
                  pl.BlockSpec((d, tn), lambda i, j: (0, j)),
                  tab_spec, tab_spec, tab_spec, tab_spec, tab_spec,
                  pl.BlockSpec((1, LANES), const),
                  pl.BlockSpec((1, LANES), const)],
        out_specs=pl.BlockSpec((tm, tn), lambda i, j: (i, j)),
        out_shape=jax.ShapeDtypeStruct((s_pad, n_out), BF16),
        scratch_shapes=[pltpu.VMEM((tm, d), BF16)],
        compiler_params=_cparams(("parallel", "arbitrary")),
        name="proj0",
    )(x, ctx_pad, mod, g_pre.reshape(1, d), w_in, *tabs, g_q.reshape(1, LANES), g_k.reshape(1, LANES))


def _rope_tables(n_lat, n_ctx, s_pad):
    rows = jnp.repeat(jnp.arange(n_lat // GRID_W, dtype=F32), GRID_W)
    cols = jnp.tile(jnp.arange(GRID_W, dtype=F32), n_lat // GRID_W)

    def angles(dim):
        axis_dim = dim // 2
        inv_freq = ROPE_THETA ** (-jnp.arange(0, axis_dim, 2, dtype=F32) / axis_dim)
        ang = jnp.concatenate([rows[:, None] * inv_freq, cols[:, None] * inv_freq], axis=-1)
        return jnp.cos(ang), jnp.sin(ang)

    def pad(t, fill):
        return jnp.concatenate([t, jnp.full((s_pad - n_lat, LANES), fill, F32)], axis=0)

    cos_a, sin_a = angles(DIFF_DIM)
    zero = jnp.zeros_like(sin_a)
    ca = jnp.tile(cos_a, (1, 4))
    saa = jnp.tile(jnp.concatenate([-sin_a, zero], axis=1), (1, 2))
    sab = jnp.tile(jnp.concatenate([zero, sin_a], axis=1), (1, 2))
    cos_b, sin_b = angles(GQA_DIM)
    cb = jnp.tile(cos_b, (1, 2))
    sbb = jnp.concatenate([-sin_b, sin_b], axis=1)
    return pad(ca, 1.0), pad(saa, 0.0), pad(sab, 0.0), pad(cb, 1.0), pad(sbb, 0.0)


def _flash_rows(q, k_ref, v_ref, m_scr, acc_scr, s_scr, is_ctx_q, tk, s_valid):
    n_chunks = k_ref.shape[0] // tk
    last = n_chunks - 1
    groups = [slice(g * ATTN_ROW_GROUP, (g + 1) * ATTN_ROW_GROUP) for g in range(q.shape[0] // ATTN_ROW_GROUP)]
    m_scr[...] = jnp.full(m_scr.shape, -jnp.inf, F32)
    acc_scr[...] = jnp.zeros(acc_scr.shape, F32)

    def rows_of(c):
        return pl.ds(c * tk if isinstance(c, int) else pl.multiple_of(c * tk, tk), tk)

    def scores(c, buf, rows):
        s_scr[buf, rows] = _dot_nt(q[rows], k_ref[rows_of(c), :])

    def consume(c, buf, rows, v1, masked):
        s = s_scr[buf, rows]
        if masked:
            kpos = c * tk + lax.broadcasted_iota(jnp.int32, (1, tk), 1)
            s = jnp.where(kpos < s_valid, s, -jnp.inf)
        m_prev = m_scr[rows]
        m_new = jnp.maximum(m_prev, jnp.max(s, axis=1, keepdims=True))
        p = jnp.exp2(s - m_new).astype(BF16)
        acc_scr[rows] = jnp.exp2(m_prev - m_new) * acc_scr[rows] + _dot(p, v1)
        m_scr[rows] = m_new

    def values(c):
        vc = v_ref[rows_of(c), :]
        return jnp.concatenate([vc, jnp.ones_like(vc)], axis=1)

    unroll = max(u for u in (2, 4, 8) if last % u == 0)
    lo = jnp.where(is_ctx_q, last, 0)
    for rows in groups:
        scores(lo, 0, rows)

    def body(i, carry):
        c = lo + unroll * i
        for j in range(unroll):
            v1 = values(c + j)
            for rows in groups:
                scores(c + j + 1, (j + 1) % 2, rows)
                consume(c + j, j % 2, rows, v1, False)
        return carry

    lax.fori_loop(0, (last - lo) // unroll, body, 0)
    v1 = values(last)
    for rows in groups:
        consume(last, 0, rows, v1, True)


def _attn_diff_kernel(q_ref, k_ref, v_ref, lam_ref, gsub_ref, o_ref, m_scr, acc_scr, s_scr,
                      *, n_lat_tiles, tk, s_valid, lam_init):
    q = q_ref[...]
    tq, hd = q.shape
    first = lax.broadcasted_iota(jnp.int32, (1, hd), 1) < DIFF_DIM
    zero = jnp.zeros_like(q)
    qq = jnp.concatenate([jnp.where(first, q, zero), jnp.where(first, zero, q)], axis=0)
    _flash_rows(qq, k_ref, v_ref, m_scr, acc_scr, s_scr, pl.program_id(1) >= n_lat_tiles, tk, s_valid)
    lam = (jnp.exp(jnp.sum(lam_ref[0:1, :] * lam_ref[1:2, :], axis=1, keepdims=True))
           - jnp.exp(jnp.sum(lam_ref[2:3, :] * lam_ref[3:4, :], axis=1, keepdims=True)) + lam_init)
    o = (acc_scr[0:tq, 0:hd] / acc_scr[0:tq, hd:hd + 1]
         - lam * (acc_scr[tq:2 * tq, 0:hd] / acc_scr[tq:2 * tq, hd:hd + 1]))
    o_ref[...] = (_rms(o, gsub_ref[...]) * (1.0 - lam_init)).astype(BF16)


def _attn_diff(p0, lam4, g_subln, n_lat_tiles, s_valid, lam_init):
    s_pad = p0.shape[0]
    tq = ROW_TILE
    hd = 2 * DIFF_DIM
    k0 = DIFF_HEADS
    v0 = 2 * DIFF_HEADS
    return pl.pallas_call(
        functools.partial(_attn_diff_kernel, n_lat_tiles=n_lat_tiles, tk=ATTN_TK, s_valid=s_valid, lam_init=lam_init),
        grid=(DIFF_HEADS, s_pad // tq),
        in_specs=[pl.BlockSpec((tq, hd), lambda h, iq: (iq, h)),
                  pl.BlockSpec((s_pad, hd), lambda h, iq: (0, k0 + h)),
                  pl.BlockSpec((s_pad, hd), lambda h, iq: (0, v0 + h)),
                  pl.BlockSpec((4, DIFF_DIM), lambda h, iq: (0, 0)),
                  pl.BlockSpec((1, hd), lambda h, iq: (0, 0))],
        out_specs=pl.BlockSpec((tq, hd), lambda h, iq: (iq, h)),
        out_shape=jax.ShapeDtypeStruct((s_pad, DIFF_HEADS * hd), BF16),
        scratch_shapes=[pltpu.VMEM((2 * tq, 1), F32), pltpu.VMEM((2 * tq, 2 * hd), F32),
                        pltpu.VMEM((2, 2 * tq, ATTN_TK), F32)],
        compiler_params=_cparams(("parallel", "arbitrary")),
        name="attn_diff",
    )(p0, p0, p0, lam4, g_subln.reshape(1, hd))


def _attn_gqa_kernel(q_ref, k_ref, v_ref, o_ref, m_scr, acc_scr, s_scr, *, n_lat_tiles, tk, s_valid, group):
    tq = q_ref.shape[0]
    dh = GQA_DIM
    qq = jnp.concatenate([q_ref[:, g * dh:(g + 1) * dh] for g in range(group)], axis=0)
    _flash_rows(qq, k_ref, v_ref, m_scr, acc_scr, s_scr, pl.program_id(1) >= n_lat_tiles, tk, s_valid)
    for g in range(group):
        rows = slice(g * tq, (g + 1) * tq)
        o_ref[:, g * dh:(g + 1) * dh] = (acc_scr[rows, 0:dh] / acc_scr[rows, dh:dh + 1]).astype(BF16)


def _attn_gqa(p0, n_lat, s_valid):
    s_pad = p0.shape[0]
    tq = 256
    group = GQA_HEADS // GQA_KV_HEADS
    qw = group * GQA_DIM
    q0 = 3 * DIFF_HEADS * 2 * DIFF_DIM // qw
    k0 = (3 * DIFF_HEADS * 2 * DIFF_DIM + GQA_HEADS * GQA_DIM) // GQA_DIM
    v0 = k0 + GQA_KV_HEADS
    return pl.pallas_call(
        functools.partial(_attn_gqa_kernel, n_lat_tiles=n_lat // tq, tk=ATTN_TK, s_valid=s_valid, group=group),
        grid=(GQA_KV_HEADS, s_pad // tq),
        in_specs=[pl.BlockSpec((tq, qw), lambda h, iq: (iq, q0 + h)),
                  pl.BlockSpec((s_pad, GQA_DIM), lambda h, iq: (0, k0 + h)),
                  pl.BlockSpec((s_pad, GQA_DIM), lambda h, iq: (0, v0 + h))],
        out_specs=pl.BlockSpec((tq, qw), lambda h, iq: (iq, h)),
        out_shape=jax.ShapeDtypeStruct((s_pad, GQA_HEADS * GQA_DIM), BF16),
        scratch_shapes=[pltpu.VMEM((group * tq, 1), F32), pltpu.VMEM((group * tq, 2 * GQA_DIM), F32),
                        pltpu.VMEM((2, group * tq, ATTN_TK), F32)],
        compiler_params=_cparams(("parallel", "arbitrary")),
        name="attn_gqa",
    )(p0, p0, p0)


def _post_mix(y, x, mod_ref, r, gpost_ref, gffn_ref, wr_ref, xo_ref, hl_ref, aff_ref, d):
    xn = x + _mod_rows(mod_ref, r, 2, d) * _rms(y, gpost_ref[...])
    xo_ref[...] = xn
    hl = _rms(xn, gffn_ref[...]) * (1.0 + _mod_rows(mod_ref, r, 4, d)) + _mod_rows(mod_ref, r, 3, d)
    hl_ref[...] = hl
    logits = _dot_hilo(hl, wr_ref[...])
    e = jnp.exp(logits - jnp.max(logits, axis=1, keepdims=True))
    aff_ref[...] = e / jnp.sum(e, axis=1, keepdims=True)


def _out0_kernel(oa_ref, ob_ref, c_ref, x_ref, mod_ref, w_ref, gpost_ref, gffn_ref, wr_ref, xo_ref, hl_ref, aff_ref,
                 *, n_lat_tiles, d):
    is_ctx = pl.program_id(0) >= n_lat_tiles
    r = jnp.where(is_ctx, 1, 0)
    ka = oa_ref.shape[1]
    y = _dot(oa_ref[...], w_ref[0:ka, :]) + _dot(ob_ref[...], w_ref[ka:, :])
    x = jnp.where(is_ctx, c_ref[...], x_ref[...])
    _post_mix(y, x, mod_ref, r, gpost_ref, gffn_ref, wr_ref, xo_ref, hl_ref, aff_ref, d)


def _out1_kernel(hf_ref, hb_ref, og_ref, ghead_ref, x_ref, mod_ref, w_ref, gpost_ref, gffn_ref, wr_ref,
                 xo_ref, hl_ref, aff_ref, y_scr, *, n_lat_tiles, d):
    r = jnp.where(pl.program_id(0) >= n_lat_tiles, 1, 0)
    for h in range(MLSTM_HEADS):
        sl = slice(h * MLSTM_DIM, (h + 1) * MLSTM_DIM)
        hs = hf_ref[:, sl].astype(F32) + hb_ref[:, sl].astype(F32)
        hn = _rms(hs, ghead_ref[:, sl])
        y_scr[:, sl] = (jax.nn.sigmoid(og_ref[:, sl].astype(F32)) * hn).astype(BF16)
    y = _dot(y_scr[...], w_ref[...])
    _post_mix(y, x_ref[...], mod_ref, r, gpost_ref, gffn_ref, wr_ref, xo_ref, hl_ref, aff_ref, d)


def _mixer_out(kernel_fn, mix_inputs, mix_specs, x, mod, w_out, g_post, g_ffn, w_router, n_rows, n_lat_tiles_tm,
               scratch, name):
    d = x.shape[1]
    tm = 256
    row = lambda i: (i, 0)
    x_row = lambda i: (jnp.minimum(i, x.shape[0] // tm - 1), 0)
    const = lambda i: (0, 0)
    ne = w_router.shape[1]
    return pl.pallas_call(
        functools.partial(kernel_fn, n_lat_tiles=n_lat_tiles_tm, d=d),
        grid=(n_rows // tm,),
        in_specs=mix_specs(tm) + [pl.BlockSpec((tm, d), x_row),
                                  pl.BlockSpec((8, N_MOD * d), const),
                                  pl.BlockSpec(w_out.shape, const),
                                  pl.BlockSpec((1, d), const),
                                  pl.BlockSpec((1, d), const),
                                  pl.BlockSpec((d, ne), const)],
        out_specs=[pl.BlockSpec((tm, d), row), pl.BlockSpec((tm, d), row), pl.BlockSpec((tm, ne), row)],
        out_shape=[jax.ShapeDtypeStruct((n_rows, d), F32), jax.ShapeDtypeStruct((n_rows, d), F32),
                   jax.ShapeDtypeStruct((n_rows, ne), F32)],
        scratch_shapes=scratch(tm),
        compiler_params=_cparams(("parallel",)),
        name=name,
    )(*mix_inputs, x, mod, w_out, g_post.reshape(1, d), g_ffn.reshape(1, d), w_router)


def _incl_cumsum(maskf, upper, strict_lower):
    mb = maskf.astype(BF16)
    local = _dot(mb, upper)
    start = jnp.sum(_dot(strict_lower, mb), axis=1, keepdims=True)
    return local, start


def _topk_kernel(aff_ref, idx_ref, gate_ref, dest_ref, cnt_ref, cnt_acc, *, cap, dest_stride, row_off):
    e = pl.program_id(0)
    ne = pl.num_programs(0)
    nb = aff_ref.shape[1]
    a = aff_ref[0]
    bits = pltpu.bitcast(a, jnp.int32)

    thr = jnp.zeros((1, 1), jnp.int32)
    for bit in range(30, -1, -1):
        cand = thr | (1 << bit)
        c = jnp.sum(jnp.sum(jnp.where(bits >= cand, 1.0, 0.0), axis=1, keepdims=True), axis=0, keepdims=True)
        thr = jnp.where(c >= cap, cand, thr)

    r_i = lax.broadcasted_iota(jnp.int32, (LANES, LANES), 0)
    c_i = lax.broadcasted_iota(jnp.int32, (LANES, LANES), 1)
    upper = jnp.where(r_i <= c_i, 1.0, 0.0).astype(BF16)
    rb = lax.broadcasted_iota(jnp.int32, (nb, nb), 0)
    cb = lax.broadcasted_iota(jnp.int32, (nb, nb), 1)
    strict_lower = jnp.where(cb < rb, 1.0, 0.0).astype(BF16)

    gt = bits > thr
    eq = bits == thr
    n_gt = jnp.sum(jnp.sum(jnp.where(gt, 1.0, 0.0), axis=1, keepdims=True), axis=0, keepdims=True)
    eq_local, eq_start = _incl_cumsum(jnp.where(eq, 1.0, 0.0), upper, strict_lower)
    sel = jnp.logical_or(gt, jnp.logical_and(eq, eq_local + eq_start <= cap - n_gt))
    self_ = jnp.where(sel, 1.0, 0.0)
    sel_local, sel_start = _incl_cumsum(self_, upper, strict_lower)
    sel_tot = jnp.sum(self_, axis=1, keepdims=True)

    @pl.when(e == 0)
    def _():
        cnt_acc[...] = jnp.zeros(cnt_acc.shape, F32)

    earlier = cnt_acc[...]
    cnt_acc[...] = earlier + self_

    lane = lax.broadcasted_iota(jnp.int32, (nb, LANES), 1)
    cols = jnp.where(lane == 0, sel_start, jnp.where(lane == 1, sel_tot, 0.0))
    rows = cols.T
    start_row = rows[0:1, 0:nb]
    tot_row = rows[1:2, 0:nb]

    p = lax.broadcasted_iota(jnp.int32, (cap, 1), 0).astype(F32)
    onehot = jnp.where(jnp.logical_and(start_row <= p, p < start_row + tot_row), 1.0, 0.0)
    blk = lax.broadcasted_iota(jnp.int32, (1, nb), 1).astype(F32)
    p_start = jnp.sum(onehot * start_row, axis=1, keepdims=True)
    p_blk = jnp.sum(onehot * blk, axis=1, keepdims=True)
    ohb = onehot.astype(BF16)
    local_rows = _dot(ohb, sel_local.astype(BF16))
    sel_rows = _dot(ohb, self_.astype(BF16))
    earlier_rows = _dot(ohb, earlier.astype(BF16))
    aff_rows = lax.dot_general(onehot, a, (((1,), (0,)), ((), ())), precision=HIGHEST, preferred_element_type=F32)
    hit = jnp.where(jnp.logical_and(local_rows == p - p_start + 1.0, sel_rows > 0.5), 1.0, 0.0)
    lane_f = lax.broadcasted_iota(jnp.int32, (1, LANES), 1).astype(F32)
    tok = p_blk * LANES + jnp.sum(hit * lane_f, axis=1, keepdims=True)
    gate = jnp.sum(hit * aff_rows, axis=1, keepdims=True)
    rank = jnp.sum(hit * earlier_rows, axis=1, keepdims=True)
    idx_ref[0] = tok.astype(jnp.int32) + row_off
    gate_ref[0] = gate
    dest_ref[0] = (rank * dest_stride + tok).astype(jnp.int32) + row_off

    @pl.when(e == ne - 1)
    def _():
        cnt_ref[...] = cnt_acc[...].astype(jnp.int32)


def _route(aff_t, cap, dest_stride, row_off):
    ne, n = aff_t.shape
    n_pad = TOPK_BLOCKS * LANES
    assert n <= n_pad
    a = jnp.pad(aff_t, ((0, 0), (0, n_pad - n)), constant_values=-1.0).reshape(ne, TOPK_BLOCKS, LANES)
    slot = pl.BlockSpec((1, cap, 1), lambda e: (e, 0, 0))
    idx, gate, dest, cnt = pl.pallas_call(
        functools.partial(_topk_kernel, cap=cap, dest_stride=dest_stride, row_off=row_off),
        grid=(ne,),
        in_specs=[pl.BlockSpec((1, TOPK_BLOCKS, LANES), lambda e: (e, 0, 0))],
        out_specs=[slot, slot, slot, pl.BlockSpec((TOPK_BLOCKS, LANES), lambda e: (0, 0))],
        out_shape=[jax.ShapeDtypeStruct((ne, cap, 1), jnp.int32), jax.ShapeDtypeStruct((ne, cap, 1), F32),
                   jax.ShapeDtypeStruct((ne, cap, 1), jnp.int32), jax.ShapeDtypeStruct((TOPK_BLOCKS, LANES), jnp.int32)],
        scratch_shapes=[pltpu.VMEM((TOPK_BLOCKS, LANES), F32)],
        compiler_params=_cparams(("arbitrary",)),
        name="route_topk",
    )(a)
    return idx.reshape(ne * cap), gate, dest.reshape(ne * cap), cnt.reshape(n_pad)[:n]


def _pack_bf16_pairs(y):
    n = y.shape[1] // 2
    bits = lax.bitcast_convert_type(y.astype(BF16).astype(F32), jnp.uint32)
    return (bits[:, n:] & jnp.uint32(0xFFFF0000)) | (bits[:, :n] >> 16)


def _unpack_bf16_pairs(w):
    return (lax.bitcast_convert_type(w << 16, F32), lax.bitcast_convert_type(w & jnp.uint32(0xFFFF0000), F32))


def _row_copy(src, s, dst, t, sem):
    return pltpu.make_async_copy(src.at[pl.ds(s, 1)], dst.at[pl.ds(t, 1)], sem)


def _ffn_body(idx_sm, dest_sm, hl_hbm, gate_ref, wg_ref, wu_ref, wd_ref, z_hbm, xbuf, xb, acc, zp, gsem, ssem, *, tm):
    f = pl.program_id(2)
    nm = pl.num_programs(1)
    t = pl.program_id(0) * nm + pl.program_id(1)
    n_tiles = pl.num_programs(0) * nm
    cur = t % 2
    nxt = 1 - cur
    chunk = tm // FFN_STEPS
    last_step = f == pl.num_programs(2) - 1
    t_next = jnp.minimum(t + 1, n_tiles - 1)
    t_prev = jnp.maximum(t - 1, 0)

    def gather(tile, slot, r):
        return _row_copy(hl_hbm, idx_sm[tile * tm + r], xbuf.at[slot], r, gsem.at[slot])

    def scatter(tile, slot, r):
        return _row_copy(zp.at[slot], r, z_hbm, dest_sm[tile * tm + r], ssem.at[slot])

    def for_rows(fn):
        def body(r, c):
            fn(r)
            return c
        lax.fori_loop(0, tm, body, 0, unroll=8)

    @pl.when(jnp.logical_and(t == 0, f == 0))
    def _():
        for_rows(lambda r: gather(0, 0, r).start())
        zp[1] = jnp.zeros(zp.shape[1:], jnp.uint32)

    @pl.when(f == 0)
    def _():
        for_rows(lambda r: gather(t, cur, r).wait())
        xb[...] = xbuf[cur].astype(BF16)
        acc[...] = jnp.zeros(acc.shape, F32)

        @pl.when(t >= 1)
        def _():
            for_rows(lambda r: scatter(t_prev, cur, r).wait())

    for r in range(chunk):
        gather(t_next, nxt, f * chunk + r).start()
        scatter(t_prev, nxt, f * chunk + r).start()

    x = xb[...]
    g = _dot(x, wg_ref[0].astype(BF16))
    u = _dot(x, wu_ref[0].astype(BF16))
    hid = (g * jax.nn.sigmoid(g) * u).astype(BF16)
    acc[...] += _dot(hid, wd_ref[0].astype(BF16))

    @pl.when(last_step)
    def _():
        zp[cur] = _pack_bf16_pairs(acc[...] * gate_ref[0])

    @pl.when(jnp.logical_and(t == n_tiles - 1, last_step))
    def _():
        for_rows(lambda r: gather(t, nxt, r).wait())
        for_rows(lambda r: scatter(t_prev, nxt, r).wait())
        for_rows(lambda r: scatter(t, cur, r).start())
        for_rows(lambda r: scatter(t, cur, r).wait())


def _expert_ffn(idx, dest, hl, gate, w_gate, w_up, w_down, z_rows, n_tiles):
    ne, d, ff = w_gate.shape
    cap = gate.shape[1]
    tm = cap // n_tiles
    assert tm * n_tiles == cap and tm % FFN_STEPS == 0 and tm % 16 == 0
    tf = ff // FFN_STEPS
    any_spec = pl.BlockSpec(memory_space=pl.ANY)
    return pl.pallas_call(
        functools.partial(_ffn_body, tm=tm),
        grid_spec=pltpu.PrefetchScalarGridSpec(
            num_scalar_prefetch=2,
            grid=(ne, n_tiles, FFN_STEPS),
            in_specs=[any_spec,
                      pl.BlockSpec((1, tm, 1), lambda e, m, f, *_: (e, m, 0)),
                      pl.BlockSpec((1, d, tf), lambda e, m, f, *_: (e, 0, f)),
                      pl.BlockSpec((1, d, tf), lambda e, m, f, *_: (e, 0, f)),
                      pl.BlockSpec((1, tf, d), lambda e, m, f, *_: (e, f, 0))],
            out_specs=any_spec,
            scratch_shapes=[pltpu.VMEM((2, tm, d), F32), pltpu.VMEM((tm, d), BF16), pltpu.VMEM((tm, d), F32),
                            pltpu.VMEM((2, tm, d // 2), jnp.uint32),
                            pltpu.SemaphoreType.DMA((2,)), pltpu.SemaphoreType.DMA((2,))]),
        out_shape=jax.ShapeDtypeStruct((z_rows, d // 2), jnp.uint32),
        compiler_params=_cparams(("arbitrary", "arbitrary", "arbitrary")),
        name="expert_ffn",
    )(idx, dest, hl, gate, w_gate, w_up, w_down)


def _combine_kernel(maxc_sm, z_ref, cnt_ref, x_ref, mod_ref, g_ref, o_ref, acc, *, n_lat_tiles, d):
    i = pl.program_id(0)
    k = pl.program_id(1)
    nk = pl.num_programs(1)

    @pl.when(k == 0)
    def _():
        acc[...] = jnp.zeros(acc.shape, F32)

    @pl.when(k < maxc_sm[i])
    def _():
        lo, hi = _unpack_bf16_pairs(z_ref[...])
        live = cnt_ref[...] > k
        acc[:, 0:d // 2] += jnp.where(live, lo, 0.0)
        acc[:, d // 2:d] += jnp.where(live, hi, 0.0)

    @pl.when(k == nk - 1)
    def _():
        r = jnp.where(i >= n_lat_tiles, 1, 0)
        o_ref[...] = x_ref[...] + _mod_rows(mod_ref, r, 5, d) * _rms(acc[...], g_ref[...])


def _combine(z, cnt, x, mod, g_post, n_rows, z_stride, n_lat_tiles_tb):
    d = x.shape[1]
    tb = ROW_TILE
    nb = n_rows // tb
    zb = z_stride // tb
    maxc = jnp.max(cnt[:n_rows].reshape(nb, tb), axis=1)

    def z_map(i, k, maxc_sm):
        return (jnp.maximum(jnp.minimum(k, maxc_sm[i] - 1), 0) * zb + i, 0)

    return pl.pallas_call(
        functools.partial(_combine_kernel, n_lat_tiles=n_lat_tiles_tb, d=d),
        grid_spec=pltpu.PrefetchScalarGridSpec(
            num_scalar_prefetch=1,
            grid=(nb, N_EXPERTS),
            in_specs=[pl.BlockSpec((tb, d // 2), z_map),
                      pl.BlockSpec((tb, 1), lambda i, k, s: (i, 0)),
                      pl.BlockSpec((tb, d), lambda i, k, s: (i, 0)),
                      pl.BlockSpec((8, N_MOD * d), lambda i, k, s: (0, 0)),
                      pl.BlockSpec((1, d), lambda i, k, s: (0, 0))],
            out_specs=pl.BlockSpec((tb, d), lambda i, k, s: (i, 0)),
            scratch_shapes=[pltpu.VMEM((tb, d), F32)]),
        out_shape=jax.ShapeDtypeStruct((n_rows, d), F32),
        compiler_params=_cparams(("parallel", "arbitrary")),
        name="moe_combine",
    )(maxc, z, cnt.reshape(-1, 1), x, mod, g_post.reshape(1, d))


def _moe(hl, aff, x_mid, mod, p, n_lat, n_ctx, s_pad, with_ctx):
    aff_t = aff.T
    ne = N_EXPERTS
    cap_l = EC_CAPACITY_FACTOR * n_lat // ne
    n_tiles = max(1, cap_l // FFN_MAX_TILE_ROWS)
    idx, gate, dest, cnt = _route(aff_t[:, :n_lat], cap_l, s_pad, 0)
    n_rows = n_lat
    if with_ctx:
        cap_c = EC_CAPACITY_FACTOR * n_ctx // ne
        idx_c, gate_c, dest_c, cnt_c = _route(aff_t[:, n_lat:n_lat + n_ctx], cap_c, s_pad, n_lat)

        def per_tile(a, b):
            return jnp.concatenate([a.reshape(ne, n_tiles, cap_l // n_tiles), b.reshape(ne, n_tiles, cap_c // n_tiles)],
                                   axis=2).reshape(ne, cap_l + cap_c)
        idx = per_tile(idx, idx_c).reshape(-1)
        dest = per_tile(dest, dest_c).reshape(-1)
        gate = per_tile(gate, gate_c)[:, :, None]
        cnt = jnp.concatenate([cnt, cnt_c, jnp.zeros((s_pad - n_lat - n_ctx,), jnp.int32)])
        n_rows = s_pad
    z = _expert_ffn(idx, dest, hl, gate, p['w_gate'], p['w_up'], p['w_down'], ne * s_pad, n_tiles)
    return _combine(z, cnt, x_mid, mod, p['g_post_ffn'], n_rows, s_pad, n_lat // ROW_TILE)


def _proj1_kernel(x_ref, mod_ref, gpre_ref, w_ref, wg_ref, bg_ref, o_ref, g_ref, h_scr, *, n_lat_tiles, d):
    i = pl.program_id(0)
    j = pl.program_id(1)

    @pl.when(j == 0)
    def _():
        r = jnp.where(i >= n_lat_tiles, 1, 0)
        h = _premix(x_ref[...], mod_ref, gpre_ref, r, d)
        h_scr[...] = h.astype(BF16)
        g_ref[...] = _dot_hilo(h, wg_ref[...]) + bg_ref[...]

    o_ref[...] = _dot(h_scr[...], w_ref[...]).astype(BF16)


def _proj1(xs, mod, g_pre, w_main, n_out, w_gates, b_gate, n_lat_tiles):
    s_pad, d = xs.shape
    ng = w_gates.shape[1]
    tm, tn = ROW_TILE, 1024
    row = lambda i, j: (i, 0)
    const = lambda i, j: (0, 0)
    return pl.pallas_call(
        functools.partial(_proj1_kernel, n_lat_tiles=n_lat_tiles, d=d),
        grid=(s_pad // tm, n_out // tn),
        in_specs=[pl.BlockSpec((tm, d), row),
                  pl.BlockSpec((8, N_MOD * d), const),
                  pl.BlockSpec((1, d), const),
                  pl.BlockSpec((d, tn), lambda i, j: (0, j)),
                  pl.BlockSpec((d, ng), const),
                  pl.BlockSpec((1, ng), const)],
        out_specs=[pl.BlockSpec((tm, tn), lambda i, j: (i, j)), pl.BlockSpec((tm, ng), row)],
        out_shape=[jax.ShapeDtypeStruct((s_pad, n_out), BF16), jax.ShapeDtypeStruct((s_pad, ng), F32)],
        scratch_shapes=[pltpu.VMEM((tm, d), BF16)],
        compiler_params=_cparams(("parallel", "arbitrary")),
        name="proj1",
    )(xs, mod, g_pre.reshape(1, d), w_main, w_gates, b_gate.reshape(1, ng))


def _conv_kernel(cur_ref, prev_ref, next_ref, w_ref, b_ref, o_ref, *, n_lat, n_ctx, k_cols0, k_scale):
    i = pl.program_id(0)
    j = pl.program_id(1)
    tm, tc = cur_ref.shape
    hr = prev_ref.shape[0]
    cur = cur_ref[...].astype(F32)
    row = lax.broadcasted_iota(jnp.int32, (tm, 1), 0)
    g = i * tm + row
    prev_row = prev_ref[hr - 1:hr, :].astype(F32)
    next_row = next_ref[0:1, :].astype(F32)
    before = jnp.where(row == 0, prev_row, pltpu.roll(cur, 1, 0))
    after = jnp.where(row == tm - 1, next_row, pltpu.roll(cur, tm - 1, 0))
    before = jnp.where(jnp.logical_or(g == 0, g == n_lat), 0.0, before)
    after = jnp.where(jnp.logical_or(g == n_lat - 1, g == n_lat + n_ctx - 1), 0.0, after)
    y = before * w_ref[0:1, :] + cur * w_ref[1:2, :] + after * w_ref[2:3, :] + b_ref[...]
    y = y * jax.nn.sigmoid(y)
    o_ref[...] = (y * jnp.where(j * tc >= k_cols0, k_scale, 1.0)).astype(BF16)


def _conv_silu(u, w_conv, b_conv, n_cols, n_lat, n_ctx):
    s_pad = u.shape[0]
    tm, tc, hr = ROW_TILE, 1024, 16
    rb = tm // hr
    last = s_pad // hr - 1
    return pl.pallas_call(
        functools.partial(_conv_kernel, n_lat=n_lat, n_ctx=n_ctx, k_cols0=n_cols // 2, k_scale=MLSTM_DIM ** -0.5),
        grid=(s_pad // tm, n_cols // tc),
        in_specs=[pl.BlockSpec((tm, tc), lambda i, j: (i, j)),
                  pl.BlockSpec((hr, tc), lambda i, j: (jnp.maximum(i * rb - 1, 0), j)),
                  pl.BlockSpec((hr, tc), lambda i, j: (jnp.minimum((i + 1) * rb, last), j)),
                  pl.BlockSpec((8, tc), lambda i, j: (0, j)),
                  pl.BlockSpec((1, tc), lambda i, j: (0, j))],
        out_specs=pl.BlockSpec((tm, tc), lambda i, j: (i, j)),
        out_shape=jax.ShapeDtypeStruct((s_pad, n_cols), BF16),
        compiler_params=_cparams(("parallel", "parallel")),
        name="conv_silu",
    )(u, u, u, jnp.pad(w_conv, ((0, 8 - w_conv.shape[0]), (0, 0))), b_conv.reshape(1, n_cols))


def _log_sigmoid(x):
    return -(jnp.maximum(-x, 0.0) + jnp.log1p(jnp.exp(-jnp.abs(x))))


def _mlstm_chains(chains):
    def each(fn, *lists):
        return [fn(*a) for a in zip(*lists)]

    L = chains[0]['q'].shape[0]
    t_i = lax.broadcasted_iota(jnp.int32, (L, L), 0)
    s_i = lax.broadcasted_iota(jnp.int32, (L, L), 1)
    seen = [(s_i >= t_i) if ch['reverse'] else (s_i <= t_i) for ch in chains]
    q = [ch['q'] for ch in chains]
    k = [ch['k'] for ch in chains]
    v = [ch['v'] for ch in chains]
    ig_c = [ch['ig_c'] for ch in chains]
    ig_r = [ch['ig_r'] for ch in chains]
    lf_c = [_log_sigmoid(ch['fg_c']) for ch in chains]
    lf_r = [_log_sigmoid(ch['fg_r']) for ch in chains]
    m_prev = [ch['m'][...] for ch in chains]
    c_prev = [ch['c'][...] for ch in chains]
    n_prev = [ch['n'][...] for ch in chains]
    qk = each(_dot_nt, q, k)
    qc = each(lambda a, c: _dot_nt(a, c.astype(BF16)), q, c_prev)
    b_c = each(lambda sn, r: jnp.sum(jnp.where(sn, 1.0, 0.0) * r, axis=1, keepdims=True), seen, lf_r)
    b_r = each(lambda sn, c, r: jnp.sum(jnp.where(sn, 0.0, 1.0) * c, axis=0, keepdims=True) + r, seen, lf_c, lf_r)
    log_d = each(lambda sn, bc, br, g: jnp.where(sn, bc - br + g, -jnp.inf), seen, b_c, b_r, ig_r)
    log_inter = each(lambda bc, m: bc + m, b_c, m_prev)
    m_t = each(lambda li, ld: jnp.maximum(li, jnp.max(ld, axis=1, keepdims=True)), log_inter, log_d)
    w_intra = each(lambda ld, mt, s: jnp.exp(ld - mt) * s, log_d, m_t, qk)
    w_inter = each(lambda li, mt: jnp.exp(li - mt), log_inter, m_t)
    wiv = each(lambda w, vv: _dot(w.astype(BF16), vv), w_intra, v)
    num = each(lambda wi, a, b: wi * a + b, w_inter, qc, wiv)
    den = each(lambda wi, a, n, w: wi * jnp.sum(a.astype(F32) * n, axis=1, keepdims=True)
               + jnp.sum(w, axis=1, keepdims=True), w_inter, q, n_prev, w_intra)
    for ch, nu, de, mt in zip(chains, num, den, m_t):
        ch['o'][...] = (nu / jnp.maximum(jnp.abs(de), jnp.exp(-mt))).astype(ch['o'].dtype)
    b_end = each(lambda c: jnp.sum(c, axis=0, keepdims=True), lf_c)
    log_w = each(lambda be, bc, g: be - bc + g, b_end, b_c, ig_c)
    m_new = each(lambda be, m, lw: jnp.maximum(be + m, jnp.max(lw, axis=0, keepdims=True)), b_end, m_prev, log_w)
    w = each(lambda lw, mn: jnp.exp(lw - mn), log_w, m_new)
    decay = each(lambda be, m, mn: jnp.exp(be + m - mn), b_end, m_prev, m_new)
    upd = each(lambda ww, vv, kk: lax.dot_general((ww * vv.astype(F32)).astype(BF16), kk, (((0,), (0,)), ((), ())),
                                                  preferred_element_type=F32), w, v, k)
    for ch, dc, cp, u_, np_, ww, kk, mn in zip(chains, decay, c_prev, upd, n_prev, w, k, m_new):
        ch['c'][...] = dc * cp + u_
        ch['n'][...] = dc * np_ + jnp.sum(ww * kk.astype(F32), axis=0, keepdims=True)
        ch['m'][...] = mn


def _mlstm_kernel(qf_ref, kf_ref, vf_ref, gcf_ref, grf_ref, qb_ref, kb_ref, vb_ref, gcb_ref, grb_ref,
                  hf_ref, hb_ref, c_scr, n_scr, m_scr):
    @pl.when(pl.program_id(1) == 0)
    def _():
        c_scr[...] = jnp.zeros(c_scr.shape, F32)
        n_scr[...] = jnp.zeros(n_scr.shape, F32)
        m_scr[...] = jnp.zeros(m_scr.shape, F32)

    dh = MLSTM_DIM
    chains = []
    for i in range(gcf_ref.shape[0]):
        cols = slice(i * dh, (i + 1) * dh)
        for j, (q_ref, k_ref, v_ref, gc_ref, gr_ref, o_ref) in enumerate(
                ((qf_ref, kf_ref, vf_ref, gcf_ref, grf_ref, hf_ref), (qb_ref, kb_ref, vb_ref, gcb_ref, grb_ref, hb_ref))):
            gc = gc_ref[i]
            gr = gr_ref[i]
            s = 2 * i + j
            chains.append(dict(q=q_ref[:, cols], k=k_ref[:, cols], v=v_ref[:, cols],
                               ig_c=gc[:, 2 * j:2 * j + 1], fg_c=gc[:, 2 * j + 1:2 * j + 2],
                               ig_r=gr[2 * j:2 * j + 1, :], fg_r=gr[2 * j + 1:2 * j + 2, :],
                               c=c_scr.at[s], n=n_scr.at[s], m=m_scr.at[s], o=o_ref.at[:, cols], reverse=j == 1))
    _mlstm_chains(chains)


def _mlstm_scan(qk, u, gates, n_lat, n_ctx):
    s_pad = qk.shape[0]
    L, dh, nh, hp = MLSTM_CHUNK, MLSTM_DIM, MLSTM_HEADS, MLSTM_HEADS_PER_STEP
    nlc, ncc = n_lat // L, n_ctx // L
    steps = nlc + ncc
    ng = nh // hp
    g4 = gates.reshape(s_pad, 2, 2, nh).transpose(3, 0, 1, 2).reshape(nh, s_pad, 4)
    g4t = g4.transpose(0, 2, 1)

    def fwd(c):
        return jnp.where(c < ncc, nlc + c, c - ncc)

    def bwd(c):
        return jnp.where(c < ncc, nlc + ncc - 1 - c, nlc - 1 - (c - ncc))

    def specs(pos):
        return [pl.BlockSpec((L, hp * dh), lambda h, c: (pos(c), h)),
                pl.BlockSpec((L, hp * dh), lambda h, c: (pos(c), ng + h)),
                pl.BlockSpec((L, hp * dh), lambda h, c: (pos(c), 2 * ng + h)),
                pl.BlockSpec((hp, L, 4), lambda h, c: (h, pos(c), 0)),
                pl.BlockSpec((hp, 4, L), lambda h, c: (h, 0, pos(c)))]

    out_shape = jax.ShapeDtypeStruct((steps * L, nh * dh), BF16)
    return pl.pallas_call(
        _mlstm_kernel,
        grid=(ng, steps),
        in_specs=specs(fwd) + specs(bwd),
        out_specs=[pl.BlockSpec((L, hp * dh), lambda h, c: (fwd(c), h)),
                   pl.BlockSpec((L, hp * dh), lambda h, c: (bwd(c), h))],
        out_shape=[out_shape, out_shape],
        scratch_shapes=[pltpu.VMEM((2 * hp, dh, dh), F32), pltpu.VMEM((2 * hp, 1, dh), F32),
                        pltpu.VMEM((2 * hp, 1, 1), F32)],
        compiler_params=_cparams(("parallel", "arbitrary")),
        name="mlstm_scan",
    )(qk, qk, u, g4, g4t, qk, qk, u, g4, g4t)


def kernel(x, c, ctx, c_ctx, w_mod_0, b_mod_0, g_pre_mix_0, g_post_mix_0, w_in_0, lambda_q1_0, lambda_k1_0, lambda_q2_0, lambda_k2_0, g_subln_0, g_q_0, g_k_0, w_out_0, g_pre_ffn_0, g_post_ffn_0, w_router_0, w_gate_0, w_up_0, w_down_0, w_mod_1, b_mod_1, g_pre_mix_1, g_post_mix_1, w_in_1, b_gate_1, w_conv_1, b_conv_1, g_head_1, w_out_1, g_pre_ffn_1, g_post_ffn_1, w_router_1, w_gate_1, w_up_1, w_down_1):
    b, n_lat, d = x.shape
    n_ctx = ctx.shape[1]
    assert b == 1 and n_lat % ROW_TILE == 0 and n_ctx % MLSTM_CHUNK == 0 and n_ctx <= ROW_TILE
    s_valid = n_lat + n_ctx
    s_pad = n_lat + ROW_TILE
    n_lat_tiles = n_lat // ROW_TILE

    ctx_pad = jnp.concatenate([ctx[0], jnp.zeros((s_pad - s_valid, d), F32)], axis=0)
    cc = jnp.concatenate([c, c_ctx[None, :], jnp.zeros((6, d), F32)], axis=0)

    mod0 = _modulation(cc, w_mod_0, b_mod_0)
    tabs = _rope_tables(n_lat, n_ctx, s_pad)
    p0 = _proj0(x[0], ctx_pad, mod0, g_pre_mix_0, w_in_0.astype(BF16), tabs, g_q_0, g_k_0, n_lat_tiles)
    lam_init = 0.8 - 0.6 * math.exp(-0.3 * 0)
    lam4 = jnp.stack([lambda_q1_0, lambda_k1_0, lambda_q2_0, lambda_k2_0])
    o_a = _attn_diff(p0, lam4, g_subln_0, n_lat_tiles, s_valid, lam_init)
    o_b = _attn_gqa(p0, n_lat, s_valid)
    ka = o_a.shape[1]
    x_mid, hl, aff = _mixer_out(
        _out0_kernel, (o_a, o_b, ctx_pad),
        lambda tm: [pl.BlockSpec((tm, ka), lambda i: (i, 0)), pl.BlockSpec((tm, o_b.shape[1]), lambda i: (i, 0)),
                    pl.BlockSpec((tm, d), lambda i: (jnp.maximum(i - n_lat // tm, 0), 0))],
        x[0], mod0, w_out_0.astype(BF16), g_post_mix_0, g_pre_ffn_0, w_router_0, s_pad, n_lat // 256,
        lambda tm: [], "attn_out")
    p_moe0 = dict(w_gate=w_gate_0, w_up=w_up_0, w_down=w_down_0, g_post_ffn=g_post_ffn_0)
    xs = _moe(hl, aff, x_mid, mod0, p_moe0, n_lat, n_ctx, s_pad, True)

    mod1 = _modulation(cc, w_mod_1, b_mod_1)
    inner = MLSTM_HEADS * MLSTM_DIM
    u, gates = _proj1(xs, mod1, g_pre_mix_1, w_in_1.astype(BF16), 4 * inner, w_in_1[:, 4 * inner:], b_gate_1,
                      n_lat_tiles)
    qk = _conv_silu(u, w_conv_1, b_conv_1, 2 * inner, n_lat, n_ctx)
    hf, hb = _mlstm_scan(qk, u, gates, n_lat, n_ctx)
    og_blk = 3 * inner // inner
    x_mid, hl, aff = _mixer_out(
        _out1_kernel, (hf, hb, u, g_head_1.reshape(1, inner)),
        lambda tm: [pl.BlockSpec((tm, inner), lambda i: (i, 0)), pl.BlockSpec((tm, inner), lambda i: (i, 0)),
                    pl.BlockSpec((tm, inner), lambda i: (i, og_blk)), pl.BlockSpec((1, inner), lambda i: (0, 0))],
        xs, mod1, w_out_1.astype(BF16), g_post_mix_1, g_pre_ffn_1, w_router_1, n_lat, n_lat // 256,
        lambda tm: [pltpu.VMEM((tm, inner), BF16)], "mlstm_out")
    p_moe1 = dict(w_gate=w_gate_1, w_up=w_up_1, w_down=w_down_1, g_post_ffn=g_post_ffn_1)
    out = _moe(hl, aff, x_mid, mod1, p_moe1, n_lat, n_ctx, s_pad, False)
    return out[None]
```

```python
import functools
import math

import jax
import jax.numpy as jnp
from jax import lax
from jax.experimental import pallas as pl
from jax.experimental.pallas import tpu as pltpu

F32 = jnp.float32
BF16 = jnp.bfloat16
HIGHEST = lax.Precision.HIGHEST

GRID_W = 64
ROPE_THETA = 10000.0
RMS_EPS = 1e-6
N_MOD = 6
DIFF_HEADS = 8
DIFF_DIM = 64
GQA_HEADS = 8
GQA_KV_HEADS = 2
GQA_DIM = 128
MLSTM_HEADS = 8
MLSTM_DIM = 256
MLSTM_CHUNK = 128
N_EXPERTS = 16
EC_CAPACITY_FACTOR = 2

LANES = 128
ROW_TILE = 512
MLSTM_HEADS_PER_STEP = 4
FFN_MAX_TILE_ROWS = 1024
FFN_STEPS = 8
ATTN_TK = 512
ATTN_ROW_GROUP = 256
LOG2_E = math.log2(math.e)
TOPK_BLOCKS = 128
VMEM_LIMIT = 56 * 1024 * 1024


def _cparams(sem):
    return pltpu.CompilerParams(dimension_semantics=sem, vmem_limit_bytes=VMEM_LIMIT)


def _rms(x, g):
    return x * lax.rsqrt(jnp.mean(x * x, axis=-1, keepdims=True) + RMS_EPS) * g


def _dot(a, b):
    return jnp.dot(a, b, preferred_element_type=F32)


def _dot_nt(a, b, precision=None):
    return lax.dot_general(a, b, (((1,), (1,)), ((), ())), precision=precision, preferred_element_type=F32)


def _mod_kernel(c_ref, w_ref, b_ref, o_ref):
    c = c_ref[...]
    s = c * jax.nn.sigmoid(c)
    o_ref[...] = lax.dot_general(s, w_ref[...], (((1,), (0,)), ((), ())), precision=HIGHEST,
                                 preferred_element_type=F32) + b_ref[...]


def _modulation(cc, w_mod, b_mod):
    d, n = w_mod.shape
    tn = 1024
    return pl.pallas_call(
        _mod_kernel,
        grid=(n // tn,),
        in_specs=[pl.BlockSpec((8, d), lambda j: (0, 0)),
                  pl.BlockSpec((d, tn), lambda j: (0, j)),
                  pl.BlockSpec((1, tn), lambda j: (0, j))],
        out_specs=pl.BlockSpec((8, tn), lambda j: (0, j)),
        out_shape=jax.ShapeDtypeStruct((8, n), F32),
        compiler_params=_cparams(("arbitrary",)),
        name="modulation",
    )(cc, w_mod, b_mod.reshape(1, n))


def _mod_rows(mod_ref, r, k, d):
    return mod_ref[pl.ds(r, 1), k * d:(k + 1) * d]


def _premix(x, mod_ref, g_ref, r, d):
    return _rms(x, g_ref[...]) * (1.0 + _mod_rows(mod_ref, r, 1, d)) + _mod_rows(mod_ref, r, 0, d)


def _dot_hilo(a, b):
    a_hi = a.astype(BF16)
    a_lo = (a - a_hi.astype(F32)).astype(BF16)
    b_hi = b.astype(BF16)
    b_lo = (b - b_hi.astype(F32)).astype(BF16)
    return _dot(a_hi, b_hi) + (_dot(a_hi, b_lo) + _dot(a_lo, b_hi))


def _rope_a(x, ca, sa, sb):
    return x * ca + pltpu.roll(x, LANES - DIFF_DIM // 2, 1) * sa + pltpu.roll(x, DIFF_DIM // 2, 1) * sb


def _rope_b(x, cb, sb):
    return x * cb + pltpu.roll(x, GQA_DIM // 2, 1) * sb


def _proj0_kernel(x_ref, c_ref, mod_ref, gpre_ref, w_ref, ca_ref, saa_ref, sab_ref, cb_ref, sbb_ref, gq_ref, gk_ref,
                  o_ref, *, n_lat_tiles, d, tn):
    is_ctx = pl.program_id(0) >= n_lat_tiles
    x = jnp.where(is_ctx, c_ref[...], x_ref[...])
    h = _premix(x, mod_ref, gpre_ref, jnp.where(is_ctx, 1, 0), d).astype(BF16)
    n_qa = DIFF_HEADS
    n_qb = GQA_HEADS * GQA_DIM // LANES
    n_kb = GQA_KV_HEADS * GQA_DIM // LANES

    def epilogue(a, gi):
        if gi < 2 * n_qa:
            a = _rope_a(a, ca_ref[...], saa_ref[...], sab_ref[...])
            return a * (DIFF_DIM ** -0.5 * LOG2_E) if gi < n_qa else a
        if gi < 3 * n_qa:
            return a
        if gi < 3 * n_qa + n_qb:
            return _rope_b(_rms(a, gq_ref[...]), cb_ref[...], sbb_ref[...]) * (GQA_DIM ** -0.5 * LOG2_E)
        if gi < 3 * n_qa + n_qb + n_kb:
            return _rope_b(_rms(a, gk_ref[...]), cb_ref[...], sbb_ref[...])
        return a

    for j in range(w_ref.shape[1] // tn):
        acc = _dot(h, w_ref[:, j * tn:(j + 1) * tn])
        for g in range(tn // LANES):
            gi = j * (tn // LANES) + g
            o_ref[:, gi * LANES:(gi + 1) * LANES] = epilogue(acc[:, g * LANES:(g + 1) * LANES], gi).astype(BF16)


def _proj0(x, ctx_pad, mod, g_pre, w_in, tabs, g_q, g_k, n_lat_tiles):
    d = x.shape[1]
    s_pad = x.shape[0] + ctx_pad.shape[0]
    n_out = w_in.shape[1]
    tm, tn = ROW_TILE, 512
    row = lambda i: (i, 0)
    const = lambda i: (0, 0)
    once = pl.Buffered(1)
    tab_spec = pl.BlockSpec((tm, LANES), row)
    return pl.pallas_call(
        functools.partial(_proj0_kernel, n_lat_tiles=n_lat_tiles, d=d, tn=tn),
        grid=(s_pad // tm,),
        in_specs=[pl.BlockSpec((tm, d), lambda i: (jnp.minimum(i, n_lat_tiles - 1), 0)),
                  pl.BlockSpec((tm, d), const, pipeline_mode=once),
                  pl.BlockSpec((8, N_MOD * d), const, pipeline_mode=once),
                  pl.BlockSpec((1, d), const),
                  pl.BlockSpec((d, n_out), const, pipeline_mode=once),
                  tab_spec, tab_spec, tab_spec, tab_spec, tab_spec,
                  pl.BlockSpec((1, LANES), const),
                  pl.BlockSpec((1, LANES), const)],
        out_specs=pl.BlockSpec((tm, n_out), row),
        out_shape=jax.ShapeDtypeStruct((s_pad, n_out), BF16),
        compiler_params=_cparams(("parallel",)),
        name="proj0",
    )(x, ctx_pad, mod, g_pre.reshape(1, d), w_in, *tabs, g_q.reshape(1, LANES), g_k.reshape(1, LANES))


def _rope_tables(n_lat, n_ctx, s_pad):
    rows = jnp.repeat(jnp.arange(n_lat // GRID_W, dtype=F32), GRID_W)
    cols = jnp.tile(jnp.arange(GRID_W, dtype=F32), n_lat // GRID_W)

    def angles(dim):
        axis_dim = dim // 2
        inv_freq = ROPE_THETA ** (-jnp.arange(0, axis_dim, 2, dtype=F32) / axis_dim)
        ang = jnp.concatenate([rows[:, None] * inv_freq, cols[:, None] * inv_freq], axis=-1)
        return jnp.cos(ang), jnp.sin(ang)

    def pad(t, fill):
        return jnp.concatenate([t, jnp.full((s_pad - n_lat, LANES), fill, F32)], axis=0)

    cos_a, sin_a = angles(DIFF_DIM)
    zero = jnp.zeros_like(sin_a)
    ca = jnp.tile(cos_a, (1, 4))
    saa = jnp.tile(jnp.concatenate([-sin_a, zero], axis=1), (1, 2))
    sab = jnp.tile(jnp.concatenate([zero, sin_a], axis=1), (1, 2))
    cos_b, sin_b = angles(GQA_DIM)
    cb = jnp.tile(cos_b, (1, 2))
    sbb = jnp.concatenate([-sin_b, sin_b], axis=1)
    return pad(ca, 1.0), pad(saa, 0.0), pad(sab, 0.0), pad(cb, 1.0), pad(sbb, 0.0)


def _flash_rows(q, k_ref, v_ref, m_scr, acc_scr, s_scr, is_ctx_q, tk, s_valid):
    n_chunks = k_ref.shape[0] // tk
    last = n_chunks - 1
    groups = [slice(g * ATTN_ROW_GROUP, (g + 1) * ATTN_ROW_GROUP) for g in range(q.shape[0] // ATTN_ROW_GROUP)]
    m_scr[...] = jnp.full(m_scr.shape, -jnp.inf, F32)
    acc_scr[...] = jnp.zeros(acc_scr.shape, F32)

    def rows_of(c):
        return pl.ds(c * tk if isinstance(c, int) else pl.multiple_of(c * tk, tk), tk)

    def scores(c, buf, rows):
        s_scr[buf, rows] = _dot_nt(q[rows], k_ref[rows_of(c), :])

    def consume(c, buf, rows, v1, masked):
        s = s_scr[buf, rows]
        if masked:
            kpos = c * tk + lax.broadcasted_iota(jnp.int32, (1, tk), 1)
            s = jnp.where(kpos < s_valid, s, -jnp.inf)
        m_prev = m_scr[rows]
        m_new = jnp.maximum(m_prev, jnp.max(s, axis=1, keepdims=True))
        p = jnp.exp2(s - m_new).astype(BF16)
        acc_scr[rows] = jnp.exp2(m_prev - m_new) * acc_scr[rows] + _dot(p, v1)
        m_scr[rows] = m_new

    def values(c):
        vc = v_ref[rows_of(c), :]
        return jnp.concatenate([vc, jnp.ones_like(vc)], axis=1)

    unroll = max(u for u in (2, 4, 8) if last % u == 0)
    lo = jnp.where(is_ctx_q, last, 0)
    for rows in groups:
        scores(lo, 0, rows)

    def body(i, carry):
        c = lo + unroll * i
        for j in range(unroll):
            v1 = values(c + j)
            for rows in groups:
                scores(c + j + 1, (j + 1) % 2, rows)
                consume(c + j, j % 2, rows, v1, False)
        return carry

    lax.fori_loop(0, (last - lo) // unroll, body, 0)
    v1 = values(last)
    for rows in groups:
        consume(last, 0, rows, v1, True)


def _attn_diff_kernel(q_ref, k_ref, v_ref, lam_ref, gsub_ref, o_ref, m_scr, acc_scr, s_scr,
                      *, n_lat_tiles, tk, s_valid, lam_init):
    q = q_ref[...]
    tq, hd = q.shape
    first = lax.broadcasted_iota(jnp.int32, (1, hd), 1) < DIFF_DIM
    zero = jnp.zeros_like(q)
    qq = jnp.concatenate([jnp.where(first, q, zero), jnp.where(first, zero, q)], axis=0)
    _flash_rows(qq, k_ref, v_ref, m_scr, acc_scr, s_scr, pl.program_id(1) >= n_lat_tiles, tk, s_valid)
    lam = (jnp.exp(jnp.sum(lam_ref[0:1, :] * lam_ref[1:2, :], axis=1, keepdims=True))
           - jnp.exp(jnp.sum(lam_ref[2:3, :] * lam_ref[3:4, :], axis=1, keepdims=True)) + lam_init)
    o = (acc_scr[0:tq, 0:hd] / acc_scr[0:tq, hd:hd + 1]
         - lam * (acc_scr[tq:2 * tq, 0:hd] / acc_scr[tq:2 * tq, hd:hd + 1]))
    o_ref[...] = (_rms(o, gsub_ref[...]) * (1.0 - lam_init)).astype(BF16)


def _attn_diff(p0, lam4, g_subln, n_lat_tiles, s_valid, lam_init):
    s_pad = p0.shape[0]
    tq = ROW_TILE
    hd = 2 * DIFF_DIM
    k0 = DIFF_HEADS
    v0 = 2 * DIFF_HEADS
    return pl.pallas_call(
        functools.partial(_attn_diff_kernel, n_lat_tiles=n_lat_tiles, tk=ATTN_TK, s_valid=s_valid, lam_init=lam_init),
        grid=(DIFF_HEADS, s_pad // tq),
        in_specs=[pl.BlockSpec((tq, hd), lambda h, iq: (iq, h)),
                  pl.BlockSpec((s_pad, hd), lambda h, iq: (0, k0 + h)),
                  pl.BlockSpec((s_pad, hd), lambda h, iq: (0, v0 + h)),
                  pl.BlockSpec((4, DIFF_DIM), lambda h, iq: (0, 0)),
                  pl.BlockSpec((1, hd), lambda h, iq: (0, 0))],
        out_specs=pl.BlockSpec((tq, hd), lambda h, iq: (iq, h)),
        out_shape=jax.ShapeDtypeStruct((s_pad, DIFF_HEADS * hd), BF16),
        scratch_shapes=[pltpu.VMEM((2 * tq, 1), F32), pltpu.VMEM((2 * tq, 2 * hd), F32),
                        pltpu.VMEM((2, 2 * tq, ATTN_TK), F32)],
        compiler_params=_cparams(("parallel", "arbitrary")),
        name="attn_diff",
    )(p0, p0, p0, lam4, g_subln.reshape(1, hd))


def _attn_gqa_kernel(q_ref, k_ref, v_ref, o_ref, m_scr, acc_scr, s_scr, *, n_lat_tiles, tk, s_valid, group):
    tq = q_ref.shape[0]
    dh = GQA_DIM
    qq = jnp.concatenate([q_ref[:, g * dh:(g + 1) * dh] for g in range(group)], axis=0)
    _flash_rows(qq, k_ref, v_ref, m_scr, acc_scr, s_scr, pl.program_id(1) >= n_lat_tiles, tk, s_valid)
    for g in range(group):
        rows = slice(g * tq, (g + 1) * tq)
        o_ref[:, g * dh:(g + 1) * dh] = (acc_scr[rows, 0:dh] / acc_scr[rows, dh:dh + 1]).astype(BF16)


def _attn_gqa(p0, n_lat, s_valid):
    s_pad = p0.shape[0]
    tq = 256
    group = GQA_HEADS // GQA_KV_HEADS
    qw = group * GQA_DIM
    q0 = 3 * DIFF_HEADS * 2 * DIFF_DIM // qw
    k0 = (3 * DIFF_HEADS * 2 * DIFF_DIM + GQA_HEADS * GQA_DIM) // GQA_DIM
    v0 = k0 + GQA_KV_HEADS
    return pl.pallas_call(
        functools.partial(_attn_gqa_kernel, n_lat_tiles=n_lat // tq, tk=ATTN_TK, s_valid=s_valid, group=group),
        grid=(GQA_KV_HEADS, s_pad // tq),
        in_specs=[pl.BlockSpec((tq, qw), lambda h, iq: (iq, q0 + h)),
                  pl.BlockSpec((s_pad, GQA_DIM), lambda h, iq: (0, k0 + h)),
                  pl.BlockSpec((s_pad, GQA_DIM), lambda h, iq: (0, v0 + h))],
        out_specs=pl.BlockSpec((tq, qw), lambda h, iq: (iq, h)),
        out_shape=jax.ShapeDtypeStruct((s_pad, GQA_HEADS * GQA_DIM), BF16),
        scratch_shapes=[pltpu.VMEM((group * tq, 1), F32), pltpu.VMEM((group * tq, 2 * GQA_DIM), F32),
                        pltpu.VMEM((2, group * tq, ATTN_TK), F32)],
        compiler_params=_cparams(("parallel", "arbitrary")),
        name="attn_gqa",
    )(p0, p0, p0)


def _post_mix(y, x, mod_ref, r, gpost_ref, gffn_ref, wr_ref, xo_ref, hl_ref, aff_ref, d):
    xn = x + _mod_rows(mod_ref, r, 2, d) * _rms(y, gpost_ref[...])
    xo_ref[...] = xn
    hl = _rms(xn, gffn_ref[...]) * (1.0 + _mod_rows(mod_ref, r, 4, d)) + _mod_rows(mod_ref, r, 3, d)
    hl_ref[...] = hl
    logits = _dot_hilo(hl, wr_ref[...])
    e = jnp.exp(logits - jnp.max(logits, axis=1, keepdims=True))
    aff_ref[...] = e / jnp.sum(e, axis=1, keepdims=True)


def _out0_kernel(oa_ref, ob_ref, c_ref, x_ref, mod_ref, w_ref, gpost_ref, gffn_ref, wr_ref, xo_ref, hl_ref, aff_ref,
                 *, n_lat_tiles, d):
    is_ctx = pl.program_id(0) >= n_lat_tiles
    r = jnp.where(is_ctx, 1, 0)
    ka = oa_ref.shape[1]
    y = _dot(oa_ref[...], w_ref[0:ka, :]) + _dot(ob_ref[...], w_ref[ka:, :])
    x = jnp.where(is_ctx, c_ref[...], x_ref[...])
    _post_mix(y, x, mod_ref, r, gpost_ref, gffn_ref, wr_ref, xo_ref, hl_ref, aff_ref, d)


def _out1_kernel(hf_ref, hb_ref, og_ref, ghead_ref, x_ref, mod_ref, w_ref, gpost_ref, gffn_ref, wr_ref,
                 xo_ref, hl_ref, aff_ref, y_scr, *, n_lat_tiles, d):
    r = jnp.where(pl.program_id(0) >= n_lat_tiles, 1, 0)
    for h in range(MLSTM_HEADS):
        sl = slice(h * MLSTM_DIM, (h + 1) * MLSTM_DIM)
        hs = hf_ref[:, sl].astype(F32) + hb_ref[:, sl].astype(F32)
        hn = _rms(hs, ghead_ref[:, sl])
        y_scr[:, sl] = (jax.nn.sigmoid(og_ref[:, sl].astype(F32)) * hn).astype(BF16)
    y = _dot(y_scr[...], w_ref[...])
    _post_mix(y, x_ref[...], mod_ref, r, gpost_ref, gffn_ref, wr_ref, xo_ref, hl_ref, aff_ref, d)


def _mixer_out(kernel_fn, mix_inputs, mix_specs, x, mod, w_out, g_post, g_ffn, w_router, n_rows, n_lat_tiles_tm,
               scratch, name):
    d = x.shape[1]
    tm = 256
    row = lambda i: (i, 0)
    x_row = lambda i: (jnp.minimum(i, x.shape[0] // tm - 1), 0)
    const = lambda i: (0, 0)
    ne = w_router.shape[1]
    return pl.pallas_call(
        functools.partial(kernel_fn, n_lat_tiles=n_lat_tiles_tm, d=d),
        grid=(n_rows // tm,),
        in_specs=mix_specs(tm) + [pl.BlockSpec((tm, d), x_row),
                                  pl.BlockSpec((8, N_MOD * d), const),
                                  pl.BlockSpec(w_out.shape, const),
                                  pl.BlockSpec((1, d), const),
                                  pl.BlockSpec((1, d), const),
                                  pl.BlockSpec((d, ne), const)],
        out_specs=[pl.BlockSpec((tm, d), row), pl.BlockSpec((tm, d), row), pl.BlockSpec((tm, ne), row)],
        out_shape=[jax.ShapeDtypeStruct((n_rows, d), F32), jax.ShapeDtypeStruct((n_rows, d), F32),
                   jax.ShapeDtypeStruct((n_rows, ne), F32)],
        scratch_shapes=scratch(tm),
        compiler_params=_cparams(("parallel",)),
        name=name,
    )(*mix_inputs, x, mod, w_out, g_post.reshape(1, d), g_ffn.reshape(1, d), w_router)


def _incl_cumsum(maskf, upper, strict_lower):
    mb = maskf.astype(BF16)
    local = _dot(mb, upper)
    start = jnp.sum(_dot(strict_lower, mb), axis=1, keepdims=True)
    return local, start


def _topk_kernel(aff_ref, idx_ref, gate_ref, dest_ref, cnt_ref, cnt_acc, *, cap, dest_stride, row_off):
    e = pl.program_id(0)
    ne = pl.num_programs(0)
    nb = aff_ref.shape[1]
    a = aff_ref[0]
    bits = pltpu.bitcast(a, jnp.int32)

    thr = jnp.zeros((1, 1), jnp.int32)
    for bit in range(30, -1, -1):
        cand = thr | (1 << bit)
        c = jnp.sum(jnp.sum(jnp.where(bits >= cand, 1.0, 0.0), axis=1, keepdims=True), axis=0, keepdims=True)
        thr = jnp.where(c >= cap, cand, thr)

    r_i = lax.broadcasted_iota(jnp.int32, (LANES, LANES), 0)
    c_i = lax.broadcasted_iota(jnp.int32, (LANES, LANES), 1)
    upper = jnp.where(r_i <= c_i, 1.0, 0.0).astype(BF16)
    rb = lax.broadcasted_iota(jnp.int32, (nb, nb), 0)
    cb = lax.broadcasted_iota(jnp.int32, (nb, nb), 1)
    strict_lower = jnp.where(cb < rb, 1.0, 0.0).astype(BF16)

    gt = bits > thr
    eq = bits == thr
    n_gt = jnp.sum(jnp.sum(jnp.where(gt, 1.0, 0.0), axis=1, keepdims=True), axis=0, keepdims=True)
    eq_local, eq_start = _incl_cumsum(jnp.where(eq, 1.0, 0.0), upper, strict_lower)
    sel = jnp.logical_or(gt, jnp.logical_and(eq, eq_local + eq_start <= cap - n_gt))
    self_ = jnp.where(sel, 1.0, 0.0)
    sel_local, sel_start = _incl_cumsum(self_, upper, strict_lower)
    sel_tot = jnp.sum(self_, axis=1, keepdims=True)

    @pl.when(e == 0)
    def _():
        cnt_acc[...] = jnp.zeros(cnt_acc.shape, F32)

    earlier = cnt_acc[...]
    cnt_acc[...] = earlier + self_

    lane = lax.broadcasted_iota(jnp.int32, (nb, LANES), 1)
    cols = jnp.where(lane == 0, sel_start, jnp.where(lane == 1, sel_tot, 0.0))
    rows = cols.T
    start_row = rows[0:1, 0:nb]
    tot_row = rows[1:2, 0:nb]

    p = lax.broadcasted_iota(jnp.int32, (cap, 1), 0).astype(F32)
    onehot = jnp.where(jnp.logical_and(start_row <= p, p < start_row + tot_row), 1.0, 0.0)
    blk = lax.broadcasted_iota(jnp.int32, (1, nb), 1).astype(F32)
    p_start = jnp.sum(onehot * start_row, axis=1, keepdims=True)
    p_blk = jnp.sum(onehot * blk, axis=1, keepdims=True)
    ohb = onehot.astype(BF16)
    local_rows = _dot(ohb, sel_local.astype(BF16))
    sel_rows = _dot(ohb, self_.astype(BF16))
    earlier_rows = _dot(ohb, earlier.astype(BF16))
    aff_rows = lax.dot_general(onehot, a, (((1,), (0,)), ((), ())), precision=HIGHEST, preferred_element_type=F32)
    hit = jnp.where(jnp.logical_and(local_rows == p - p_start + 1.0, sel_rows > 0.5), 1.0, 0.0)
    lane_f = lax.broadcasted_iota(jnp.int32, (1, LANES), 1).astype(F32)
    tok = p_blk * LANES + jnp.sum(hit * lane_f, axis=1, keepdims=True)
    gate = jnp.sum(hit * aff_rows, axis=1, keepdims=True)
    rank = jnp.sum(hit * earlier_rows, axis=1, keepdims=True)
    idx_ref[0] = tok.astype(jnp.int32) + row_off
    gate_ref[0] = gate
    dest_ref[0] = (rank * dest_stride + tok).astype(jnp.int32) + row_off

    @pl.when(e == ne - 1)
    def _():
        cnt_ref[...] = cnt_acc[...].astype(jnp.int32)


def _route(aff_t, cap, dest_stride, row_off):
    ne, n = aff_t.shape
    n_pad = TOPK_BLOCKS * LANES
    assert n <= n_pad
    a = jnp.pad(aff_t, ((0, 0), (0, n_pad - n)), constant_values=-1.0).reshape(ne, TOPK_BLOCKS, LANES)
    slot = pl.BlockSpec((1, cap, 1), lambda e: (e, 0, 0))
    idx, gate, dest, cnt = pl.pallas_call(
        functools.partial(_topk_kernel, cap=cap, dest_stride=dest_stride, row_off=row_off),
        grid=(ne,),
        in_specs=[pl.BlockSpec((1, TOPK_BLOCKS, LANES), lambda e: (e, 0, 0))],
        out_specs=[slot, slot, slot, pl.BlockSpec((TOPK_BLOCKS, LANES), lambda e: (0, 0))],
        out_shape=[jax.ShapeDtypeStruct((ne, cap, 1), jnp.int32), jax.ShapeDtypeStruct((ne, cap, 1), F32),
                   jax.ShapeDtypeStruct((ne, cap, 1), jnp.int32), jax.ShapeDtypeStruct((TOPK_BLOCKS, LANES), jnp.int32)],
        scratch_shapes=[pltpu.VMEM((TOPK_BLOCKS, LANES), F32)],
        compiler_params=_cparams(("arbitrary",)),
        name="route_topk",
    )(a)
    return idx.reshape(ne * cap), gate, dest.reshape(ne * cap), cnt.reshape(n_pad)[:n]


def _pack_bf16_pairs(y):
    n = y.shape[1] // 2
    bits = lax.bitcast_convert_type(y.astype(BF16).astype(F32), jnp.uint32)
    return (bits[:, n:] & jnp.uint32(0xFFFF0000)) | (bits[:, :n] >> 16)


def _unpack_bf16_pairs(w):
    return (lax.bitcast_convert_type(w << 16, F32), lax.bitcast_convert_type(w & jnp.uint32(0xFFFF0000), F32))


def _row_copy(src, s, dst, t, sem):
    return pltpu.make_async_copy(src.at[pl.ds(s, 1)], dst.at[pl.ds(t, 1)], sem)


def _ffn_body(idx_sm, dest_sm, hl_hbm, gate_ref, wg_ref, wu_ref, wd_ref, z_hbm, xbuf, xb, acc, zp, gsem, ssem, *, tm):
    f = pl.program_id(2)
    nm = pl.num_programs(1)
    t = pl.program_id(0) * nm + pl.program_id(1)
    n_tiles = pl.num_programs(0) * nm
    cur = t % 2
    nxt = 1 - cur
    chunk = tm // FFN_STEPS
    last_step = f == pl.num_programs(2) - 1
    t_next = jnp.minimum(t + 1, n_tiles - 1)
    t_prev = jnp.maximum(t - 1, 0)

    def gather(tile, slot, r):
        return _row_copy(hl_hbm, idx_sm[tile * tm + r], xbuf.at[slot], r, gsem.at[slot])

    def scatter(tile, slot, r):
        return _row_copy(zp.at[slot], r, z_hbm, dest_sm[tile * tm + r], ssem.at[slot])

    def for_rows(fn):
        def body(r, c):
            fn(r)
            return c
        lax.fori_loop(0, tm, body, 0, unroll=8)

    @pl.when(jnp.logical_and(t == 0, f == 0))
    def _():
        for_rows(lambda r: gather(0, 0, r).start())
        zp[1] = jnp.zeros(zp.shape[1:], jnp.uint32)

    @pl.when(f == 0)
    def _():
        for_rows(lambda r: gather(t, cur, r).wait())
        xb[...] = xbuf[cur].astype(BF16)
        acc[...] = jnp.zeros(acc.shape, F32)

        @pl.when(t >= 1)
        def _():
            for_rows(lambda r: scatter(t_prev, cur, r).wait())

    for r in range(chunk):
        gather(t_next, nxt, f * chunk + r).start()
        scatter(t_prev, nxt, f * chunk + r).start()

    x = xb[...]
    g = _dot(x, wg_ref[0].astype(BF16))
    u = _dot(x, wu_ref[0].astype(BF16))
    hid = (g * jax.nn.sigmoid(g) * u).astype(BF16)
    acc[...] += _dot(hid, wd_ref[0].astype(BF16))

    @pl.when(last_step)
    def _():
        zp[cur] = _pack_bf16_pairs(acc[...] * gate_ref[0])

    @pl.when(jnp.logical_and(t == n_tiles - 1, last_step))
    def _():
        for_rows(lambda r: gather(t, nxt, r).wait())
        for_rows(lambda r: scatter(t_prev, nxt, r).wait())
        for_rows(lambda r: scatter(t, cur, r).start())
        for_rows(lambda r: scatter(t, cur, r).wait())


def _expert_ffn(idx, dest, hl, gate, w_gate, w_up, w_down, z_rows, n_tiles):
    ne, d, ff = w_gate.shape
    cap = gate.shape[1]
    tm = cap // n_tiles
    assert tm * n_tiles == cap and tm % FFN_STEPS == 0 and tm % 16 == 0
    tf = ff // FFN_STEPS
    any_spec = pl.BlockSpec(memory_space=pl.ANY)
    return pl.pallas_call(
        functools.partial(_ffn_body, tm=tm),
        grid_spec=pltpu.PrefetchScalarGridSpec(
            num_scalar_prefetch=2,
            grid=(ne, n_tiles, FFN_STEPS),
            in_specs=[any_spec,
                      pl.BlockSpec((1, tm, 1), lambda e, m, f, *_: (e, m, 0)),
                      pl.BlockSpec((1, d, tf), lambda e, m, f, *_: (e, 0, f)),
                      pl.BlockSpec((1, d, tf), lambda e, m, f, *_: (e, 0, f)),
                      pl.BlockSpec((1, tf, d), lambda e, m, f, *_: (e, f, 0))],
            out_specs=any_spec,
            scratch_shapes=[pltpu.VMEM((2, tm, d), F32), pltpu.VMEM((tm, d), BF16), pltpu.VMEM((tm, d), F32),
                            pltpu.VMEM((2, tm, d // 2), jnp.uint32),
                            pltpu.SemaphoreType.DMA((2,)), pltpu.SemaphoreType.DMA((2,))]),
        out_shape=jax.ShapeDtypeStruct((z_rows, d // 2), jnp.uint32),
        compiler_params=_cparams(("arbitrary", "arbitrary", "arbitrary")),
        name="expert_ffn",
    )(idx, dest, hl, gate, w_gate, w_up, w_down)


def _combine_kernel(maxc_sm, z_ref, cnt_ref, x_ref, mod_ref, g_ref, o_ref, acc, *, n_lat_tiles, d):
    i = pl.program_id(0)
    k = pl.program_id(1)
    nk = pl.num_programs(1)

    @pl.when(k == 0)
    def _():
        acc[...] = jnp.zeros(acc.shape, F32)

    @pl.when(k < maxc_sm[i])
    def _():
        lo, hi = _unpack_bf16_pairs(z_ref[...])
        live = cnt_ref[...] > k
        acc[:, 0:d // 2] += jnp.where(live, lo, 0.0)
        acc[:, d // 2:d] += jnp.where(live, hi, 0.0)

    @pl.when(k == nk - 1)
    def _():
        r = jnp.where(i >= n_lat_tiles, 1, 0)
        o_ref[...] = x_ref[...] + _mod_rows(mod_ref, r, 5, d) * _rms(acc[...], g_ref[...])


def _combine(z, cnt, x, mod, g_post, n_rows, z_stride, n_lat_tiles_tb):
    d = x.shape[1]
    tb = ROW_TILE
    nb = n_rows // tb
    zb = z_stride // tb
    maxc = jnp.max(cnt[:n_rows].reshape(nb, tb), axis=1)

    def z_map(i, k, maxc_sm):
        return (jnp.maximum(jnp.minimum(k, maxc_sm[i] - 1), 0) * zb + i, 0)

    return pl.pallas_call(
        functools.partial(_combine_kernel, n_lat_tiles=n_lat_tiles_tb, d=d),
        grid_spec=pltpu.PrefetchScalarGridSpec(
            num_scalar_prefetch=1,
            grid=(nb, N_EXPERTS),
            in_specs=[pl.BlockSpec((tb, d // 2), z_map),
                      pl.BlockSpec((tb, 1), lambda i, k, s: (i, 0)),
                      pl.BlockSpec((tb, d), lambda i, k, s: (i, 0)),
                      pl.BlockSpec((8, N_MOD * d), lambda i, k, s: (0, 0)),
                      pl.BlockSpec((1, d), lambda i, k, s: (0, 0))],
            out_specs=pl.BlockSpec((tb, d), lambda i, k, s: (i, 0)),
            scratch_shapes=[pltpu.VMEM((tb, d), F32)]),
        out_shape=jax.ShapeDtypeStruct((n_rows, d), F32),
        compiler_params=_cparams(("parallel", "arbitrary")),
        name="moe_combine",
    )(maxc, z, cnt.reshape(-1, 1), x, mod, g_post.reshape(1, d))


def _moe(hl, aff, x_mid, mod, p, n_lat, n_ctx, s_pad, with_ctx):
    aff_t = aff.T
    ne = N_EXPERTS
    cap_l = EC_CAPACITY_FACTOR * n_lat // ne
    n_tiles = max(1, cap_l // FFN_MAX_TILE_ROWS)
    idx, gate, dest, cnt = _route(aff_t[:, :n_lat], cap_l, s_pad, 0)
    n_rows = n_lat
    if with_ctx:
        cap_c = EC_CAPACITY_FACTOR * n_ctx // ne
        idx_c, gate_c, dest_c, cnt_c = _route(aff_t[:, n_lat:n_lat + n_ctx], cap_c, s_pad, n_lat)

        def per_tile(a, b):
            return jnp.concatenate([a.reshape(ne, n_tiles, cap_l // n_tiles), b.reshape(ne, n_tiles, cap_c // n_tiles)],
                                   axis=2).reshape(ne, cap_l + cap_c)
        idx = per_tile(idx, idx_c).reshape(-1)
        dest = per_tile(dest, dest_c).reshape(-1)
        gate = per_tile(gate, gate_c)[:, :, None]
        cnt = jnp.concatenate([cnt, cnt_c, jnp.zeros((s_pad - n_lat - n_ctx,), jnp.int32)])
        n_rows = s_pad
    z = _expert_ffn(idx, dest, hl, gate, p['w_gate'], p['w_up'], p['w_down'], ne * s_pad, n_tiles)
    return _combine(z, cnt, x_mid, mod, p['g_post_ffn'], n_rows, s_pad, n_lat // ROW_TILE)


def _proj1_kernel(x_ref, mod_ref, gpre_ref, w_ref, wg_ref, bg_ref, o_ref, g_ref, h_scr, *, n_lat_tiles, d):
    i = pl.program_id(0)
    j = pl.program_id(1)

    @pl.when(j == 0)
    def _():
        r = jnp.where(i >= n_lat_tiles, 1, 0)
        h = _premix(x_ref[...], mod_ref, gpre_ref, r, d)
        h_scr[...] = h.astype(BF16)
        g_ref[...] = _dot_hilo(h, wg_ref[...]) + bg_ref[...]

    o_ref[...] = _dot(h_scr[...], w_ref[...]).astype(BF16)


def _proj1(xs, mod, g_pre, w_main, n_out, w_gates, b_gate, n_lat_tiles):
    s_pad, d = xs.shape
    ng = w_gates.shape[1]
    tm, tn = ROW_TILE, 1024
    row = lambda i, j: (i, 0)
    const = lambda i, j: (0, 0)
    return pl.pallas_call(
        functools.partial(_proj1_kernel, n_lat_tiles=n_lat_tiles, d=d),
        grid=(s_pad // tm, n_out // tn),
        in_specs=[pl.BlockSpec((tm, d), row),
                  pl.BlockSpec((8, N_MOD * d), const),
                  pl.BlockSpec((1, d), const),
                  pl.BlockSpec((d, tn), lambda i, j: (0, j)),
                  pl.BlockSpec((d, ng), const),
                  pl.BlockSpec((1, ng), const)],
        out_specs=[pl.BlockSpec((tm, tn), lambda i, j: (i, j)), pl.BlockSpec((tm, ng), row)],
        out_shape=[jax.ShapeDtypeStruct((s_pad, n_out), BF16), jax.ShapeDtypeStruct((s_pad, ng), F32)],
        scratch_shapes=[pltpu.VMEM((tm, d), BF16)],
        compiler_params=_cparams(("parallel", "arbitrary")),
        name="proj1",
    )(xs, mod, g_pre.reshape(1, d), w_main, w_gates, b_gate.reshape(1, ng))


def _conv_kernel(cur_ref, prev_ref, next_ref, w_ref, b_ref, o_ref, *, n_lat, n_ctx, k_cols0, k_scale):
    i = pl.program_id(0)
    j = pl.program_id(1)
    tm, tc = cur_ref.shape
    hr = prev_ref.shape[0]
    cur = cur_ref[...].astype(F32)
    row = lax.broadcasted_iota(jnp.int32, (tm, 1), 0)
    g = i * tm + row
    prev_row = prev_ref[hr - 1:hr, :].astype(F32)
    next_row = next_ref[0:1, :].astype(F32)
    before = jnp.where(row == 0, prev_row, pltpu.roll(cur, 1, 0))
    after = jnp.where(row == tm - 1, next_row, pltpu.roll(cur, tm - 1, 0))
    before = jnp.where(jnp.logical_or(g == 0, g == n_lat), 0.0, before)
    after = jnp.where(jnp.logical_or(g == n_lat - 1, g == n_lat + n_ctx - 1), 0.0, after)
    y = before * w_ref[0:1, :] + cur * w_ref[1:2, :] + after * w_ref[2:3, :] + b_ref[...]
    y = y * jax.nn.sigmoid(y)
    o_ref[...] = (y * jnp.where(j * tc >= k_cols0, k_scale, 1.0)).astype(BF16)


def _conv_silu(u, w_conv, b_conv, n_cols, n_lat, n_ctx):
    s_pad = u.shape[0]
    tm, tc, hr = ROW_TILE, 1024, 16
    rb = tm // hr
    last = s_pad // hr - 1
    return pl.pallas_call(
        functools.partial(_conv_kernel, n_lat=n_lat, n_ctx=n_ctx, k_cols0=n_cols // 2, k_scale=MLSTM_DIM ** -0.5),
        grid=(s_pad // tm, n_cols // tc),
        in_specs=[pl.BlockSpec((tm, tc), lambda i, j: (i, j)),
                  pl.BlockSpec((hr, tc), lambda i, j: (jnp.maximum(i * rb - 1, 0), j)),
                  pl.BlockSpec((hr, tc), lambda i, j: (jnp.minimum((i + 1) * rb, last), j)),
                  pl.BlockSpec((8, tc), lambda i, j: (0, j)),
                  pl.BlockSpec((1, tc), lambda i, j: (0, j))],
        out_specs=pl.BlockSpec((tm, tc), lambda i, j: (i, j)),
        out_shape=jax.ShapeDtypeStruct((s_pad, n_cols), BF16),
        compiler_params=_cparams(("parallel", "parallel")),
        name="conv_silu",
    )(u, u, u, jnp.pad(w_conv, ((0, 8 - w_conv.shape[0]), (0, 0))), b_conv.reshape(1, n_cols))


def _log_sigmoid(x):
    return -(jnp.maximum(-x, 0.0) + jnp.log1p(jnp.exp(-jnp.abs(x))))


def _mlstm_chains(chains):
    def each(fn, *lists):
        return [fn(*a) for a in zip(*lists)]

    L = chains[0]['q'].shape[0]
    t_i = lax.broadcasted_iota(jnp.int32, (L, L), 0)
    s_i = lax.broadcasted_iota(jnp.int32, (L, L), 1)
    seen = [(s_i >= t_i) if ch['reverse'] else (s_i <= t_i) for ch in chains]
    q = [ch['q'] for ch in chains]
    k = [ch['k'] for ch in chains]
    v = [ch['v'] for ch in chains]
    ig_c = [ch['ig_c'] for ch in chains]
    ig_r = [ch['ig_r'] for ch in chains]
    lf_c = [_log_sigmoid(ch['fg_c']) for ch in chains]
    lf_r = [_log_sigmoid(ch['fg_r']) for ch in chains]
    m_prev = [ch['m'][...] for ch in chains]
    c_prev = [ch['c'][...] for ch in chains]
    n_prev = [ch['n'][...] for ch in chains]
    qk = each(_dot_nt, q, k)
    qc = each(lambda a, c: _dot_nt(a, c.astype(BF16)), q, c_prev)
    b_c = each(lambda sn, r: jnp.sum(jnp.where(sn, 1.0, 0.0) * r, axis=1, keepdims=True), seen, lf_r)
    b_r = each(lambda sn, c, r: jnp.sum(jnp.where(sn, 0.0, 1.0) * c, axis=0, keepdims=True) + r, seen, lf_c, lf_r)
    log_d = each(lambda sn, bc, br, g: jnp.where(sn, bc - br + g, -jnp.inf), seen, b_c, b_r, ig_r)
    log_inter = each(lambda bc, m: bc + m, b_c, m_prev)
    m_t = each(lambda li, ld: jnp.maximum(li, jnp.max(ld, axis=1, keepdims=True)), log_inter, log_d)
    w_intra = each(lambda ld, mt, s: jnp.exp(ld - mt) * s, log_d, m_t, qk)
    w_inter = each(lambda li, mt: jnp.exp(li - mt), log_inter, m_t)
    wiv = each(lambda w, vv: _dot(w.astype(BF16), vv), w_intra, v)
    num = each(lambda wi, a, b: wi * a + b, w_inter, qc, wiv)
    den = each(lambda wi, a, n, w: wi * jnp.sum(a.astype(F32) * n, axis=1, keepdims=True)
               + jnp.sum(w, axis=1, keepdims=True), w_inter, q, n_prev, w_intra)
    for ch, nu, de, mt in zip(chains, num, den, m_t):
        ch['o'][...] = (nu / jnp.maximum(jnp.abs(de), jnp.exp(-mt))).astype(ch['o'].dtype)
    b_end = each(lambda c: jnp.sum(c, axis=0, keepdims=True), lf_c)
    log_w = each(lambda be, bc, g: be - bc + g, b_end, b_c, ig_c)
    m_new = each(lambda be, m, lw: jnp.maximum(be + m, jnp.max(lw, axis=0, keepdims=True)), b_end, m_prev, log_w)
    w = each(lambda lw, mn: jnp.exp(lw - mn), log_w, m_new)
    decay = each(lambda be, m, mn: jnp.exp(be + m - mn), b_end, m_prev, m_new)
    upd = each(lambda ww, vv, kk: lax.dot_general((ww * vv.astype(F32)).astype(BF16), kk, (((0,), (0,)), ((), ())),
                                                  preferred_element_type=F32), w, v, k)
    for ch, dc, cp, u_, np_, ww, kk, mn in zip(chains, decay, c_prev, upd, n_prev, w, k, m_new):
        ch['c'][...] = dc * cp + u_
        ch['n'][...] = dc * np_ + jnp.sum(ww * kk.astype(F32), axis=0, keepdims=True)
        ch['m'][...] = mn


def _mlstm_kernel(qf_ref, kf_ref, vf_ref, gcf_ref, grf_ref, qb_ref, kb_ref, vb_ref, gcb_ref, grb_ref,
                  hf_ref, hb_ref, c_scr, n_scr, m_scr):
    @pl.when(pl.program_id(1) == 0)
    def _():
        c_scr[...] = jnp.zeros(c_scr.shape, F32)
        n_scr[...] = jnp.zeros(n_scr.shape, F32)
        m_scr[...] = jnp.zeros(m_scr.shape, F32)

    dh = MLSTM_DIM
    chains = []
    for i in range(gcf_ref.shape[0]):
        cols = slice(i * dh, (i + 1) * dh)
        for j, (q_ref, k_ref, v_ref, gc_ref, gr_ref, o_ref) in enumerate(
                ((qf_ref, kf_ref, vf_ref, gcf_ref, grf_ref, hf_ref), (qb_ref, kb_ref, vb_ref, gcb_ref, grb_ref, hb_ref))):
            gc = gc_ref[i]
            gr = gr_ref[i]
            s = 2 * i + j
            chains.append(dict(q=q_ref[:, cols], k=k_ref[:, cols], v=v_ref[:, cols],
                               ig_c=gc[:, 2 * j:2 * j + 1], fg_c=gc[:, 2 * j + 1:2 * j + 2],
                               ig_r=gr[2 * j:2 * j + 1, :], fg_r=gr[2 * j + 1:2 * j + 2, :],
                               c=c_scr.at[s], n=n_scr.at[s], m=m_scr.at[s], o=o_ref.at[:, cols], reverse=j == 1))
    _mlstm_chains(chains)


def _mlstm_scan(qk, u, gates, n_lat, n_ctx):
    s_pad = qk.shape[0]
    L, dh, nh, hp = MLSTM_CHUNK, MLSTM_DIM, MLSTM_HEADS, MLSTM_HEADS_PER_STEP
    nlc, ncc = n_lat // L, n_ctx // L
    steps = nlc + ncc
    ng = nh // hp
    g4 = gates.reshape(s_pad, 2, 2, nh).transpose(3, 0, 1, 2).reshape(nh, s_pad, 4)
    g4t = g4.transpose(0, 2, 1)

    def fwd(c):
        return jnp.where(c < ncc, nlc + c, c - ncc)

    def bwd(c):
        return jnp.where(c < ncc, nlc + ncc - 1 - c, nlc - 1 - (c - ncc))

    def specs(pos):
        return [pl.BlockSpec((L, hp * dh), lambda h, c: (pos(c), h)),
                pl.BlockSpec((L, hp * dh), lambda h, c: (pos(c), ng + h)),
                pl.BlockSpec((L, hp * dh), lambda h, c: (pos(c), 2 * ng + h)),
                pl.BlockSpec((hp, L, 4), lambda h, c: (h, pos(c), 0)),
                pl.BlockSpec((hp, 4, L), lambda h, c: (h, 0, pos(c)))]

    out_shape = jax.ShapeDtypeStruct((steps * L, nh * dh), BF16)
    return pl.pallas_call(
        _mlstm_kernel,
        grid=(ng, steps),
        in_specs=specs(fwd) + specs(bwd),
        out_specs=[pl.BlockSpec((L, hp * dh), lambda h, c: (fwd(c), h)),
                   pl.BlockSpec((L, hp * dh), lambda h, c: (bwd(c), h))],
        out_shape=[out_shape, out_shape],
        scratch_shapes=[pltpu.VMEM((2 * hp, dh, dh), F32), pltpu.VMEM((2 * hp, 1, dh), F32),
                        pltpu.VMEM((2 * hp, 1, 1), F32)],
        compiler_params=_cparams(("parallel", "arbitrary")),
        name="mlstm_scan",
    )(qk, qk, u, g4, g4t, qk, qk, u, g4, g4t)


def kernel(x, c, ctx, c_ctx, w_mod_0, b_mod_0, g_pre_mix_0, g_post_mix_0, w_in_0, lambda_q1_0, lambda_k1_0, lambda_q2_0, lambda_k2_0, g_subln_0, g_q_0, g_k_0, w_out_0, g_pre_ffn_0, g_post_ffn_0, w_router_0, w_gate_0, w_up_0, w_down_0, w_mod_1, b_mod_1, g_pre_mix_1, g_post_mix_1, w_in_1, b_gate_1, w_conv_1, b_conv_1, g_head_1, w_out_1, g_pre_ffn_1, g_post_ffn_1, w_router_1, w_gate_1, w_up_1, w_down_1):
    b, n_lat, d = x.shape
    n_ctx = ctx.shape[1]
    assert b == 1 and n_lat % ROW_TILE == 0 and n_ctx % MLSTM_CHUNK == 0 and n_ctx <= ROW_TILE
    s_valid = n_lat + n_ctx
    s_pad = n_lat + ROW_TILE
    n_lat_tiles = n_lat // ROW_TILE

    ctx_pad = jnp.concatenate([ctx[0], jnp.zeros((s_pad - s_valid, d), F32)], axis=0)
    cc = jnp.concatenate([c, c_ctx[None, :], jnp.zeros((6, d), F32)], axis=0)

    mod0 = _modulation(cc, w_mod_0, b_mod_0)
    tabs = _rope_tables(n_lat, n_ctx, s_pad)
    p0 = _proj0(x[0], ctx_pad, mod0, g_pre_mix_0, w_in_0.astype(BF16), tabs, g_q_0, g_k_0, n_lat_tiles)
    lam_init = 0.8 - 0.6 * math.exp(-0.3 * 0)
    lam4 = jnp.stack([lambda_q1_0, lambda_k1_0, lambda_q2_0, lambda_k2_0])
    o_a = _attn_diff(p0, lam4, g_subln_0, n_lat_tiles, s_valid, lam_init)
    o_b = _attn_gqa(p0, n_lat, s_valid)
    ka = o_a.shape[1]
    x_mid, hl, aff = _mixer_out(
        _out0_kernel, (o_a, o_b, ctx_pad),
        lambda tm: [pl.BlockSpec((tm, ka), lambda i: (i, 0)), pl.BlockSpec((tm, o_b.shape[1]), lambda i: (i, 0)),
                    pl.BlockSpec((tm, d), lambda i: (jnp.maximum(i - n_lat // tm, 0), 0))],
        x[0], mod0, w_out_0.astype(BF16), g_post_mix_0, g_pre_ffn_0, w_router_0, s_pad, n_lat // 256,
        lambda tm: [], "attn_out")
    p_moe0 = dict(w_gate=w_gate_0, w_up=w_up_0, w_down=w_down_0, g_post_ffn=g_post_ffn_0)
    xs = _moe(hl, aff, x_mid, mod0, p_moe0, n_lat, n_ctx, s_pad, True)

    mod1 = _modulation(cc, w_mod_1, b_mod_1)
    inner = MLSTM_HEADS * MLSTM_DIM
    u, gates = _proj1(xs, mod1, g_pre_mix_1, w_in_1.astype(BF16), 4 * inner, w_in_1[:, 4 * inner:], b_gate_1,
                      n_lat_tiles)
    qk = _conv_silu(u, w_conv_1, b_conv_1, 2 * inner, n_lat, n_ctx)
    hf, hb = _mlstm_scan(qk, u, gates, n_lat, n_ctx)
    og_blk = 3 * inner // inner
    x_mid, hl, aff = _mixer_out(
        _out1_kernel, (hf, hb, u, g_head_1.reshape(1, inner)),
        lambda tm: [pl.BlockSpec((tm, inner), lambda i: (i, 0)), pl.BlockSpec((tm, inner), lambda i: (i, 0)),
                    pl.BlockSpec((tm, inner), lambda i: (i, og_blk)), pl.BlockSpec((1, inner), lambda i: (0, 0))],
        xs, mod1, w_out_1.astype(BF16), g_post_mix_1, g_pre_ffn_1, w_router_1, n_lat, n_lat // 256,
        lambda tm: [pltpu.VMEM((tm, inner), BF16)], "mlstm_out")
    p_moe1 = dict(w_gate=w_gate_1, w_up=w_up_1, w_down=w_down_1, g_post_ffn=g_post_ffn_1)
    out = _moe(hl, aff, x_mid, mod1, p_moe1, n_lat, n_ctx, s_pad, False)
    return out[None]
```

```python
import functools
import math

import jax
import jax.numpy as jnp
from jax import lax
from jax.experimental import pallas as pl
from jax.experimental.pallas import tpu as pltpu

F32 = jnp.float32
BF16 = jnp.bfloat16
HIGHEST = lax.Precision.HIGHEST

GRID_W = 64
ROPE_THETA = 10000.0
RMS_EPS = 1e-6
N_MOD = 6
DIFF_HEADS = 8
DIFF_DIM = 64
GQA_HEADS = 8
GQA_KV_HEADS = 2
GQA_DIM = 128
MLSTM_HEADS = 8
MLSTM_DIM = 256
MLSTM_CHUNK = 128
N_EXPERTS = 16
EC_CAPACITY_FACTOR = 2

LANES = 128
ROW_TILE = 512
MLSTM_HEADS_PER_STEP = 4
FFN_MAX_TILE_ROWS = 1024
FFN_STEPS = 8
ATTN_TK = 512
ATTN_ROW_GROUP = 256
LOG2_E = math.log2(math.e)
TOPK_BLOCKS = 128
VMEM_LIMIT = 56 * 1024 * 1024


def _cparams(sem):
    return pltpu.CompilerParams(dimension_semantics=sem, vmem_limit_bytes=VMEM_LIMIT)


def _rms(x, g):
    return x * lax.rsqrt(jnp.mean(x * x, axis=-1, keepdims=True) + RMS_EPS) * g


def _dot(a, b):
    return jnp.dot(a, b, preferred_element_type=F32)


def _dot_nt(a, b, precision=None):
    return lax.dot_general(a, b, (((1,), (1,)), ((), ())), precision=precision, preferred_element_type=F32)


def _mod_kernel(c_ref, w_ref, b_ref, o_ref):
    c = c_ref[...]
    s = c * jax.nn.sigmoid(c)
    o_ref[...] = lax.dot_general(s, w_ref[...], (((1,), (0,)), ((), ())), precision=HIGHEST,
                                 preferred_element_type=F32) + b_ref[...]


def _modulation(cc, w_mod, b_mod):
    d, n = w_mod.shape
    tn = 1024
    return pl.pallas_call(
        _mod_kernel,
        grid=(n // tn,),
        in_specs=[pl.BlockSpec((8, d), lambda j: (0, 0)),
                  pl.BlockSpec((d, tn), lambda j: (0, j)),
                  pl.BlockSpec((1, tn), lambda j: (0, j))],
        out_specs=pl.BlockSpec((8, tn), lambda j: (0, j)),
        out_shape=jax.ShapeDtypeStruct((8, n), F32),
        compiler_params=_cparams(("arbitrary",)),
        name="modulation",
    )(cc, w_mod, b_mod.reshape(1, n))


def _mod_rows(mod_ref, r, k, d):
    return mod_ref[pl.ds(r, 1), k * d:(k + 1) * d]


def _premix(x, mod_ref, g_ref, r, d):
    return _rms(x, g_ref[...]) * (1.0 + _mod_rows(mod_ref, r, 1, d)) + _mod_rows(mod_ref, r, 0, d)


def _dot_hilo(a, b):
    a_hi = a.astype(BF16)
    a_lo = (a - a_hi.astype(F32)).astype(BF16)
    b_hi = b.astype(BF16)
    b_lo = (b - b_hi.astype(F32)).astype(BF16)
    return _dot(a_hi, b_hi) + (_dot(a_hi, b_lo) + _dot(a_lo, b_hi))


def _rope_a(x, ca, sa, sb):
    return x * ca + pltpu.roll(x, LANES - DIFF_DIM // 2, 1) * sa + pltpu.roll(x, DIFF_DIM // 2, 1) * sb


def _rope_b(x, cb, sb):
    return x * cb + pltpu.roll(x, GQA_DIM // 2, 1) * sb


def _proj0_kernel(x_ref, c_ref, mod_ref, gpre_ref, w_ref, ca_ref, saa_ref, sab_ref, cb_ref, sbb_ref, gq_ref, gk_ref,
                  o_ref, *, n_lat_tiles, d, tn):
    is_ctx = pl.program_id(0) >= n_lat_tiles
    x = jnp.where(is_ctx, c_ref[...], x_ref[...])
    h = _premix(x, mod_ref, gpre_ref, jnp.where(is_ctx, 1, 0), d).astype(BF16)
    n_qa = DIFF_HEADS
    n_qb = GQA_HEADS * GQA_DIM // LANES
    n_kb = GQA_KV_HEADS * GQA_DIM // LANES

    def epilogue(a, gi):
        if gi < 2 * n_qa:
            a = _rope_a(a, ca_ref[...], saa_ref[...], sab_ref[...])
            return a * (DIFF_DIM ** -0.5 * LOG2_E) if gi < n_qa else a
        if gi < 3 * n_qa:
            return a
        if gi < 3 * n_qa + n_qb:
            return _rope_b(_rms(a, gq_ref[...]), cb_ref[...], sbb_ref[...]) * (GQA_DIM ** -0.5 * LOG2_E)
        if gi < 3 * n_qa + n_qb + n_kb:
            return _rope_b(_rms(a, gk_ref[...]), cb_ref[...], sbb_ref[...])
        return a

    for j in range(w_ref.shape[1] // tn):
        acc = _dot(h, w_ref[:, j * tn:(j + 1) * tn])
        for g in range(tn // LANES):
            gi = j * (tn // LANES) + g
            o_ref[:, gi * LANES:(gi + 1) * LANES] = epilogue(acc[:, g * LANES:(g + 1) * LANES], gi).astype(BF16)


def _proj0(x, ctx_pad, mod, g_pre, w_in, tabs, g_q, g_k, n_lat_tiles):
    d = x.shape[1]
    s_pad = x.shape[0] + ctx_pad.shape[0]
    n_out = w_in.shape[1]
    tm, tn = ROW_TILE, 512
    row = lambda i: (i, 0)
    const = lambda i: (0, 0)
    once = pl.Buffered(1)
    tab_spec = pl.BlockSpec((tm, LANES), row)
    return pl.pallas_call(
        functools.partial(_proj0_kernel, n_lat_tiles=n_lat_tiles, d=d, tn=tn),
        grid=(s_pad // tm,),
        in_specs=[pl.BlockSpec((tm, d), lambda i: (jnp.minimum(i, n_lat_tiles - 1), 0)),
                  pl.BlockSpec((tm, d), const, pipeline_mode=once),
                  pl.BlockSpec((8, N_MOD * d), const, pipeline_mode=once),
                  pl.BlockSpec((1, d), const),
                  pl.BlockSpec((d, n_out), const, pipeline_mode=once),
                  tab_spec, tab_spec, tab_spec, tab_spec, tab_spec,
                  pl.BlockSpec((1, LANES), const),
                  pl.BlockSpec((1, LANES), const)],
        out_specs=pl.BlockSpec((tm, n_out), row),
        out_shape=jax.ShapeDtypeStruct((s_pad, n_out), BF16),
        compiler_params=_cparams(("parallel",)),
        name="proj0",
    )(x, ctx_pad, mod, g_pre.reshape(1, d), w_in, *tabs, g_q.reshape(1, LANES), g_k.reshape(1, LANES))


def _rope_tables(n_lat, n_ctx, s_pad):
    rows = jnp.repeat(jnp.arange(n_lat // GRID_W, dtype=F32), GRID_W)
    cols = jnp.tile(jnp.arange(GRID_W, dtype=F32), n_lat // GRID_W)

    def angles(dim):
        axis_dim = dim // 2
        inv_freq = ROPE_THETA ** (-jnp.arange(0, axis_dim, 2, dtype=F32) / axis_dim)
        ang = jnp.concatenate([rows[:, None] * inv_freq, cols[:, None] * inv_freq], axis=-1)
        return jnp.cos(ang), jnp.sin(ang)

    def pad(t, fill):
        return jnp.concatenate([t, jnp.full((s_pad - n_lat, LANES), fill, F32)], axis=0)

    cos_a, sin_a = angles(DIFF_DIM)
    zero = jnp.zeros_like(sin_a)
    ca = jnp.tile(cos_a, (1, 4))
    saa = jnp.tile(jnp.concatenate([-sin_a, zero], axis=1), (1, 2))
    sab = jnp.tile(jnp.concatenate([zero, sin_a], axis=1), (1, 2))
    cos_b, sin_b = angles(GQA_DIM)
    cb = jnp.tile(cos_b, (1, 2))
    sbb = jnp.concatenate([-sin_b, sin_b], axis=1)
    return pad(ca, 1.0), pad(saa, 0.0), pad(sab, 0.0), pad(cb, 1.0), pad(sbb, 0.0)


def _flash_rows(q, k_ref, v_ref, m_scr, acc_scr, s_scr, is_ctx_q, tk, s_valid):
    n_chunks = k_ref.shape[0] // tk
    last = n_chunks - 1
    groups = [slice(g * ATTN_ROW_GROUP, (g + 1) * ATTN_ROW_GROUP) for g in range(q.shape[0] // ATTN_ROW_GROUP)]
    m_scr[...] = jnp.full(m_scr.shape, -jnp.inf, F32)
    acc_scr[...] = jnp.zeros(acc_scr.shape, F32)

    def rows_of(c):
        return pl.ds(c * tk if isinstance(c, int) else pl.multiple_of(c * tk, tk), tk)

    def scores(c, buf, rows):
        s_scr[buf, rows] = _dot_nt(q[rows], k_ref[rows_of(c), :])

    def consume(c, buf, rows, v1, masked):
        s = s_scr[buf, rows]
        if masked:
            kpos = c * tk + lax.broadcasted_iota(jnp.int32, (1, tk), 1)
            s = jnp.where(kpos < s_valid, s, -jnp.inf)
        m_prev = m_scr[rows]
        m_new = jnp.maximum(m_prev, jnp.max(s, axis=1, keepdims=True))
        p = jnp.exp2(s - m_new).astype(BF16)
        acc_scr[rows] = jnp.exp2(m_prev - m_new) * acc_scr[rows] + _dot(p, v1)
        m_scr[rows] = m_new

    def values(c):
        vc = v_ref[rows_of(c), :]
        return jnp.concatenate([vc, jnp.ones_like(vc)], axis=1)

    unroll = max(u for u in (2, 4, 8) if last % u == 0)
    lo = jnp.where(is_ctx_q, last, 0)
    for rows in groups:
        scores(lo, 0, rows)

    def body(i, carry):
        c = lo + unroll * i
        for j in range(unroll):
            v1 = values(c + j)
            for rows in groups:
                scores(c + j + 1, (j + 1) % 2, rows)
                consume(c + j, j % 2, rows, v1, False)
        return carry

    lax.fori_loop(0, (last - lo) // unroll, body, 0)
    v1 = values(last)
    for rows in groups:
        consume(last, 0, rows, v1, True)


def _attn_diff_kernel(q_ref, k_ref, v_ref, lam_ref, gsub_ref, o_ref, m_scr, acc_scr, s_scr,
                      *, n_lat_tiles, tk, s_valid, lam_init):
    q = q_ref[...]
    tq, hd = q.shape
    first = lax.broadcasted_iota(jnp.int32, (1, hd), 1) < DIFF_DIM
    zero = jnp.zeros_like(q)
    qq = jnp.concatenate([jnp.where(first, q, zero), jnp.where(first, zero, q)], axis=0)
    _flash_rows(qq, k_ref, v_ref, m_scr, acc_scr, s_scr, pl.program_id(1) >= n_lat_tiles, tk, s_valid)
    lam = (jnp.exp(jnp.sum(lam_ref[0:1, :] * lam_ref[1:2, :], axis=1, keepdims=True))
           - jnp.exp(jnp.sum(lam_ref[2:3, :] * lam_ref[3:4, :], axis=1, keepdims=True)) + lam_init)
    o = (acc_scr[0:tq, 0:hd] / acc_scr[0:tq, hd:hd + 1]
         - lam * (acc_scr[tq:2 * tq, 0:hd] / acc_scr[tq:2 * tq, hd:hd + 1]))
    o_ref[...] = (_rms(o, gsub_ref[...]) * (1.0 - lam_init)).astype(BF16)


def _attn_diff(p0, lam4, g_subln, n_lat_tiles, s_valid, lam_init):
    s_pad = p0.shape[0]
    tq = ROW_TILE
    hd = 2 * DIFF_DIM
    k0 = DIFF_HEADS
    v0 = 2 * DIFF_HEADS
    return pl.pallas_call(
        functools.partial(_attn_diff_kernel, n_lat_tiles=n_lat_tiles, tk=ATTN_TK, s_valid=s_valid, lam_init=lam_init),
        grid=(DIFF_HEADS, s_pad // tq),
        in_specs=[pl.BlockSpec((tq, hd), lambda h, iq: (iq, h)),
                  pl.BlockSpec((s_pad, hd), lambda h, iq: (0, k0 + h)),
                  pl.BlockSpec((s_pad, hd), lambda h, iq: (0, v0 + h)),
                  pl.BlockSpec((4, DIFF_DIM), lambda h, iq: (0, 0)),
                  pl.BlockSpec((1, hd), lambda h, iq: (0, 0))],
        out_specs=pl.BlockSpec((tq, hd), lambda h, iq: (iq, h)),
        out_shape=jax.ShapeDtypeStruct((s_pad, DIFF_HEADS * hd), BF16),
        scratch_shapes=[pltpu.VMEM((2 * tq, 1), F32), pltpu.VMEM((2 * tq, 2 * hd), F32),
                        pltpu.VMEM((2, 2 * tq, ATTN_TK), F32)],
        compiler_params=_cparams(("parallel", "arbitrary")),
        name="attn_diff",
    )(p0, p0, p0, lam4, g_subln.reshape(1, hd))


def _attn_gqa_kernel(q_ref, k_ref, v_ref, o_ref, m_scr, acc_scr, s_scr, *, n_lat_tiles, tk, s_valid, group):
    tq = q_ref.shape[0]
    dh = GQA_DIM
    qq = jnp.concatenate([q_ref[:, g * dh:(g + 1) * dh] for g in range(group)], axis=0)
    _flash_rows(qq, k_ref, v_ref, m_scr, acc_scr, s_scr, pl.program_id(1) >= n_lat_tiles, tk, s_valid)
    for g in range(group):
        rows = slice(g * tq, (g + 1) * tq)
        o_ref[:, g * dh:(g + 1) * dh] = (acc_scr[rows, 0:dh] / acc_scr[rows, dh:dh + 1]).astype(BF16)


def _attn_gqa(p0, n_lat, s_valid):
    s_pad = p0.shape[0]
    tq = 256
    group = GQA_HEADS // GQA_KV_HEADS
    qw = group * GQA_DIM
    q0 = 3 * DIFF_HEADS * 2 * DIFF_DIM // qw
    k0 = (3 * DIFF_HEADS * 2 * DIFF_DIM + GQA_HEADS * GQA_DIM) // GQA_DIM
    v0 = k0 + GQA_KV_HEADS
    return pl.pallas_call(
        functools.partial(_attn_gqa_kernel, n_lat_tiles=n_lat // tq, tk=ATTN_TK, s_valid=s_valid, group=group),
        grid=(GQA_KV_HEADS, s_pad // tq),
        in_specs=[pl.BlockSpec((tq, qw), lambda h, iq: (iq, q0 + h)),
                  pl.BlockSpec((s_pad, GQA_DIM), lambda h, iq: (0, k0 + h)),
                  pl.BlockSpec((s_pad, GQA_DIM), lambda h, iq: (0, v0 + h))],
        out_specs=pl.BlockSpec((tq, qw), lambda h, iq: (iq, h)),
        out_shape=jax.ShapeDtypeStruct((s_pad, GQA_HEADS * GQA_DIM), BF16),
        scratch_shapes=[pltpu.VMEM((group * tq, 1), F32), pltpu.VMEM((group * tq, 2 * GQA_DIM), F32),
                        pltpu.VMEM((2, group * tq, ATTN_TK), F32)],
        compiler_params=_cparams(("parallel", "arbitrary")),
        name="attn_gqa",
    )(p0, p0, p0)


def _post_mix(y, x, mod_ref, r, gpost_ref, gffn_ref, wr_ref, xo_ref, hl_ref, aff_ref, d):
    xn = x + _mod_rows(mod_ref, r, 2, d) * _rms(y, gpost_ref[...])
    xo_ref[...] = xn
    hl = _rms(xn, gffn_ref[...]) * (1.0 + _mod_rows(mod_ref, r, 4, d)) + _mod_rows(mod_ref, r, 3, d)
    hl_ref[...] = hl
    logits = _dot_hilo(hl, wr_ref[...])
    e = jnp.exp(logits - jnp.max(logits, axis=1, keepdims=True))
    aff_ref[...] = e / jnp.sum(e, axis=1, keepdims=True)


def _out0_kernel(oa_ref, ob_ref, c_ref, x_ref, mod_ref, w_ref, gpost_ref, gffn_ref, wr_ref, xo_ref, hl_ref, aff_ref,
                 *, n_lat_tiles, d):
    is_ctx = pl.program_id(0) >= n_lat_tiles
    r = jnp.where(is_ctx, 1, 0)
    ka = oa_ref.shape[1]
    y = _dot(oa_ref[...], w_ref[0:ka, :]) + _dot(ob_ref[...], w_ref[ka:, :])
    x = jnp.where(is_ctx, c_ref[...], x_ref[...])
    _post_mix(y, x, mod_ref, r, gpost_ref, gffn_ref, wr_ref, xo_ref, hl_ref, aff_ref, d)


def _out1_kernel(hf_ref, hb_ref, og_ref, ghead_ref, x_ref, mod_ref, w_ref, gpost_ref, gffn_ref, wr_ref,
                 xo_ref, hl_ref, aff_ref, y_scr, *, n_lat_tiles, d):
    r = jnp.where(pl.program_id(0) >= n_lat_tiles, 1, 0)
    for h in range(MLSTM_HEADS):
        sl = slice(h * MLSTM_DIM, (h + 1) * MLSTM_DIM)
        hs = hf_ref[:, sl].astype(F32) + hb_ref[:, sl].astype(F32)
        hn = _rms(hs, ghead_ref[:, sl])
        y_scr[:, sl] = (jax.nn.sigmoid(og_ref[:, sl].astype(F32)) * hn).astype(BF16)
    y = _dot(y_scr[...], w_ref[...])
    _post_mix(y, x_ref[...], mod_ref, r, gpost_ref, gffn_ref, wr_ref, xo_ref, hl_ref, aff_ref, d)


def _mixer_out(kernel_fn, mix_inputs, mix_specs, x, mod, w_out, g_post, g_ffn, w_router, n_rows, n_lat_tiles_tm,
               scratch, name):
    d = x.shape[1]
    tm = 256
    row = lambda i: (i, 0)
    x_row = lambda i: (jnp.minimum(i, x.shape[0] // tm - 1), 0)
    const = lambda i: (0, 0)
    ne = w_router.shape[1]
    return pl.pallas_call(
        functools.partial(kernel_fn, n_lat_tiles=n_lat_tiles_tm, d=d),
        grid=(n_rows // tm,),
        in_specs=mix_specs(tm) + [pl.BlockSpec((tm, d), x_row),
                                  pl.BlockSpec((8, N_MOD * d), const),
                                  pl.BlockSpec(w_out.shape, const),
                                  pl.BlockSpec((1, d), const),
                                  pl.BlockSpec((1, d), const),
                                  pl.BlockSpec((d, ne), const)],
        out_specs=[pl.BlockSpec((tm, d), row), pl.BlockSpec((tm, d), row), pl.BlockSpec((tm, ne), row)],
        out_shape=[jax.ShapeDtypeStruct((n_rows, d), F32), jax.ShapeDtypeStruct((n_rows, d), F32),
                   jax.ShapeDtypeStruct((n_rows, ne), F32)],
        scratch_shapes=scratch(tm),
        compiler_params=_cparams(("parallel",)),
        name=name,
    )(*mix_inputs, x, mod, w_out, g_post.reshape(1, d), g_ffn.reshape(1, d), w_router)


def _incl_cumsum(maskf, upper, strict_lower):
    mb = maskf.astype(BF16)
    local = _dot(mb, upper)
    start = jnp.sum(_dot(strict_lower, mb), axis=1, keepdims=True)
    return local, start


def _topk_kernel(aff_ref, idx_ref, gate_ref, dest_ref, cnt_ref, cnt_acc, *, cap, dest_stride, row_off):
    e = pl.program_id(0)
    ne = pl.num_programs(0)
    nb = aff_ref.shape[1]
    a = aff_ref[0]
    bits = pltpu.bitcast(a, jnp.int32)

    thr = jnp.zeros((1, 1), jnp.int32)
    for bit in range(30, -1, -1):
        cand = thr | (1 << bit)
        c = jnp.sum(jnp.sum(jnp.where(bits >= cand, 1.0, 0.0), axis=1, keepdims=True), axis=0, keepdims=True)
        thr = jnp.where(c >= cap, cand, thr)

    r_i = lax.broadcasted_iota(jnp.int32, (LANES, LANES), 0)
    c_i = lax.broadcasted_iota(jnp.int32, (LANES, LANES), 1)
    upper = jnp.where(r_i <= c_i, 1.0, 0.0).astype(BF16)
    rb = lax.broadcasted_iota(jnp.int32, (nb, nb), 0)
    cb = lax.broadcasted_iota(jnp.int32, (nb, nb), 1)
    strict_lower = jnp.where(cb < rb, 1.0, 0.0).astype(BF16)

    gt = bits > thr
    eq = bits == thr
    n_gt = jnp.sum(jnp.sum(jnp.where(gt, 1.0, 0.0), axis=1, keepdims=True), axis=0, keepdims=True)
    eq_local, eq_start = _incl_cumsum(jnp.where(eq, 1.0, 0.0), upper, strict_lower)
    sel = jnp.logical_or(gt, jnp.logical_and(eq, eq_local + eq_start <= cap - n_gt))
    self_ = jnp.where(sel, 1.0, 0.0)
    sel_local, sel_start = _incl_cumsum(self_, upper, strict_lower)
    sel_tot = jnp.sum(self_, axis=1, keepdims=True)

    @pl.when(e == 0)
    def _():
        cnt_acc[...] = jnp.zeros(cnt_acc.shape, F32)

    earlier = cnt_acc[...]
    cnt_acc[...] = earlier + self_

    lane = lax.broadcasted_iota(jnp.int32, (nb, LANES), 1)
    cols = jnp.where(lane == 0, sel_start, jnp.where(lane == 1, sel_tot, 0.0))
    rows = cols.T
    start_row = rows[0:1, 0:nb]
    tot_row = rows[1:2, 0:nb]

    p = lax.broadcasted_iota(jnp.int32, (cap, 1), 0).astype(F32)
    onehot = jnp.where(jnp.logical_and(start_row <= p, p < start_row + tot_row), 1.0, 0.0)
    blk = lax.broadcasted_iota(jnp.int32, (1, nb), 1).astype(F32)
    p_start = jnp.sum(onehot * start_row, axis=1, keepdims=True)
    p_blk = jnp.sum(onehot * blk, axis=1, keepdims=True)
    ohb = onehot.astype(BF16)
    local_rows = _dot(ohb, sel_local.astype(BF16))
    sel_rows = _dot(ohb, self_.astype(BF16))
    earlier_rows = _dot(ohb, earlier.astype(BF16))
    aff_rows = lax.dot_general(onehot, a, (((1,), (0,)), ((), ())), precision=HIGHEST, preferred_element_type=F32)
    hit = jnp.where(jnp.logical_and(local_rows == p - p_start + 1.0, sel_rows > 0.5), 1.0, 0.0)
    lane_f = lax.broadcasted_iota(jnp.int32, (1, LANES), 1).astype(F32)
    tok = p_blk * LANES + jnp.sum(hit * lane_f, axis=1, keepdims=True)
    gate = jnp.sum(hit * aff_rows, axis=1, keepdims=True)
    rank = jnp.sum(hit * earlier_rows, axis=1, keepdims=True)
    idx_ref[0] = tok.astype(jnp.int32) + row_off
    gate_ref[0] = gate
    dest_ref[0] = (rank * dest_stride + tok).astype(jnp.int32) + row_off

    @pl.when(e == ne - 1)
    def _():
        cnt_ref[...] = cnt_acc[...].astype(jnp.int32)


def _route(aff_t, cap, dest_stride, row_off):
    ne, n = aff_t.shape
    n_pad = TOPK_BLOCKS * LANES
    assert n <= n_pad
    a = jnp.pad(aff_t, ((0, 0), (0, n_pad - n)), constant_values=-1.0).reshape(ne, TOPK_BLOCKS, LANES)
    slot = pl.BlockSpec((1, cap, 1), lambda e: (e, 0, 0))
    idx, gate, dest, cnt = pl.pallas_call(
        functools.partial(_topk_kernel, cap=cap, dest_stride=dest_stride, row_off=row_off),
        grid=(ne,),
        in_specs=[pl.BlockSpec((1, TOPK_BLOCKS, LANES), lambda e: (e, 0, 0))],
        out_specs=[slot, slot, slot, pl.BlockSpec((TOPK_BLOCKS, LANES), lambda e: (0, 0))],
        out_shape=[jax.ShapeDtypeStruct((ne, cap, 1), jnp.int32), jax.ShapeDtypeStruct((ne, cap, 1), F32),
                   jax.ShapeDtypeStruct((ne, cap, 1), jnp.int32), jax.ShapeDtypeStruct((TOPK_BLOCKS, LANES), jnp.int32)],
        scratch_shapes=[pltpu.VMEM((TOPK_BLOCKS, LANES), F32)],
        compiler_params=_cparams(("arbitrary",)),
        name="route_topk",
    )(a)
    return idx.reshape(ne * cap), gate, dest.reshape(ne * cap), cnt.reshape(n_pad)[:n]


def _pack_bf16_pairs(y):
    n = y.shape[1] // 2
    bits = lax.bitcast_convert_type(y.astype(BF16).astype(F32), jnp.uint32)
    return (bits[:, n:] & jnp.uint32(0xFFFF0000)) | (bits[:, :n] >> 16)


def _unpack_bf16_pairs(w):
    return (lax.bitcast_convert_type(w << 16, F32), lax.bitcast_convert_type(w & jnp.uint32(0xFFFF0000), F32))


def _row_copy(src, s, dst, t, sem):
    return pltpu.make_async_copy(src.at[pl.ds(s, 1)], dst.at[pl.ds(t, 1)], sem)


def _ffn_body(idx_sm, dest_sm, hl_hbm, gate_ref, wg_ref, wu_ref, wd_ref, z_hbm, xbuf, xb, acc, zp, gsem, ssem, *, tm):
    f = pl.program_id(2)
    nm = pl.num_programs(1)
    t = pl.program_id(0) * nm + pl.program_id(1)
    n_tiles = pl.num_programs(0) * nm
    cur = t % 2
    nxt = 1 - cur
    chunk = tm // FFN_STEPS
    last_step = f == pl.num_programs(2) - 1
    t_next = jnp.minimum(t + 1, n_tiles - 1)
    t_prev = jnp.maximum(t - 1, 0)

    def gather(tile, slot, r):
        return _row_copy(hl_hbm, idx_sm[tile * tm + r], xbuf.at[slot], r, gsem.at[slot])

    def scatter(tile, slot, r):
        return _row_copy(zp.at[slot], r, z_hbm, dest_sm[tile * tm + r], ssem.at[slot])

    def for_rows(fn):
        def body(r, c):
            fn(r)
            return c
        lax.fori_loop(0, tm, body, 0, unroll=8)

    @pl.when(jnp.logical_and(t == 0, f == 0))
    def _():
        for_rows(lambda r: gather(0, 0, r).start())
        zp[1] = jnp.zeros(zp.shape[1:], jnp.uint32)

    @pl.when(f == 0)
    def _():
        for_rows(lambda r: gather(t, cur, r).wait())
        xb[...] = xbuf[cur].astype(BF16)
        acc[...] = jnp.zeros(acc.shape, F32)

        @pl.when(t >= 1)
        def _():
            for_rows(lambda r: scatter(t_prev, cur, r).wait())

    for r in range(chunk):
        gather(t_next, nxt, f * chunk + r).start()
        scatter(t_prev, nxt, f * chunk + r).start()

    x = xb[...]
    g = _dot(x, wg_ref[0].astype(BF16))
    u = _dot(x, wu_ref[0].astype(BF16))
    hid = (g * jax.nn.sigmoid(g) * u).astype(BF16)
    acc[...] += _dot(hid, wd_ref[0].astype(BF16))

    @pl.when(last_step)
    def _():
        zp[cur] = _pack_bf16_pairs(acc[...] * gate_ref[0])

    @pl.when(jnp.logical_and(t == n_tiles - 1, last_step))
    def _():
        for_rows(lambda r: gather(t, nxt, r).wait())
        for_rows(lambda r: scatter(t_prev, nxt, r).wait())
        for_rows(lambda r: scatter(t, cur, r).start())
        for_rows(lambda r: scatter(t, cur, r).wait())


def _expert_ffn(idx, dest, hl, gate, w_gate, w_up, w_down, z_rows, n_tiles):
    ne, d, ff = w_gate.shape
    cap = gate.shape[1]
    tm = cap // n_tiles
    assert tm * n_tiles == cap and tm % FFN_STEPS == 0 and tm % 16 == 0
    tf = ff // FFN_STEPS
    any_spec = pl.BlockSpec(memory_space=pl.ANY)
    return pl.pallas_call(
        functools.partial(_ffn_body, tm=tm),
        grid_spec=pltpu.PrefetchScalarGridSpec(
            num_scalar_prefetch=2,
            grid=(ne, n_tiles, FFN_STEPS),
            in_specs=[any_spec,
                      pl.BlockSpec((1, tm, 1), lambda e, m, f, *_: (e, m, 0)),
                      pl.BlockSpec((1, d, tf), lambda e, m, f, *_: (e, 0, f)),
                      pl.BlockSpec((1, d, tf), lambda e, m, f, *_: (e, 0, f)),
                      pl.BlockSpec((1, tf, d), lambda e, m, f, *_: (e, f, 0))],
            out_specs=any_spec,
            scratch_shapes=[pltpu.VMEM((2, tm, d), F32), pltpu.VMEM((tm, d), BF16), pltpu.VMEM((tm, d), F32),
                            pltpu.VMEM((2, tm, d // 2), jnp.uint32),
                            pltpu.SemaphoreType.DMA((2,)), pltpu.SemaphoreType.DMA((2,))]),
        out_shape=jax.ShapeDtypeStruct((z_rows, d // 2), jnp.uint32),
        compiler_params=_cparams(("arbitrary", "arbitrary", "arbitrary")),
        name="expert_ffn",
    )(idx, dest, hl, gate, w_gate, w_up, w_down)


def _combine_kernel(maxc_sm, za_ref, zb_ref, cnt_ref, x_ref, mod_ref, g_ref, o_ref, acc, *, n_lat_tiles, d):
    i = pl.program_id(0)
    kk = pl.program_id(1)
    nk = pl.num_programs(1)

    @pl.when(kk == 0)
    def _():
        acc[...] = jnp.zeros(acc.shape, F32)

    def add_slab(z_ref, k):
        @pl.when(k < maxc_sm[i])
        def _():
            lo, hi = _unpack_bf16_pairs(z_ref[...])
            live = cnt_ref[...] > k
            acc[:, 0:d // 2] += jnp.where(live, lo, 0.0)
            acc[:, d // 2:d] += jnp.where(live, hi, 0.0)

    add_slab(za_ref, 2 * kk)
    add_slab(zb_ref, 2 * kk + 1)

    @pl.when(kk == nk - 1)
    def _():
        r = jnp.where(i >= n_lat_tiles, 1, 0)
        o_ref[...] = x_ref[...] + _mod_rows(mod_ref, r, 5, d) * _rms(acc[...], g_ref[...])


def _combine(z, cnt, x, mod, g_post, n_rows, z_stride, n_lat_tiles_tb):
    d = x.shape[1]
    tb = ROW_TILE
    nb = n_rows // tb
    zb = z_stride // tb
    maxc = jnp.max(cnt[:n_rows].reshape(nb, tb), axis=1)

    def z_map(j):
        return lambda i, kk, maxc_sm: (jnp.maximum(jnp.minimum(2 * kk + j, maxc_sm[i] - 1), 0) * zb + i, 0)

    return pl.pallas_call(
        functools.partial(_combine_kernel, n_lat_tiles=n_lat_tiles_tb, d=d),
        grid_spec=pltpu.PrefetchScalarGridSpec(
            num_scalar_prefetch=1,
            grid=(nb, N_EXPERTS // 2),
            in_specs=[pl.BlockSpec((tb, d // 2), z_map(0)),
                      pl.BlockSpec((tb, d // 2), z_map(1)),
                      pl.BlockSpec((tb, 1), lambda i, k, s: (i, 0)),
                      pl.BlockSpec((tb, d), lambda i, k, s: (i, 0)),
                      pl.BlockSpec((8, N_MOD * d), lambda i, k, s: (0, 0)),
                      pl.BlockSpec((1, d), lambda i, k, s: (0, 0))],
            out_specs=pl.BlockSpec((tb, d), lambda i, k, s: (i, 0)),
            scratch_shapes=[pltpu.VMEM((tb, d), F32)]),
        out_shape=jax.ShapeDtypeStruct((n_rows, d), F32),
        compiler_params=_cparams(("parallel", "arbitrary")),
        name="moe_combine",
    )(maxc, z, z, cnt.reshape(-1, 1), x, mod, g_post.reshape(1, d))


def _moe(hl, aff, x_mid, mod, p, n_lat, n_ctx, s_pad, with_ctx):
    aff_t = aff.T
    ne = N_EXPERTS
    cap_l = EC_CAPACITY_FACTOR * n_lat // ne
    n_tiles = max(1, cap_l // FFN_MAX_TILE_ROWS)
    idx, gate, dest, cnt = _route(aff_t[:, :n_lat], cap_l, s_pad, 0)
    n_rows = n_lat
    if with_ctx:
        cap_c = EC_CAPACITY_FACTOR * n_ctx // ne
        idx_c, gate_c, dest_c, cnt_c = _route(aff_t[:, n_lat:n_lat + n_ctx], cap_c, s_pad, n_lat)

        def per_tile(a, b):
            return jnp.concatenate([a.reshape(ne, n_tiles, cap_l // n_tiles), b.reshape(ne, n_tiles, cap_c // n_tiles)],
                                   axis=2).reshape(ne, cap_l + cap_c)
        idx = per_tile(idx, idx_c).reshape(-1)
        dest = per_tile(dest, dest_c).reshape(-1)
        gate = per_tile(gate, gate_c)[:, :, None]
        cnt = jnp.concatenate([cnt, cnt_c, jnp.zeros((s_pad - n_lat - n_ctx,), jnp.int32)])
        n_rows = s_pad
    z = _expert_ffn(idx, dest, hl, gate, p['w_gate'], p['w_up'], p['w_down'], ne * s_pad, n_tiles)
    return _combine(z, cnt, x_mid, mod, p['g_post_ffn'], n_rows, s_pad, n_lat // ROW_TILE)


def _proj1_kernel(x_ref, mod_ref, gpre_ref, w_ref, wg_ref, bg_ref, o_ref, g_ref, h_scr, *, n_lat_tiles, d):
    i = pl.program_id(0)
    j = pl.program_id(1)

    @pl.when(j == 0)
    def _():
        r = jnp.where(i >= n_lat_tiles, 1, 0)
        h = _premix(x_ref[...], mod_ref, gpre_ref, r, d)
        h_scr[...] = h.astype(BF16)
        g_ref[...] = _dot_hilo(h, wg_ref[...]) + bg_ref[...]

    o_ref[...] = _dot(h_scr[...], w_ref[...]).astype(BF16)


def _proj1(xs, mod, g_pre, w_main, n_out, w_gates, b_gate, n_lat_tiles):
    s_pad, d = xs.shape
    ng = w_gates.shape[1]
    tm, tn = ROW_TILE, 1024
    row = lambda i, j: (i, 0)
    const = lambda i, j: (0, 0)
    return pl.pallas_call(
        functools.partial(_proj1_kernel, n_lat_tiles=n_lat_tiles, d=d),
        grid=(s_pad // tm, n_out // tn),
        in_specs=[pl.BlockSpec((tm, d), row),
                  pl.BlockSpec((8, N_MOD * d), const),
                  pl.BlockSpec((1, d), const),
                  pl.BlockSpec((d, tn), lambda i, j: (0, j)),
                  pl.BlockSpec((d, ng), const),
                  pl.BlockSpec((1, ng), const)],
        out_specs=[pl.BlockSpec((tm, tn), lambda i, j: (i, j)), pl.BlockSpec((tm, ng), row)],
        out_shape=[jax.ShapeDtypeStruct((s_pad, n_out), BF16), jax.ShapeDtypeStruct((s_pad, ng), F32)],
        scratch_shapes=[pltpu.VMEM((tm, d), BF16)],
        compiler_params=_cparams(("parallel", "arbitrary")),
        name="proj1",
    )(xs, mod, g_pre.reshape(1, d), w_main, w_gates, b_gate.reshape(1, ng))


def _conv_kernel(cur_ref, prev_ref, next_ref, w_ref, b_ref, o_ref, *, n_lat, n_ctx, k_cols0, k_scale):
    i = pl.program_id(0)
    j = pl.program_id(1)
    tm, tc = cur_ref.shape
    hr = prev_ref.shape[0]
    cur = cur_ref[...].astype(F32)
    row = lax.broadcasted_iota(jnp.int32, (tm, 1), 0)
    g = i * tm + row
    prev_row = prev_ref[hr - 1:hr, :].astype(F32)
    next_row = next_ref[0:1, :].astype(F32)
    before = jnp.where(row == 0, prev_row, pltpu.roll(cur, 1, 0))
    after = jnp.where(row == tm - 1, next_row, pltpu.roll(cur, tm - 1, 0))
    before = jnp.where(jnp.logical_or(g == 0, g == n_lat), 0.0, before)
    after = jnp.where(jnp.logical_or(g == n_lat - 1, g == n_lat + n_ctx - 1), 0.0, after)
    y = before * w_ref[0:1, :] + cur * w_ref[1:2, :] + after * w_ref[2:3, :] + b_ref[...]
    y = y * jax.nn.sigmoid(y)
    o_ref[...] = (y * jnp.where(j * tc >= k_cols0, k_scale, 1.0)).astype(BF16)


def _conv_silu(u, w_conv, b_conv, n_cols, n_lat, n_ctx):
    s_pad = u.shape[0]
    tm, tc, hr = ROW_TILE, 1024, 16
    rb = tm // hr
    last = s_pad // hr - 1
    return pl.pallas_call(
        functools.partial(_conv_kernel, n_lat=n_lat, n_ctx=n_ctx, k_cols0=n_cols // 2, k_scale=MLSTM_DIM ** -0.5),
        grid=(s_pad // tm, n_cols // tc),
        in_specs=[pl.BlockSpec((tm, tc), lambda i, j: (i, j)),
                  pl.BlockSpec((hr, tc), lambda i, j: (jnp.maximum(i * rb - 1, 0), j)),
                  pl.BlockSpec((hr, tc), lambda i, j: (jnp.minimum((i + 1) * rb, last), j)),
                  pl.BlockSpec((8, tc), lambda i, j: (0, j)),
                  pl.BlockSpec((1, tc), lambda i, j: (0, j))],
        out_specs=pl.BlockSpec((tm, tc), lambda i, j: (i, j)),
        out_shape=jax.ShapeDtypeStruct((s_pad, n_cols), BF16),
        compiler_params=_cparams(("parallel", "parallel")),
        name="conv_silu",
    )(u, u, u, jnp.pad(w_conv, ((0, 8 - w_conv.shape[0]), (0, 0))), b_conv.reshape(1, n_cols))


def _log_sigmoid(x):
    return -(jnp.maximum(-x, 0.0) + jnp.log1p(jnp.exp(-jnp.abs(x))))


def _mlstm_chains(chains):
    def each(fn, *lists):
        return [fn(*a) for a in zip(*lists)]

    L = chains[0]['q'].shape[0]
    t_i = lax.broadcasted_iota(jnp.int32, (L, L), 0)
    s_i = lax.broadcasted_iota(jnp.int32, (L, L), 1)
    seen = [(s_i >= t_i) if ch['reverse'] else (s_i <= t_i) for ch in chains]
    q = [ch['q'] for ch in chains]
    k = [ch['k'] for ch in chains]
    v = [ch['v'] for ch in chains]
    ig_c = [ch['ig_c'] for ch in chains]
    ig_r = [ch['ig_r'] for ch in chains]
    lf_c = [_log_sigmoid(ch['fg_c']) for ch in chains]
    lf_r = [_log_sigmoid(ch['fg_r']) for ch in chains]
    m_prev = [ch['m'][...] for ch in chains]
    c_prev = [ch['c'][...] for ch in chains]
    n_prev = [ch['n'][...] for ch in chains]
    qk = each(_dot_nt, q, k)
    qc = each(lambda a, c: _dot_nt(a, c.astype(BF16)), q, c_prev)
    b_c = each(lambda sn, r: jnp.sum(jnp.where(sn, 1.0, 0.0) * r, axis=1, keepdims=True), seen, lf_r)
    b_r = each(lambda sn, c, r: jnp.sum(jnp.where(sn, 0.0, 1.0) * c, axis=0, keepdims=True) + r, seen, lf_c, lf_r)
    log_d = each(lambda sn, bc, br, g: jnp.where(sn, bc - br + g, -jnp.inf), seen, b_c, b_r, ig_r)
    log_inter = each(lambda bc, m: bc + m, b_c, m_prev)
    m_t = each(lambda li, ld: jnp.maximum(li, jnp.max(ld, axis=1, keepdims=True)), log_inter, log_d)
    w_intra = each(lambda ld, mt, s: jnp.exp(ld - mt) * s, log_d, m_t, qk)
    w_inter = each(lambda li, mt: jnp.exp(li - mt), log_inter, m_t)
    wiv = each(lambda w, vv: _dot(w.astype(BF16), vv), w_intra, v)
    num = each(lambda wi, a, b: wi * a + b, w_inter, qc, wiv)
    den = each(lambda wi, a, n, w: wi * jnp.sum(a.astype(F32) * n, axis=1, keepdims=True)
               + jnp.sum(w, axis=1, keepdims=True), w_inter, q, n_prev, w_intra)
    for ch, nu, de, mt in zip(chains, num, den, m_t):
        ch['o'][...] = (nu / jnp.maximum(jnp.abs(de), jnp.exp(-mt))).astype(ch['o'].dtype)
    b_end = each(lambda c: jnp.sum(c, axis=0, keepdims=True), lf_c)
    log_w = each(lambda be, bc, g: be - bc + g, b_end, b_c, ig_c)
    m_new = each(lambda be, m, lw: jnp.maximum(be + m, jnp.max(lw, axis=0, keepdims=True)), b_end, m_prev, log_w)
    w = each(lambda lw, mn: jnp.exp(lw - mn), log_w, m_new)
    decay = each(lambda be, m, mn: jnp.exp(be + m - mn), b_end, m_prev, m_new)
    upd = each(lambda ww, vv, kk: lax.dot_general((ww * vv.astype(F32)).astype(BF16), kk, (((0,), (0,)), ((), ())),
                                                  preferred_element_type=F32), w, v, k)
    for ch, dc, cp, u_, np_, ww, kk, mn in zip(chains, decay, c_prev, upd, n_prev, w, k, m_new):
        ch['c'][...] = dc * cp + u_
        ch['n'][...] = dc * np_ + jnp.sum(ww * kk.astype(F32), axis=0, keepdims=True)
        ch['m'][...] = mn


def _mlstm_kernel(qf_ref, kf_ref, vf_ref, gcf_ref, grf_ref, qb_ref, kb_ref, vb_ref, gcb_ref, grb_ref,
                  hf_ref, hb_ref, c_scr, n_scr, m_scr):
    @pl.when(pl.program_id(1) == 0)
    def _():
        c_scr[...] = jnp.zeros(c_scr.shape, F32)
        n_scr[...] = jnp.zeros(n_scr.shape, F32)
        m_scr[...] = jnp.zeros(m_scr.shape, F32)

    dh = MLSTM_DIM
    chains = []
    for i in range(gcf_ref.shape[0]):
        cols = slice(i * dh, (i + 1) * dh)
        for j, (q_ref, k_ref, v_ref, gc_ref, gr_ref, o_ref) in enumerate(
                ((qf_ref, kf_ref, vf_ref, gcf_ref, grf_ref, hf_ref), (qb_ref, kb_ref, vb_ref, gcb_ref, grb_ref, hb_ref))):
            gc = gc_ref[i]
            gr = gr_ref[i]
            s = 2 * i + j
            chains.append(dict(q=q_ref[:, cols], k=k_ref[:, cols], v=v_ref[:, cols],
                               ig_c=gc[:, 2 * j:2 * j + 1], fg_c=gc[:, 2 * j + 1:2 * j + 2],
                               ig_r=gr[2 * j:2 * j + 1, :], fg_r=gr[2 * j + 1:2 * j + 2, :],
                               c=c_scr.at[s], n=n_scr.at[s], m=m_scr.at[s], o=o_ref.at[:, cols], reverse=j == 1))
    _mlstm_chains(chains)


def _mlstm_scan(qk, u, gates, n_lat, n_ctx):
    s_pad = qk.shape[0]
    L, dh, nh, hp = MLSTM_CHUNK, MLSTM_DIM, MLSTM_HEADS, MLSTM_HEADS_PER_STEP
    nlc, ncc = n_lat // L, n_ctx // L
    steps = nlc + ncc
    ng = nh // hp
    g4 = gates.reshape(s_pad, 2, 2, nh).transpose(3, 0, 1, 2).reshape(nh, s_pad, 4)
    g4t = g4.transpose(0, 2, 1)

    def fwd(c):
        return jnp.where(c < ncc, nlc + c, c - ncc)

    def bwd(c):
        return jnp.where(c < ncc, nlc + ncc - 1 - c, nlc - 1 - (c - ncc))

    def specs(pos):
        return [pl.BlockSpec((L, hp * dh), lambda h, c: (pos(c), h)),
                pl.BlockSpec((L, hp * dh), lambda h, c: (pos(c), ng + h)),
                pl.BlockSpec((L, hp * dh), lambda h, c: (pos(c), 2 * ng + h)),
                pl.BlockSpec((hp, L, 4), lambda h, c: (h, pos(c), 0)),
                pl.BlockSpec((hp, 4, L), lambda h, c: (h, 0, pos(c)))]

    out_shape = jax.ShapeDtypeStruct((steps * L, nh * dh), BF16)
    return pl.pallas_call(
        _mlstm_kernel,
        grid=(ng, steps),
        in_specs=specs(fwd) + specs(bwd),
        out_specs=[pl.BlockSpec((L, hp * dh), lambda h, c: (fwd(c), h)),
                   pl.BlockSpec((L, hp * dh), lambda h, c: (bwd(c), h))],
        out_shape=[out_shape, out_shape],
        scratch_shapes=[pltpu.VMEM((2 * hp, dh, dh), F32), pltpu.VMEM((2 * hp, 1, dh), F32),
                        pltpu.VMEM((2 * hp, 1, 1), F32)],
        compiler_params=_cparams(("parallel", "arbitrary")),
        name="mlstm_scan",
    )(qk, qk, u, g4, g4t, qk, qk, u, g4, g4t)


def kernel(x, c, ctx, c_ctx, w_mod_0, b_mod_0, g_pre_mix_0, g_post_mix_0, w_in_0, lambda_q1_0, lambda_k1_0, lambda_q2_0, lambda_k2_0, g_subln_0, g_q_0, g_k_0, w_out_0, g_pre_ffn_0, g_post_ffn_0, w_router_0, w_gate_0, w_up_0, w_down_0, w_mod_1, b_mod_1, g_pre_mix_1, g_post_mix_1, w_in_1, b_gate_1, w_conv_1, b_conv_1, g_head_1, w_out_1, g_pre_ffn_1, g_post_ffn_1, w_router_1, w_gate_1, w_up_1, w_down_1):
    b, n_lat, d = x.shape
    n_ctx = ctx.shape[1]
    assert b == 1 and n_lat % ROW_TILE == 0 and n_ctx % MLSTM_CHUNK == 0 and n_ctx <= ROW_TILE
    s_valid = n_lat + n_ctx
    s_pad = n_lat + ROW_TILE
    n_lat_tiles = n_lat // ROW_TILE

    ctx_pad = jnp.concatenate([ctx[0], jnp.zeros((s_pad - s_valid, d), F32)], axis=0)
    cc = jnp.concatenate([c, c_ctx[None, :], jnp.zeros((6, d), F32)], axis=0)

    mod0 = _modulation(cc, w_mod_0, b_mod_0)
    tabs = _rope_tables(n_lat, n_ctx, s_pad)
    p0 = _proj0(x[0], ctx_pad, mod0, g_pre_mix_0, w_in_0.astype(BF16), tabs, g_q_0, g_k_0, n_lat_tiles)
    lam_init = 0.8 - 0.6 * math.exp(-0.3 * 0)
    lam4 = jnp.stack([lambda_q1_0, lambda_k1_0, lambda_q2_0, lambda_k2_0])
    o_a = _attn_diff(p0, lam4, g_subln_0, n_lat_tiles, s_valid, lam_init)
    o_b = _attn_gqa(p0, n_lat, s_valid)
    ka = o_a.shape[1]
    x_mid, hl, aff = _mixer_out(
        _out0_kernel, (o_a, o_b, ctx_pad),
        lambda tm: [pl.BlockSpec((tm, ka), lambda i: (i, 0)), pl.BlockSpec((tm, o_b.shape[1]), lambda i: (i, 0)),
                    pl.BlockSpec((tm, d), lambda i: (jnp.maximum(i - n_lat // tm, 0), 0))],
        x[0], mod0, w_out_0.astype(BF16), g_post_mix_0, g_pre_ffn_0, w_router_0, s_pad, n_lat // 256,
        lambda tm: [], "attn_out")
    p_moe0 = dict(w_gate=w_gate_0, w_up=w_up_0, w_down=w_down_0, g_post_ffn=g_post_ffn_0)
    xs = _moe(hl, aff, x_mid, mod0, p_moe0, n_lat, n_ctx, s_pad, True)

    mod1 = _modulation(cc, w_mod_1, b_mod_1)
    inner = MLSTM_HEADS * MLSTM_DIM
    u, gates = _proj1(xs, mod1, g_pre_mix_1, w_in_1.astype(BF16), 4 * inner, w_in_1[:, 4 * inner:], b_gate_1,
                      n_lat_tiles)
    qk = _conv_silu(u, w_conv_1, b_conv_1, 2 * inner, n_lat, n_ctx)
    hf, hb = _mlstm_scan(qk, u, gates, n_lat, n_ctx)
    og_blk = 3 * inner // inner
    x_mid, hl, aff = _mixer_out(
        _out1_kernel, (hf, hb, u, g_head_1.reshape(1, inner)),
        lambda tm: [pl.BlockSpec((tm, inner), lambda i: (i, 0)), pl.BlockSpec((tm, inner), lambda i: (i, 0)),
                    pl.BlockSpec((tm, inner), lambda i: (i, og_blk)), pl.BlockSpec((1, inner), lambda i: (0, 0))],
        xs, mod1, w_out_1.astype(BF16), g_post_mix_1, g_pre_ffn_1, w_router_1, n_lat, n_lat // 256,
        lambda tm: [pltpu.VMEM((tm, inner), BF16)], "mlstm_out")
    p_moe1 = dict(w_gate=w_gate_1, w_up=w_up_1, w_down=w_down_1, g_post_ffn=g_post_ffn_1)
    out = _moe(hl, aff, x_mid, mod1, p_moe1, n_lat, n_ctx, s_pad, False)
    return out[None]
```

```python
import functools
import math

import jax
import jax.numpy as jnp
from jax import lax
from jax.experimental import pallas as pl
from jax.experimental.pallas import tpu as pltpu

F32 = jnp.float32
BF16 = jnp.bfloat16
HIGHEST = lax.Precision.HIGHEST

GRID_W = 64
ROPE_THETA = 10000.0
RMS_EPS = 1e-6
N_MOD = 6
DIFF_HEADS = 8
DIFF_DIM = 64
GQA_HEADS = 8
GQA_KV_HEADS = 2
GQA_DIM = 128
MLSTM_HEADS = 8
MLSTM_DIM = 256
MLSTM_CHUNK = 128
N_EXPERTS = 16
EC_CAPACITY_FACTOR = 2

LANES = 128
ROW_TILE = 512
MLSTM_HEADS_PER_STEP = 4
FFN_MAX_TILE_ROWS = 1024
FFN_STEPS = 8
ATTN_TK = 512
ATTN_ROW_GROUP = 256
LOG2_E = math.log2(math.e)
TOPK_BLOCKS = 128
VMEM_LIMIT = 56 * 1024 * 1024


def _cparams(sem):
    return pltpu.CompilerParams(dimension_semantics=sem, vmem_limit_bytes=VMEM_LIMIT)


def _rms(x, g):
    return x * lax.rsqrt(jnp.mean(x * x, axis=-1, keepdims=True) + RMS_EPS) * g


def _dot(a, b):
    return jnp.dot(a, b, preferred_element_type=F32)


def _dot_nt(a, b, precision=None):
    return lax.dot_general(a, b, (((1,), (1,)), ((), ())), precision=precision, preferred_element_type=F32)


def _mod_kernel(c_ref, w_ref, b_ref, o_ref):
    c = c_ref[...]
    s = c * jax.nn.sigmoid(c)
    o_ref[...] = lax.dot_general(s, w_ref[...], (((1,), (0,)), ((), ())), precision=HIGHEST,
                                 preferred_element_type=F32) + b_ref[...]


def _modulation(cc, w_mod, b_mod):
    d, n = w_mod.shape
    tn = 1024
    return pl.pallas_call(
        _mod_kernel,
        grid=(n // tn,),
        in_specs=[pl.BlockSpec((8, d), lambda j: (0, 0)),
                  pl.BlockSpec((d, tn), lambda j: (0, j)),
                  pl.BlockSpec((1, tn), lambda j: (0, j))],
        out_specs=pl.BlockSpec((8, tn), lambda j: (0, j)),
        out_shape=jax.ShapeDtypeStruct((8, n), F32),
        compiler_params=_cparams(("arbitrary",)),
        name="modulation",
    )(cc, w_mod, b_mod.reshape(1, n))


def _mod_rows(mod_ref, r, k, d):
    return mod_ref[pl.ds(r, 1), k * d:(k + 1) * d]


def _premix(x, mod_ref, g_ref, r, d):
    return _rms(x, g_ref[...]) * (1.0 + _mod_rows(mod_ref, r, 1, d)) + _mod_rows(mod_ref, r, 0, d)


def _dot_hilo(a, b):
    a_hi = a.astype(BF16)
    a_lo = (a - a_hi.astype(F32)).astype(BF16)
    b_hi = b.astype(BF16)
    b_lo = (b - b_hi.astype(F32)).astype(BF16)
    return _dot(a_hi, b_hi) + (_dot(a_hi, b_lo) + _dot(a_lo, b_hi))


def _rope_a(x, ca, sa, sb):
    return x * ca + pltpu.roll(x, LANES - DIFF_DIM // 2, 1) * sa + pltpu.roll(x, DIFF_DIM // 2, 1) * sb


def _rope_b(x, cb, sb):
    return x * cb + pltpu.roll(x, GQA_DIM // 2, 1) * sb


def _proj0_kernel(x_ref, c_ref, mod_ref, gpre_ref, w_ref, ca_ref, saa_ref, sab_ref, cb_ref, sbb_ref, gq_ref, gk_ref,
                  o_ref, *, n_lat_tiles, d, tn):
    is_ctx = pl.program_id(0) >= n_lat_tiles
    x = jnp.where(is_ctx, c_ref[...], x_ref[...])
    h = _premix(x, mod_ref, gpre_ref, jnp.where(is_ctx, 1, 0), d).astype(BF16)
    n_qa = DIFF_HEADS
    n_qb = GQA_HEADS * GQA_DIM // LANES
    n_kb = GQA_KV_HEADS * GQA_DIM // LANES

    def epilogue(a, gi):
        if gi < 2 * n_qa:
            a = _rope_a(a, ca_ref[...], saa_ref[...], sab_ref[...])
            return a * (DIFF_DIM ** -0.5 * LOG2_E) if gi < n_qa else a
        if gi < 3 * n_qa:
            return a
        if gi < 3 * n_qa + n_qb:
            return _rope_b(_rms(a, gq_ref[...]), cb_ref[...], sbb_ref[...]) * (GQA_DIM ** -0.5 * LOG2_E)
        if gi < 3 * n_qa + n_qb + n_kb:
            return _rope_b(_rms(a, gk_ref[...]), cb_ref[...], sbb_ref[...])
        return a

    for j in range(w_ref.shape[1] // tn):
        acc = _dot(h, w_ref[:, j * tn:(j + 1) * tn])
        for g in range(tn // LANES):
            gi = j * (tn // LANES) + g
            o_ref[:, gi * LANES:(gi + 1) * LANES] = epilogue(acc[:, g * LANES:(g + 1) * LANES], gi).astype(BF16)


def _proj0(x, ctx_pad, mod, g_pre, w_in, tabs, g_q, g_k, n_lat_tiles):
    d = x.shape[1]
    s_pad = x.shape[0] + ctx_pad.shape[0]
    n_out = w_in.shape[1]
    tm, tn = ROW_TILE, 512
    row = lambda i: (i, 0)
    const = lambda i: (0, 0)
    once = pl.Buffered(1)
    tab_spec = pl.BlockSpec((tm, LANES), row)
    return pl.pallas_call(
        functools.partial(_proj0_kernel, n_lat_tiles=n_lat_tiles, d=d, tn=tn),
        grid=(s_pad // tm,),
        in_specs=[pl.BlockSpec((tm, d), lambda i: (jnp.minimum(i, n_lat_tiles - 1), 0)),
                  pl.BlockSpec((tm, d), const, pipeline_mode=once),
                  pl.BlockSpec((8, N_MOD * d), const, pipeline_mode=once),
                  pl.BlockSpec((1, d), const),
                  pl.BlockSpec((d, n_out), const, pipeline_mode=once),
                  tab_spec, tab_spec, tab_spec, tab_spec, tab_spec,
                  pl.BlockSpec((1, LANES), const),
                  pl.BlockSpec((1, LANES), const)],
        out_specs=pl.BlockSpec((tm, n_out), row),
        out_shape=jax.ShapeDtypeStruct((s_pad, n_out), BF16),
        compiler_params=_cparams(("parallel",)),
        name="proj0",
    )(x, ctx_pad, mod, g_pre.reshape(1, d), w_in, *tabs, g_q.reshape(1, LANES), g_k.reshape(1, LANES))


def _rope_tables(n_lat, n_ctx, s_pad):
    rows = jnp.repeat(jnp.arange(n_lat // GRID_W, dtype=F32), GRID_W)
    cols = jnp.tile(jnp.arange(GRID_W, dtype=F32), n_lat // GRID_W)

    def angles(dim):
        axis_dim = dim // 2
        inv_freq = ROPE_THETA ** (-jnp.arange(0, axis_dim, 2, dtype=F32) / axis_dim)
        ang = jnp.concatenate([rows[:, None] * inv_freq, cols[:, None] * inv_freq], axis=-1)
        return jnp.cos(ang), jnp.sin(ang)

    def pad(t, fill):
        return jnp.concatenate([t, jnp.full((s_pad - n_lat, LANES), fill, F32)], axis=0)

    cos_a, sin_a = angles(DIFF_DIM)
    zero = jnp.zeros_like(sin_a)
    ca = jnp.tile(cos_a, (1, 4))
    saa = jnp.tile(jnp.concatenate([-sin_a, zero], axis=1), (1, 2))
    sab = jnp.tile(jnp.concatenate([zero, sin_a], axis=1), (1, 2))
    cos_b, sin_b = angles(GQA_DIM)
    cb = jnp.tile(cos_b, (1, 2))
    sbb = jnp.concatenate([-sin_b, sin_b], axis=1)
    return pad(ca, 1.0), pad(saa, 0.0), pad(sab, 0.0), pad(cb, 1.0), pad(sbb, 0.0)


def _flash_rows(q, k_ref, v_ref, m_scr, acc_scr, s_scr, is_ctx_q, tk, s_valid):
    n_chunks = k_ref.shape[0] // tk
    last = n_chunks - 1
    groups = [slice(g * ATTN_ROW_GROUP, (g + 1) * ATTN_ROW_GROUP) for g in range(q.shape[0] // ATTN_ROW_GROUP)]
    m_scr[...] = jnp.full(m_scr.shape, -jnp.inf, F32)
    acc_scr[...] = jnp.zeros(acc_scr.shape, F32)

    def rows_of(c):
        return pl.ds(c * tk if isinstance(c, int) else pl.multiple_of(c * tk, tk), tk)

    def scores(c, buf, rows):
        s_scr[buf, rows] = _dot_nt(q[rows], k_ref[rows_of(c), :])

    def consume(c, buf, rows, v1, masked):
        s = s_scr[buf, rows]
        if masked:
            kpos = c * tk + lax.broadcasted_iota(jnp.int32, (1, tk), 1)
            s = jnp.where(kpos < s_valid, s, -jnp.inf)
        m_prev = m_scr[rows]
        m_new = jnp.maximum(m_prev, jnp.max(s, axis=1, keepdims=True))
        p = jnp.exp2(s - m_new).astype(BF16)
        acc_scr[rows] = jnp.exp2(m_prev - m_new) * acc_scr[rows] + _dot(p, v1)
        m_scr[rows] = m_new

    def values(c):
        vc = v_ref[rows_of(c), :]
        return jnp.concatenate([vc, jnp.ones_like(vc)], axis=1)

    unroll = max(u for u in (2, 4, 8) if last % u == 0)
    lo = jnp.where(is_ctx_q, last, 0)
    for rows in groups:
        scores(lo, 0, rows)

    def body(i, carry):
        c = lo + unroll * i
        for j in range(unroll):
            v1 = values(c + j)
            for rows in groups:
                scores(c + j + 1, (j + 1) % 2, rows)
                consume(c + j, j % 2, rows, v1, False)
        return carry

    lax.fori_loop(0, (last - lo) // unroll, body, 0)
    v1 = values(last)
    for rows in groups:
        consume(last, 0, rows, v1, True)


def _attn_diff_kernel(q_ref, k_ref, v_ref, lam_ref, gsub_ref, o_ref, m_scr, acc_scr, s_scr,
                      *, n_lat_tiles, tk, s_valid, lam_init):
    q = q_ref[...]
    tq, hd = q.shape
    first = lax.broadcasted_iota(jnp.int32, (1, hd), 1) < DIFF_DIM
    zero = jnp.zeros_like(q)
    qq = jnp.concatenate([jnp.where(first, q, zero), jnp.where(first, zero, q)], axis=0)
    _flash_rows(qq, k_ref, v_ref, m_scr, acc_scr, s_scr, pl.program_id(1) >= n_lat_tiles, tk, s_valid)
    lam = (jnp.exp(jnp.sum(lam_ref[0:1, :] * lam_ref[1:2, :], axis=1, keepdims=True))
           - jnp.exp(jnp.sum(lam_ref[2:3, :] * lam_ref[3:4, :], axis=1, keepdims=True)) + lam_init)
    o = (acc_scr[0:tq, 0:hd] / acc_scr[0:tq, hd:hd + 1]
         - lam * (acc_scr[tq:2 * tq, 0:hd] / acc_scr[tq:2 * tq, hd:hd + 1]))
    o_ref[...] = (_rms(o, gsub_ref[...]) * (1.0 - lam_init)).astype(BF16)


def _attn_diff(p0, lam4, g_subln, n_lat_tiles, s_valid, lam_init):
    s_pad = p0.shape[0]
    tq = ROW_TILE
    hd = 2 * DIFF_DIM
    k0 = DIFF_HEADS
    v0 = 2 * DIFF_HEADS
    return pl.pallas_call(
        functools.partial(_attn_diff_kernel, n_lat_tiles=n_lat_tiles, tk=ATTN_TK, s_valid=s_valid, lam_init=lam_init),
        grid=(DIFF_HEADS, s_pad // tq),
        in_specs=[pl.BlockSpec((tq, hd), lambda h, iq: (iq, h)),
                  pl.BlockSpec((s_pad, hd), lambda h, iq: (0, k0 + h)),
                  pl.BlockSpec((s_pad, hd), lambda h, iq: (0, v0 + h)),
                  pl.BlockSpec((4, DIFF_DIM), lambda h, iq: (0, 0)),
                  pl.BlockSpec((1, hd), lambda h, iq: (0, 0))],
        out_specs=pl.BlockSpec((tq, hd), lambda h, iq: (iq, h)),
        out_shape=jax.ShapeDtypeStruct((s_pad, DIFF_HEADS * hd), BF16),
        scratch_shapes=[pltpu.VMEM((2 * tq, 1), F32), pltpu.VMEM((2 * tq, 2 * hd), F32),
                        pltpu.VMEM((2, 2 * tq, ATTN_TK), F32)],
        compiler_params=_cparams(("parallel", "arbitrary")),
        name="attn_diff",
    )(p0, p0, p0, lam4, g_subln.reshape(1, hd))


def _attn_gqa_kernel(q_ref, k_ref, v_ref, o_ref, m_scr, acc_scr, s_scr, *, n_lat_tiles, tk, s_valid, group):
    tq = q_ref.shape[0]
    dh = GQA_DIM
    qq = jnp.concatenate([q_ref[:, g * dh:(g + 1) * dh] for g in range(group)], axis=0)
    _flash_rows(qq, k_ref, v_ref, m_scr, acc_scr, s_scr, pl.program_id(1) >= n_lat_tiles, tk, s_valid)
    for g in range(group):
        rows = slice(g * tq, (g + 1) * tq)
        o_ref[:, g * dh:(g + 1) * dh] = (acc_scr[rows, 0:dh] / acc_scr[rows, dh:dh + 1]).astype(BF16)


def _attn_gqa(p0, n_lat, s_valid):
    s_pad = p0.shape[0]
    tq = ROW_TILE
    group = GQA_HEADS // GQA_KV_HEADS
    qw = group * GQA_DIM
    q0 = 3 * DIFF_HEADS * 2 * DIFF_DIM // qw
    k0 = (3 * DIFF_HEADS * 2 * DIFF_DIM + GQA_HEADS * GQA_DIM) // GQA_DIM
    v0 = k0 + GQA_KV_HEADS
    return pl.pallas_call(
        functools.partial(_attn_gqa_kernel, n_lat_tiles=n_lat // tq, tk=ATTN_TK, s_valid=s_valid, group=group),
        grid=(GQA_KV_HEADS, s_pad // tq),
        in_specs=[pl.BlockSpec((tq, qw), lambda h, iq: (iq, q0 + h)),
                  pl.BlockSpec((s_pad, GQA_DIM), lambda h, iq: (0, k0 + h)),
                  pl.BlockSpec((s_pad, GQA_DIM), lambda h, iq: (0, v0 + h))],
        out_specs=pl.BlockSpec((tq, qw), lambda h, iq: (iq, h)),
        out_shape=jax.ShapeDtypeStruct((s_pad, GQA_HEADS * GQA_DIM), BF16),
        scratch_shapes=[pltpu.VMEM((group * tq, 1), F32), pltpu.VMEM((group * tq, 2 * GQA_DIM), F32),
                        pltpu.VMEM((2, group * tq, ATTN_TK), F32)],
        compiler_params=_cparams(("parallel", "arbitrary")),
        name="attn_gqa",
    )(p0, p0, p0)


def _post_mix(y, x, mod_ref, r, gpost_ref, gffn_ref, wr_ref, xo_ref, hl_ref, aff_ref, d):
    xn = x + _mod_rows(mod_ref, r, 2, d) * _rms(y, gpost_ref[...])
    xo_ref[...] = xn
    hl = _rms(xn, gffn_ref[...]) * (1.0 + _mod_rows(mod_ref, r, 4, d)) + _mod_rows(mod_ref, r, 3, d)
    hl_ref[...] = hl
    logits = _dot_hilo(hl, wr_ref[...])
    e = jnp.exp(logits - jnp.max(logits, axis=1, keepdims=True))
    aff_ref[...] = e / jnp.sum(e, axis=1, keepdims=True)


def _out0_kernel(oa_ref, ob_ref, c_ref, x_ref, mod_ref, w_ref, gpost_ref, gffn_ref, wr_ref, xo_ref, hl_ref, aff_ref,
                 *, n_lat_tiles, d):
    is_ctx = pl.program_id(0) >= n_lat_tiles
    r = jnp.where(is_ctx, 1, 0)
    ka = oa_ref.shape[1]
    y = _dot(oa_ref[...], w_ref[0:ka, :]) + _dot(ob_ref[...], w_ref[ka:, :])
    x = jnp.where(is_ctx, c_ref[...], x_ref[...])
    _post_mix(y, x, mod_ref, r, gpost_ref, gffn_ref, wr_ref, xo_ref, hl_ref, aff_ref, d)


def _out1_kernel(hf_ref, hb_ref, og_ref, ghead_ref, x_ref, mod_ref, w_ref, gpost_ref, gffn_ref, wr_ref,
                 xo_ref, hl_ref, aff_ref, y_scr, *, n_lat_tiles, d):
    r = jnp.where(pl.program_id(0) >= n_lat_tiles, 1, 0)
    for h in range(MLSTM_HEADS):
        sl = slice(h * MLSTM_DIM, (h + 1) * MLSTM_DIM)
        hs = hf_ref[:, sl].astype(F32) + hb_ref[:, sl].astype(F32)
        hn = _rms(hs, ghead_ref[:, sl])
        y_scr[:, sl] = (jax.nn.sigmoid(og_ref[:, sl].astype(F32)) * hn).astype(BF16)
    y = _dot(y_scr[...], w_ref[...])
    _post_mix(y, x_ref[...], mod_ref, r, gpost_ref, gffn_ref, wr_ref, xo_ref, hl_ref, aff_ref, d)


def _mixer_out(kernel_fn, mix_inputs, mix_specs, x, mod, w_out, g_post, g_ffn, w_router, n_rows, n_lat_tiles_tm,
               scratch, name):
    d = x.shape[1]
    tm = 256
    row = lambda i: (i, 0)
    x_row = lambda i: (jnp.minimum(i, x.shape[0] // tm - 1), 0)
    const = lambda i: (0, 0)
    ne = w_router.shape[1]
    return pl.pallas_call(
        functools.partial(kernel_fn, n_lat_tiles=n_lat_tiles_tm, d=d),
        grid=(n_rows // tm,),
        in_specs=mix_specs(tm) + [pl.BlockSpec((tm, d), x_row),
                                  pl.BlockSpec((8, N_MOD * d), const),
                                  pl.BlockSpec(w_out.shape, const),
                                  pl.BlockSpec((1, d), const),
                                  pl.BlockSpec((1, d), const),
                                  pl.BlockSpec((d, ne), const)],
        out_specs=[pl.BlockSpec((tm, d), row), pl.BlockSpec((tm, d), row), pl.BlockSpec((tm, ne), row)],
        out_shape=[jax.ShapeDtypeStruct((n_rows, d), F32), jax.ShapeDtypeStruct((n_rows, d), F32),
                   jax.ShapeDtypeStruct((n_rows, ne), F32)],
        scratch_shapes=scratch(tm),
        compiler_params=_cparams(("parallel",)),
        name=name,
    )(*mix_inputs, x, mod, w_out, g_post.reshape(1, d), g_ffn.reshape(1, d), w_router)


def _incl_cumsum(maskf, upper, strict_lower):
    mb = maskf.astype(BF16)
    local = _dot(mb, upper)
    start = jnp.sum(_dot(strict_lower, mb), axis=1, keepdims=True)
    return local, start


def _topk_kernel(aff_ref, idx_ref, gate_ref, dest_ref, cnt_ref, cnt_acc, *, cap, dest_stride, row_off):
    e = pl.program_id(0)
    ne = pl.num_programs(0)
    nb = aff_ref.shape[1]
    a = aff_ref[0]
    bits = pltpu.bitcast(a, jnp.int32)

    thr = jnp.zeros((1, 1), jnp.int32)
    for bit in range(30, -1, -1):
        cand = thr | (1 << bit)
        c = jnp.sum(jnp.sum(jnp.where(bits >= cand, 1.0, 0.0), axis=1, keepdims=True), axis=0, keepdims=True)
        thr = jnp.where(c >= cap, cand, thr)

    r_i = lax.broadcasted_iota(jnp.int32, (LANES, LANES), 0)
    c_i = lax.broadcasted_iota(jnp.int32, (LANES, LANES), 1)
    upper = jnp.where(r_i <= c_i, 1.0, 0.0).astype(BF16)
    rb = lax.broadcasted_iota(jnp.int32, (nb, nb), 0)
    cb = lax.broadcasted_iota(jnp.int32, (nb, nb), 1)
    strict_lower = jnp.where(cb < rb, 1.0, 0.0).astype(BF16)

    gt = bits > thr
    eq = bits == thr
    n_gt = jnp.sum(jnp.sum(jnp.where(gt, 1.0, 0.0), axis=1, keepdims=True), axis=0, keepdims=True)
    eq_local, eq_start = _incl_cumsum(jnp.where(eq, 1.0, 0.0), upper, strict_lower)
    sel = jnp.logical_or(gt, jnp.logical_and(eq, eq_local + eq_start <= cap - n_gt))
    self_ = jnp.where(sel, 1.0, 0.0)
    sel_local, sel_start = _incl_cumsum(self_, upper, strict_lower)
    sel_tot = jnp.sum(self_, axis=1, keepdims=True)

    @pl.when(e == 0)
    def _():
        cnt_acc[...] = jnp.zeros(cnt_acc.shape, F32)

    earlier = cnt_acc[...]
    cnt_acc[...] = earlier + self_

    lane = lax.broadcasted_iota(jnp.int32, (nb, LANES), 1)
    cols = jnp.where(lane == 0, sel_start, jnp.where(lane == 1, sel_tot, 0.0))
    rows = cols.T
    start_row = rows[0:1, 0:nb]
    tot_row = rows[1:2, 0:nb]

    p = lax.broadcasted_iota(jnp.int32, (cap, 1), 0).astype(F32)
    onehot = jnp.where(jnp.logical_and(start_row <= p, p < start_row + tot_row), 1.0, 0.0)
    blk = lax.broadcasted_iota(jnp.int32, (1, nb), 1).astype(F32)
    p_start = jnp.sum(onehot * start_row, axis=1, keepdims=True)
    ohb = onehot.astype(BF16)
    local_rows = _dot(ohb, sel_local.astype(BF16))
    sel_rows = _dot(ohb, self_.astype(BF16))
    earlier_rows = _dot(ohb, earlier.astype(BF16))
    aff_rows = lax.dot_general(onehot, a, (((1,), (0,)), ((), ())), precision=HIGHEST, preferred_element_type=F32)
    hit = jnp.where(jnp.logical_and(local_rows == p - p_start + 1.0, sel_rows > 0.5), 1.0, 0.0)
    lane_f = lax.broadcasted_iota(jnp.int32, (1, LANES), 1).astype(F32)
    gate_ref[0] = jnp.sum(hit * aff_rows, axis=1, keepdims=True)
    hb = hit.astype(BF16)
    rows8 = lambda v: jnp.broadcast_to(v, (8, v.shape[1])).astype(BF16)
    tok = (_dot_nt(rows8(blk), ohb) * LANES + _dot_nt(rows8(lane_f), hb))[0:1, :]
    rank = _dot_nt(jnp.ones((8, LANES), BF16), (hit * earlier_rows).astype(BF16))[0:1, :]
    idx_ref[0] = tok.astype(jnp.int32) + row_off
    dest_ref[0] = (rank * dest_stride + tok).astype(jnp.int32) + row_off

    @pl.when(e == ne - 1)
    def _():
        cnt_ref[...] = cnt_acc[...].astype(jnp.int32)


def _route(aff_t, cap, dest_stride, row_off):
    ne, n = aff_t.shape
    n_pad = TOPK_BLOCKS * LANES
    assert n <= n_pad
    a = jnp.pad(aff_t, ((0, 0), (0, n_pad - n)), constant_values=-1.0).reshape(ne, TOPK_BLOCKS, LANES)
    slot = pl.BlockSpec((1, cap, 1), lambda e: (e, 0, 0))
    lane_slot = pl.BlockSpec((1, 1, cap), lambda e: (e, 0, 0))
    idx, gate, dest, cnt = pl.pallas_call(
        functools.partial(_topk_kernel, cap=cap, dest_stride=dest_stride, row_off=row_off),
        grid=(ne,),
        in_specs=[pl.BlockSpec((1, TOPK_BLOCKS, LANES), lambda e: (e, 0, 0))],
        out_specs=[lane_slot, slot, lane_slot, pl.BlockSpec((TOPK_BLOCKS, LANES), lambda e: (0, 0))],
        out_shape=[jax.ShapeDtypeStruct((ne, 1, cap), jnp.int32), jax.ShapeDtypeStruct((ne, cap, 1), F32),
                   jax.ShapeDtypeStruct((ne, 1, cap), jnp.int32), jax.ShapeDtypeStruct((TOPK_BLOCKS, LANES), jnp.int32)],
        scratch_shapes=[pltpu.VMEM((TOPK_BLOCKS, LANES), F32)],
        compiler_params=_cparams(("arbitrary",)),
        name="route_topk",
    )(a)
    return idx.reshape(ne * cap), gate, dest.reshape(ne * cap), cnt.reshape(n_pad)[:n]


def _pack_bf16_pairs(y):
    n = y.shape[1] // 2
    bits = lax.bitcast_convert_type(y.astype(BF16).astype(F32), jnp.uint32)
    return (bits[:, n:] & jnp.uint32(0xFFFF0000)) | (bits[:, :n] >> 16)


def _unpack_bf16_pairs(w):
    return (lax.bitcast_convert_type(w << 16, F32), lax.bitcast_convert_type(w & jnp.uint32(0xFFFF0000), F32))


def _row_copy(src, s, dst, t, sem):
    return pltpu.make_async_copy(src.at[pl.ds(s, 1)], dst.at[pl.ds(t, 1)], sem)


def _ffn_body(idx_sm, dest_sm, hl_hbm, gate_ref, wg_ref, wu_ref, wd_ref, z_hbm, xbuf, xb, acc, zp, gsem, ssem, *, tm):
    f = pl.program_id(2)
    nm = pl.num_programs(1)
    t = pl.program_id(0) * nm + pl.program_id(1)
    n_tiles = pl.num_programs(0) * nm
    cur = t % 2
    nxt = 1 - cur
    chunk = tm // FFN_STEPS
    last_step = f == pl.num_programs(2) - 1
    t_next = jnp.minimum(t + 1, n_tiles - 1)
    t_prev = jnp.maximum(t - 1, 0)

    def gather(tile, slot, r):
        return _row_copy(hl_hbm, idx_sm[tile * tm + r], xbuf.at[slot], r, gsem.at[slot])

    def scatter(tile, slot, r):
        return _row_copy(zp.at[slot], r, z_hbm, dest_sm[tile * tm + r], ssem.at[slot])

    def for_rows(fn):
        def body(r, c):
            fn(r)
            return c
        lax.fori_loop(0, tm, body, 0, unroll=8)

    @pl.when(jnp.logical_and(t == 0, f == 0))
    def _():
        for_rows(lambda r: gather(0, 0, r).start())
        zp[1] = jnp.zeros(zp.shape[1:], jnp.uint32)

    @pl.when(f == 0)
    def _():
        for_rows(lambda r: gather(t, cur, r).wait())
        xb[...] = xbuf[cur].astype(BF16)
        acc[...] = jnp.zeros(acc.shape, F32)

        @pl.when(t >= 1)
        def _():
            for_rows(lambda r: scatter(t_prev, cur, r).wait())

    for r in range(chunk):
        gather(t_next, nxt, f * chunk + r).start()
        scatter(t_prev, nxt, f * chunk + r).start()

    x = xb[...]
    g = _dot(x, wg_ref[0].astype(BF16))
    u = _dot(x, wu_ref[0].astype(BF16))
    hid = (g * jax.nn.sigmoid(g) * u).astype(BF16)
    acc[...] += _dot(hid, wd_ref[0].astype(BF16))

    @pl.when(last_step)
    def _():
        zp[cur] = _pack_bf16_pairs(acc[...] * gate_ref[0])

    @pl.when(jnp.logical_and(t == n_tiles - 1, last_step))
    def _():
        for_rows(lambda r: gather(t, nxt, r).wait())
        for_rows(lambda r: scatter(t_prev, nxt, r).wait())
        for_rows(lambda r: scatter(t, cur, r).start())
        for_rows(lambda r: scatter(t, cur, r).wait())


def _expert_ffn(idx, dest, hl, gate, w_gate, w_up, w_down, z_rows, n_tiles):
    ne, d, ff = w_gate.shape
    cap = gate.shape[1]
    tm = cap // n_tiles
    assert tm * n_tiles == cap and tm % FFN_STEPS == 0 and tm % 16 == 0
    tf = ff // FFN_STEPS
    any_spec = pl.BlockSpec(memory_space=pl.ANY)
    return pl.pallas_call(
        functools.partial(_ffn_body, tm=tm),
        grid_spec=pltpu.PrefetchScalarGridSpec(
            num_scalar_prefetch=2,
            grid=(ne, n_tiles, FFN_STEPS),
            in_specs=[any_spec,
                      pl.BlockSpec((1, tm, 1), lambda e, m, f, *_: (e, m, 0)),
                      pl.BlockSpec((1, d, tf), lambda e, m, f, *_: (e, 0, f)),
                      pl.BlockSpec((1, d, tf), lambda e, m, f, *_: (e, 0, f)),
                      pl.BlockSpec((1, tf, d), lambda e, m, f, *_: (e, f, 0))],
            out_specs=any_spec,
            scratch_shapes=[pltpu.VMEM((2, tm, d), F32), pltpu.VMEM((tm, d), BF16), pltpu.VMEM((tm, d), F32),
                            pltpu.VMEM((2, tm, d // 2), jnp.uint32),
                            pltpu.SemaphoreType.DMA((2,)), pltpu.SemaphoreType.DMA((2,))]),
        out_shape=jax.ShapeDtypeStruct((z_rows, d // 2), jnp.uint32),
        compiler_params=_cparams(("arbitrary", "arbitrary", "arbitrary")),
        name="expert_ffn",
    )(idx, dest, hl, gate, w_gate, w_up, w_down)


def _combine_kernel(maxc_sm, za_ref, zb_ref, cnt_ref, x_ref, mod_ref, g_ref, o_ref, acc, *, n_lat_tiles, d):
    i = pl.program_id(0)
    kk = pl.program_id(1)
    nk = pl.num_programs(1)

    @pl.when(kk == 0)
    def _():
        acc[...] = jnp.zeros(acc.shape, F32)

    def add_slab(z_ref, k):
        @pl.when(k < maxc_sm[i])
        def _():
            lo, hi = _unpack_bf16_pairs(z_ref[...])
            live = cnt_ref[...] > k
            acc[:, 0:d // 2] += jnp.where(live, lo, 0.0)
            acc[:, d // 2:d] += jnp.where(live, hi, 0.0)

    add_slab(za_ref, 2 * kk)
    add_slab(zb_ref, 2 * kk + 1)

    @pl.when(kk == nk - 1)
    def _():
        r = jnp.where(i >= n_lat_tiles, 1, 0)
        o_ref[...] = x_ref[...] + _mod_rows(mod_ref, r, 5, d) * _rms(acc[...], g_ref[...])


def _combine(z, cnt, x, mod, g_post, n_rows, z_stride, n_lat_tiles_tb):
    d = x.shape[1]
    tb = ROW_TILE
    nb = n_rows // tb
    zb = z_stride // tb
    maxc = jnp.max(cnt[:n_rows].reshape(nb, tb), axis=1)

    def z_map(j):
        return lambda i, kk, maxc_sm: (jnp.maximum(jnp.minimum(2 * kk + j, maxc_sm[i] - 1), 0) * zb + i, 0)

    return pl.pallas_call(
        functools.partial(_combine_kernel, n_lat_tiles=n_lat_tiles_tb, d=d),
        grid_spec=pltpu.PrefetchScalarGridSpec(
            num_scalar_prefetch=1,
            grid=(nb, N_EXPERTS // 2),
            in_specs=[pl.BlockSpec((tb, d // 2), z_map(0)),
                      pl.BlockSpec((tb, d // 2), z_map(1)),
                      pl.BlockSpec((tb, 1), lambda i, k, s: (i, 0)),
                      pl.BlockSpec((tb, d), lambda i, k, s: (i, 0)),
                      pl.BlockSpec((8, N_MOD * d), lambda i, k, s: (0, 0)),
                      pl.BlockSpec((1, d), lambda i, k, s: (0, 0))],
            out_specs=pl.BlockSpec((tb, d), lambda i, k, s: (i, 0)),
            scratch_shapes=[pltpu.VMEM((tb, d), F32)]),
        out_shape=jax.ShapeDtypeStruct((n_rows, d), F32),
        compiler_params=_cparams(("parallel", "arbitrary")),
        name="moe_combine",
    )(maxc, z, z, cnt.reshape(-1, 1), x, mod, g_post.reshape(1, d))


def _moe(hl, aff, x_mid, mod, p, n_lat, n_ctx, s_pad, with_ctx):
    aff_t = aff.T
    ne = N_EXPERTS
    cap_l = EC_CAPACITY_FACTOR * n_lat // ne
    n_tiles = max(1, cap_l // FFN_MAX_TILE_ROWS)
    idx, gate, dest, cnt = _route(aff_t[:, :n_lat], cap_l, s_pad, 0)
    n_rows = n_lat
    if with_ctx:
        cap_c = EC_CAPACITY_FACTOR * n_ctx // ne
        idx_c, gate_c, dest_c, cnt_c = _route(aff_t[:, n_lat:n_lat + n_ctx], cap_c, s_pad, n_lat)

        def per_tile(a, b):
            return jnp.concatenate([a.reshape(ne, n_tiles, cap_l // n_tiles), b.reshape(ne, n_tiles, cap_c // n_tiles)],
                                   axis=2).reshape(ne, cap_l + cap_c)
        idx = per_tile(idx, idx_c).reshape(-1)
        dest = per_tile(dest, dest_c).reshape(-1)
        gate = per_tile(gate, gate_c)[:, :, None]
        cnt = jnp.concatenate([cnt, cnt_c, jnp.zeros((s_pad - n_lat - n_ctx,), jnp.int32)])
        n_rows = s_pad
    z = _expert_ffn(idx, dest, hl, gate, p['w_gate'], p['w_up'], p['w_down'], ne * s_pad, n_tiles)
    return _combine(z, cnt, x_mid, mod, p['g_post_ffn'], n_rows, s_pad, n_lat // ROW_TILE)


def _proj1_kernel(x_ref, mod_ref, gpre_ref, w_ref, wg_ref, bg_ref, o_ref, g_ref, h_scr, *, n_lat_tiles, d):
    i = pl.program_id(0)
    j = pl.program_id(1)

    @pl.when(j == 0)
    def _():
        r = jnp.where(i >= n_lat_tiles, 1, 0)
        h = _premix(x_ref[...], mod_ref, gpre_ref, r, d)
        h_scr[...] = h.astype(BF16)
        g_ref[...] = _dot_hilo(h, wg_ref[...]) + bg_ref[...]

    o_ref[...] = _dot(h_scr[...], w_ref[...]).astype(BF16)


def _proj1(xs, mod, g_pre, w_main, n_out, w_gates, b_gate, n_lat_tiles):
    s_pad, d = xs.shape
    ng = w_gates.shape[1]
    tm, tn = ROW_TILE, 1024
    row = lambda i, j: (i, 0)
    const = lambda i, j: (0, 0)
    return pl.pallas_call(
        functools.partial(_proj1_kernel, n_lat_tiles=n_lat_tiles, d=d),
        grid=(s_pad // tm, n_out // tn),
        in_specs=[pl.BlockSpec((tm, d), row),
                  pl.BlockSpec((8, N_MOD * d), const),
                  pl.BlockSpec((1, d), const),
                  pl.BlockSpec((d, tn), lambda i, j: (0, j)),
                  pl.BlockSpec((d, ng), const),
                  pl.BlockSpec((1, ng), const)],
        out_specs=[pl.BlockSpec((tm, tn), lambda i, j: (i, j)), pl.BlockSpec((tm, ng), row)],
        out_shape=[jax.ShapeDtypeStruct((s_pad, n_out), BF16), jax.ShapeDtypeStruct((s_pad, ng), F32)],
        scratch_shapes=[pltpu.VMEM((tm, d), BF16)],
        compiler_params=_cparams(("parallel", "arbitrary")),
        name="proj1",
    )(xs, mod, g_pre.reshape(1, d), w_main, w_gates, b_gate.reshape(1, ng))


def _conv_kernel(cur_ref, prev_ref, next_ref, w_ref, b_ref, o_ref, *, n_lat, n_ctx, k_cols0, k_scale):
    i = pl.program_id(0)
    j = pl.program_id(1)
    tm, tc = cur_ref.shape
    hr = prev_ref.shape[0]
    cur = cur_ref[...].astype(F32)
    row = lax.broadcasted_iota(jnp.int32, (tm, 1), 0)
    g = i * tm + row
    prev_row = prev_ref[hr - 1:hr, :].astype(F32)
    next_row = next_ref[0:1, :].astype(F32)
    before = jnp.where(row == 0, prev_row, pltpu.roll(cur, 1, 0))
    after = jnp.where(row == tm - 1, next_row, pltpu.roll(cur, tm - 1, 0))
    before = jnp.where(jnp.logical_or(g == 0, g == n_lat), 0.0, before)
    after = jnp.where(jnp.logical_or(g == n_lat - 1, g == n_lat + n_ctx - 1), 0.0, after)
    y = before * w_ref[0:1, :] + cur * w_ref[1:2, :] + after * w_ref[2:3, :] + b_ref[...]
    y = y * jax.nn.sigmoid(y)
    o_ref[...] = (y * jnp.where(j * tc >= k_cols0, k_scale, 1.0)).astype(BF16)


def _conv_silu(u, w_conv, b_conv, n_cols, n_lat, n_ctx):
    s_pad = u.shape[0]
    tm, tc, hr = ROW_TILE, 1024, 16
    rb = tm // hr
    last = s_pad // hr - 1
    return pl.pallas_call(
        functools.partial(_conv_kernel, n_lat=n_lat, n_ctx=n_ctx, k_cols0=n_cols // 2, k_scale=MLSTM_DIM ** -0.5),
        grid=(s_pad // tm, n_cols // tc),
        in_specs=[pl.BlockSpec((tm, tc), lambda i, j: (i, j)),
                  pl.BlockSpec((hr, tc), lambda i, j: (jnp.maximum(i * rb - 1, 0), j)),
                  pl.BlockSpec((hr, tc), lambda i, j: (jnp.minimum((i + 1) * rb, last), j)),
                  pl.BlockSpec((8, tc), lambda i, j: (0, j)),
                  pl.BlockSpec((1, tc), lambda i, j: (0, j))],
        out_specs=pl.BlockSpec((tm, tc), lambda i, j: (i, j)),
        out_shape=jax.ShapeDtypeStruct((s_pad, n_cols), BF16),
        compiler_params=_cparams(("parallel", "parallel")),
        name="conv_silu",
    )(u, u, u, jnp.pad(w_conv, ((0, 8 - w_conv.shape[0]), (0, 0))), b_conv.reshape(1, n_cols))


def _log_sigmoid(x):
    return -(jnp.maximum(-x, 0.0) + jnp.log1p(jnp.exp(-jnp.abs(x))))


def _mlstm_chains(chains):
    def each(fn, *lists):
        return [fn(*a) for a in zip(*lists)]

    L = chains[0]['q'].shape[0]
    t_i = lax.broadcasted_iota(jnp.int32, (L, L), 0)
    s_i = lax.broadcasted_iota(jnp.int32, (L, L), 1)
    seen = [(s_i >= t_i) if ch['reverse'] else (s_i <= t_i) for ch in chains]
    q = [ch['q'] for ch in chains]
    k = [ch['k'] for ch in chains]
    v = [ch['v'] for ch in chains]
    ig_c = [ch['ig_c'] for ch in chains]
    ig_r = [ch['ig_r'] for ch in chains]
    lf_c = [_log_sigmoid(ch['fg_c']) for ch in chains]
    lf_r = [_log_sigmoid(ch['fg_r']) for ch in chains]
    m_prev = [ch['m'][...] for ch in chains]
    c_prev = [ch['c'][...] for ch in chains]
    n_prev = [ch['n'][...] for ch in chains]
    qk = each(_dot_nt, q, k)
    qc = each(lambda a, c: _dot_nt(a, c.astype(BF16)), q, c_prev)
    b_c = each(lambda sn, r: jnp.sum(jnp.where(sn, 1.0, 0.0) * r, axis=1, keepdims=True), seen, lf_r)
    b_r = each(lambda sn, c, r: jnp.sum(jnp.where(sn, 0.0, 1.0) * c, axis=0, keepdims=True) + r, seen, lf_c, lf_r)
    log_d = each(lambda sn, bc, br, g: jnp.where(sn, bc - br + g, -jnp.inf), seen, b_c, b_r, ig_r)
    log_inter = each(lambda bc, m: bc + m, b_c, m_prev)
    m_t = each(lambda li, ld: jnp.maximum(li, jnp.max(ld, axis=1, keepdims=True)), log_inter, log_d)
    w_intra = each(lambda ld, mt, s: jnp.exp(ld - mt) * s, log_d, m_t, qk)
    w_inter = each(lambda li, mt: jnp.exp(li - mt), log_inter, m_t)
    wiv = each(lambda w, vv: _dot(w.astype(BF16), vv), w_intra, v)
    num = each(lambda wi, a, b: wi * a + b, w_inter, qc, wiv)
    den = each(lambda wi, a, n, w: wi * jnp.sum(a.astype(F32) * n, axis=1, keepdims=True)
               + jnp.sum(w, axis=1, keepdims=True), w_inter, q, n_prev, w_intra)
    for ch, nu, de, mt in zip(chains, num, den, m_t):
        ch['o'][...] = (nu / jnp.maximum(jnp.abs(de), jnp.exp(-mt))).astype(ch['o'].dtype)
    b_end = each(lambda c: jnp.sum(c, axis=0, keepdims=True), lf_c)
    log_w = each(lambda be, bc, g: be - bc + g, b_end, b_c, ig_c)
    m_new = each(lambda be, m, lw: jnp.maximum(be + m, jnp.max(lw, axis=0, keepdims=True)), b_end, m_prev, log_w)
    w = each(lambda lw, mn: jnp.exp(lw - mn), log_w, m_new)
    decay = each(lambda be, m, mn: jnp.exp(be + m - mn), b_end, m_prev, m_new)
    upd = each(lambda ww, vv, kk: lax.dot_general((ww * vv.astype(F32)).astype(BF16), kk, (((0,), (0,)), ((), ())),
                                                  preferred_element_type=F32), w, v, k)
    for ch, dc, cp, u_, np_, ww, kk, mn in zip(chains, decay, c_prev, upd, n_prev, w, k, m_new):
        ch['c'][...] = dc * cp + u_
        ch['n'][...] = dc * np_ + jnp.sum(ww * kk.astype(F32), axis=0, keepdims=True)
        ch['m'][...] = mn


def _mlstm_kernel(qf_ref, kf_ref, vf_ref, gcf_ref, grf_ref, qb_ref, kb_ref, vb_ref, gcb_ref, grb_ref,
                  hf_ref, hb_ref, c_scr, n_scr, m_scr):
    @pl.when(pl.program_id(1) == 0)
    def _():
        c_scr[...] = jnp.zeros(c_scr.shape, F32)
        n_scr[...] = jnp.zeros(n_scr.shape, F32)
        m_scr[...] = jnp.zeros(m_scr.shape, F32)

    dh = MLSTM_DIM
    chains = []
    for i in range(gcf_ref.shape[0]):
        cols = slice(i * dh, (i + 1) * dh)
        for j, (q_ref, k_ref, v_ref, gc_ref, gr_ref, o_ref) in enumerate(
                ((qf_ref, kf_ref, vf_ref, gcf_ref, grf_ref, hf_ref), (qb_ref, kb_ref, vb_ref, gcb_ref, grb_ref, hb_ref))):
            gc = gc_ref[i]
            gr = gr_ref[i]
            s = 2 * i + j
            chains.append(dict(q=q_ref[:, cols], k=k_ref[:, cols], v=v_ref[:, cols],
                               ig_c=gc[:, 2 * j:2 * j + 1], fg_c=gc[:, 2 * j + 1:2 * j + 2],
                               ig_r=gr[2 * j:2 * j + 1, :], fg_r=gr[2 * j + 1:2 * j + 2, :],
                               c=c_scr.at[s], n=n_scr.at[s], m=m_scr.at[s], o=o_ref.at[:, cols], reverse=j == 1))
    _mlstm_chains(chains)


def _mlstm_scan(qk, u, gates, n_lat, n_ctx):
    s_pad = qk.shape[0]
    L, dh, nh, hp = MLSTM_CHUNK, MLSTM_DIM, MLSTM_HEADS, MLSTM_HEADS_PER_STEP
    nlc, ncc = n_lat // L, n_ctx // L
    steps = nlc + ncc
    ng = nh // hp
    g4 = gates.reshape(s_pad, 2, 2, nh).transpose(3, 0, 1, 2).reshape(nh, s_pad, 4)
    g4t = g4.transpose(0, 2, 1)

    def fwd(c):
        return jnp.where(c < ncc, nlc + c, c - ncc)

    def bwd(c):
        return jnp.where(c < ncc, nlc + ncc - 1 - c, nlc - 1 - (c - ncc))

    def specs(pos):
        return [pl.BlockSpec((L, hp * dh), lambda h, c: (pos(c), h)),
                pl.BlockSpec((L, hp * dh), lambda h, c: (pos(c), ng + h)),
                pl.BlockSpec((L, hp * dh), lambda h, c: (pos(c), 2 * ng + h)),
                pl.BlockSpec((hp, L, 4), lambda h, c: (h, pos(c), 0)),
                pl.BlockSpec((hp, 4, L), lambda h, c: (h, 0, pos(c)))]

    out_shape = jax.ShapeDtypeStruct((steps * L, nh * dh), BF16)
    return pl.pallas_call(
        _mlstm_kernel,
        grid=(ng, steps),
        in_specs=specs(fwd) + specs(bwd),
        out_specs=[pl.BlockSpec((L, hp * dh), lambda h, c: (fwd(c), h)),
                   pl.BlockSpec((L, hp * dh), lambda h, c: (bwd(c), h))],
        out_shape=[out_shape, out_shape],
        scratch_shapes=[pltpu.VMEM((2 * hp, dh, dh), F32), pltpu.VMEM((2 * hp, 1, dh), F32),
                        pltpu.VMEM((2 * hp, 1, 1), F32)],
        compiler_params=_cparams(("parallel", "arbitrary")),
        name="mlstm_scan",
    )(qk, qk, u, g4, g4t, qk, qk, u, g4, g4t)


def kernel(x, c, ctx, c_ctx, w_mod_0, b_mod_0, g_pre_mix_0, g_post_mix_0, w_in_0, lambda_q1_0, lambda_k1_0, lambda_q2_0, lambda_k2_0, g_subln_0, g_q_0, g_k_0, w_out_0, g_pre_ffn_0, g_post_ffn_0, w_router_0, w_gate_0, w_up_0, w_down_0, w_mod_1, b_mod_1, g_pre_mix_1, g_post_mix_1, w_in_1, b_gate_1, w_conv_1, b_conv_1, g_head_1, w_out_1, g_pre_ffn_1, g_post_ffn_1, w_router_1, w_gate_1, w_up_1, w_down_1):
    b, n_lat, d = x.shape
    n_ctx = ctx.shape[1]
    assert b == 1 and n_lat % ROW_TILE == 0 and n_ctx % MLSTM_CHUNK == 0 and n_ctx <= ROW_TILE
    s_valid = n_lat + n_ctx
    s_pad = n_lat + ROW_TILE
    n_lat_tiles = n_lat // ROW_TILE

    ctx_pad = jnp.concatenate([ctx[0], jnp.zeros((s_pad - s_valid, d), F32)], axis=0)
    cc = jnp.concatenate([c, c_ctx[None, :], jnp.zeros((6, d), F32)], axis=0)

    mod0 = _modulation(cc, w_mod_0, b_mod_0)
    tabs = _rope_tables(n_lat, n_ctx, s_pad)
    p0 = _proj0(x[0], ctx_pad, mod0, g_pre_mix_0, w_in_0.astype(BF16), tabs, g_q_0, g_k_0, n_lat_tiles)
    lam_init = 0.8 - 0.6 * math.exp(-0.3 * 0)
    lam4 = jnp.stack([lambda_q1_0, lambda_k1_0, lambda_q2_0, lambda_k2_0])
    o_a = _attn_diff(p0, lam4, g_subln_0, n_lat_tiles, s_valid, lam_init)
    o_b = _attn_gqa(p0, n_lat, s_valid)
    ka = o_a.shape[1]
    x_mid, hl, aff = _mixer_out(
        _out0_kernel, (o_a, o_b, ctx_pad),
        lambda tm: [pl.BlockSpec((tm, ka), lambda i: (i, 0)), pl.BlockSpec((tm, o_b.shape[1]), lambda i: (i, 0)),
                    pl.BlockSpec((tm, d), lambda i: (jnp.maximum(i - n_lat // tm, 0), 0))],
        x[0], mod0, w_out_0.astype(BF16), g_post_mix_0, g_pre_ffn_0, w_router_0, s_pad, n_lat // 256,
        lambda tm: [], "attn_out")
    p_moe0 = dict(w_gate=w_gate_0, w_up=w_up_0, w_down=w_down_0, g_post_ffn=g_post_ffn_0)
    xs = _moe(hl, aff, x_mid, mod0, p_moe0, n_lat, n_ctx, s_pad, True)

    mod1 = _modulation(cc, w_mod_1, b_mod_1)
    inner = MLSTM_HEADS * MLSTM_DIM
    u, gates = _proj1(xs, mod1, g_pre_mix_1, w_in_1.astype(BF16), 4 * inner, w_in_1[:, 4 * inner:], b_gate_1,
                      n_lat_tiles)
    qk = _conv_silu(u, w_conv_1, b_conv_1, 2 * inner, n_lat, n_ctx)
    hf, hb = _mlstm_scan(qk, u, gates, n_lat, n_ctx)
    og_blk = 3 * inner // inner
    x_mid, hl, aff = _mixer_out(
        _out1_kernel, (hf, hb, u, g_head_1.reshape(1, inner)),
        lambda tm: [pl.BlockSpec((tm, inner), lambda i: (i, 0)), pl.BlockSpec((tm, inner), lambda i: (i, 0)),
                    pl.BlockSpec((tm, inner), lambda i: (i, og_blk)), pl.BlockSpec((1, inner), lambda i: (0, 0))],
        xs, mod1, w_out_1.astype(BF16), g_post_mix_1, g_pre_ffn_1, w_router_1, n_lat, n_lat // 256,
        lambda tm: [pltpu.VMEM((tm, inner), BF16)], "mlstm_out")
    p_moe1 = dict(w_gate=w_gate_1, w_up=w_up_1, w_down=w_down_1, g_post_ffn=g_post_ffn_1)
    out = _moe(hl, aff, x_mid, mod1, p_moe1, n_lat, n_ctx, s_pad, False)
    return out[None]
```

```python
import functools
import math

import jax
import jax.numpy as jnp
from jax import lax
from jax.experimental import pallas as pl
from jax.experimental.pallas import tpu as pltpu

F32 = jnp.float32
BF16 = jnp.bfloat16
HIGHEST = lax.Precision.HIGHEST

GRID_W = 64
ROPE_THETA = 10000.0
RMS_EPS = 1e-6
N_MOD = 6
DIFF_HEADS = 8
DIFF_DIM = 64
GQA_HEADS = 8
GQA_KV_HEADS = 2
GQA_DIM = 128
MLSTM_HEADS = 8
MLSTM_DIM = 256
MLSTM_CHUNK = 128
N_EXPERTS = 16
EC_CAPACITY_FACTOR = 2

LANES = 128
ROW_TILE = 512
MLSTM_HEADS_PER_STEP = 4
FFN_MAX_TILE_ROWS = 1024
FFN_STEPS = 8
ATTN_TK = 512
ATTN_ROW_GROUP = 256
LOG2_E = math.log2(math.e)
TOPK_BLOCKS = 128
VMEM_LIMIT = 56 * 1024 * 1024


def _cparams(sem):
    return pltpu.CompilerParams(dimension_semantics=sem, vmem_limit_bytes=VMEM_LIMIT)


def _rms(x, g):
    return x * lax.rsqrt(jnp.mean(x * x, axis=-1, keepdims=True) + RMS_EPS) * g


def _dot(a, b):
    return jnp.dot(a, b, preferred_element_type=F32)


def _dot_nt(a, b, precision=None):
    return lax.dot_general(a, b, (((1,), (1,)), ((), ())), precision=precision, preferred_element_type=F32)


def _mod_kernel(c_ref, w_ref, b_ref, o_ref):
    c = c_ref[...]
    s = c * jax.nn.sigmoid(c)
    o_ref[...] = lax.dot_general(s, w_ref[...], (((1,), (0,)), ((), ())), precision=HIGHEST,
                                 preferred_element_type=F32) + b_ref[...]


def _modulation(cc, w_mod, b_mod):
    d, n = w_mod.shape
    tn = 1024
    return pl.pallas_call(
        _mod_kernel,
        grid=(n // tn,),
        in_specs=[pl.BlockSpec((8, d), lambda j: (0, 0)),
                  pl.BlockSpec((d, tn), lambda j: (0, j)),
                  pl.BlockSpec((1, tn), lambda j: (0, j))],
        out_specs=pl.BlockSpec((8, tn), lambda j: (0, j)),
        out_shape=jax.ShapeDtypeStruct((8, n), F32),
        compiler_params=_cparams(("arbitrary",)),
        name="modulation",
    )(cc, w_mod, b_mod.reshape(1, n))


def _mod_rows(mod_ref, r, k, d):
    return mod_ref[pl.ds(r, 1), k * d:(k + 1) * d]


def _premix(x, mod_ref, g_ref, r, d):
    return _rms(x, g_ref[...]) * (1.0 + _mod_rows(mod_ref, r, 1, d)) + _mod_rows(mod_ref, r, 0, d)


def _dot_hilo(a, b):
    a_hi = a.astype(BF16)
    a_lo = (a - a_hi.astype(F32)).astype(BF16)
    b_hi = b.astype(BF16)
    b_lo = (b - b_hi.astype(F32)).astype(BF16)
    return _dot(a_hi, b_hi) + (_dot(a_hi, b_lo) + _dot(a_lo, b_hi))


def _rope_a(x, ca, sa, sb):
    return x * ca + pltpu.roll(x, LANES - DIFF_DIM // 2, 1) * sa + pltpu.roll(x, DIFF_DIM // 2, 1) * sb


def _rope_b(x, cb, sb):
    return x * cb + pltpu.roll(x, GQA_DIM // 2, 1) * sb


def _proj0_kernel(x_ref, c_ref, mod_ref, gpre_ref, w_ref, ca_ref, saa_ref, sab_ref, cb_ref, sbb_ref, gq_ref, gk_ref,
                  o_ref, *, n_lat_tiles, d, tn):
    is_ctx = pl.program_id(0) >= n_lat_tiles
    x = jnp.where(is_ctx, c_ref[...], x_ref[...])
    h = _premix(x, mod_ref, gpre_ref, jnp.where(is_ctx, 1, 0), d).astype(BF16)
    n_qa = DIFF_HEADS
    n_qb = GQA_HEADS * GQA_DIM // LANES
    n_kb = GQA_KV_HEADS * GQA_DIM // LANES

    def epilogue(a, gi):
        if gi < 2 * n_qa:
            a = _rope_a(a, ca_ref[...], saa_ref[...], sab_ref[...])
            return a * (DIFF_DIM ** -0.5 * LOG2_E) if gi < n_qa else a
        if gi < 3 * n_qa:
            return a
        if gi < 3 * n_qa + n_qb:
            return _rope_b(_rms(a, gq_ref[...]), cb_ref[...], sbb_ref[...]) * (GQA_DIM ** -0.5 * LOG2_E)
        if gi < 3 * n_qa + n_qb + n_kb:
            return _rope_b(_rms(a, gk_ref[...]), cb_ref[...], sbb_ref[...])
        return a

    for j in range(w_ref.shape[1] // tn):
        acc = _dot(h, w_ref[:, j * tn:(j + 1) * tn])
        for g in range(tn // LANES):
            gi = j * (tn // LANES) + g
            o_ref[:, gi * LANES:(gi + 1) * LANES] = epilogue(acc[:, g * LANES:(g + 1) * LANES], gi).astype(BF16)


def _proj0(x, ctx_pad, mod, g_pre, w_in, tabs, g_q, g_k, n_lat_tiles):
    d = x.shape[1]
    s_pad = x.shape[0] + ctx_pad.shape[0]
    n_out = w_in.shape[1]
    tm, tn = ROW_TILE, 512
    row = lambda i: (i, 0)
    const = lambda i: (0, 0)
    once = pl.Buffered(1)
    tab_spec = pl.BlockSpec((tm, LANES), row)
    return pl.pallas_call(
        functools.partial(_proj0_kernel, n_lat_tiles=n_lat_tiles, d=d, tn=tn),
        grid=(s_pad // tm,),
        in_specs=[pl.BlockSpec((tm, d), lambda i: (jnp.minimum(i, n_lat_tiles - 1), 0)),
                  pl.BlockSpec((tm, d), const, pipeline_mode=once),
                  pl.BlockSpec((8, N_MOD * d), const, pipeline_mode=once),
                  pl.BlockSpec((1, d), const),
                  pl.BlockSpec((d, n_out), const, pipeline_mode=once),
                  tab_spec, tab_spec, tab_spec, tab_spec, tab_spec,
                  pl.BlockSpec((1, LANES), const),
                  pl.BlockSpec((1, LANES), const)],
        out_specs=pl.BlockSpec((tm, n_out), row),
        out_shape=jax.ShapeDtypeStruct((s_pad, n_out), BF16),
        compiler_params=_cparams(("parallel",)),
        name="proj0",
    )(x, ctx_pad, mod, g_pre.reshape(1, d), w_in, *tabs, g_q.reshape(1, LANES), g_k.reshape(1, LANES))


def _rope_tables(n_lat, n_ctx, s_pad):
    rows = jnp.repeat(jnp.arange(n_lat // GRID_W, dtype=F32), GRID_W)
    cols = jnp.tile(jnp.arange(GRID_W, dtype=F32), n_lat // GRID_W)

    def angles(dim):
        axis_dim = dim // 2
        inv_freq = ROPE_THETA ** (-jnp.arange(0, axis_dim, 2, dtype=F32) / axis_dim)
        ang = jnp.concatenate([rows[:, None] * inv_freq, cols[:, None] * inv_freq], axis=-1)
        return jnp.cos(ang), jnp.sin(ang)

    def pad(t, fill):
        return jnp.concatenate([t, jnp.full((s_pad - n_lat, LANES), fill, F32)], axis=0)

    cos_a, sin_a = angles(DIFF_DIM)
    zero = jnp.zeros_like(sin_a)
    ca = jnp.tile(cos_a, (1, 4))
    saa = jnp.tile(jnp.concatenate([-sin_a, zero], axis=1), (1, 2))
    sab = jnp.tile(jnp.concatenate([zero, sin_a], axis=1), (1, 2))
    cos_b, sin_b = angles(GQA_DIM)
    cb = jnp.tile(cos_b, (1, 2))
    sbb = jnp.concatenate([-sin_b, sin_b], axis=1)
    return pad(ca, 1.0), pad(saa, 0.0), pad(sab, 0.0), pad(cb, 1.0), pad(sbb, 0.0)


def _flash_rows(q, k_ref, v_ref, m_scr, acc_scr, s_scr, is_ctx_q, tk, s_valid):
    n_chunks = k_ref.shape[0] // tk
    last = n_chunks - 1
    groups = [slice(g * ATTN_ROW_GROUP, (g + 1) * ATTN_ROW_GROUP) for g in range(q.shape[0] // ATTN_ROW_GROUP)]
    m_scr[...] = jnp.full(m_scr.shape, -jnp.inf, F32)
    acc_scr[...] = jnp.zeros(acc_scr.shape, F32)

    def rows_of(c):
        return pl.ds(c * tk if isinstance(c, int) else pl.multiple_of(c * tk, tk), tk)

    def scores(c, buf, rows):
        s_scr[buf, rows] = _dot_nt(q[rows], k_ref[rows_of(c), :])

    def consume(c, buf, rows, v1, masked):
        s = s_scr[buf, rows]
        if masked:
            kpos = c * tk + lax.broadcasted_iota(jnp.int32, (1, tk), 1)
            s = jnp.where(kpos < s_valid, s, -jnp.inf)
        m_prev = m_scr[rows]
        m_new = jnp.maximum(m_prev, jnp.max(s, axis=1, keepdims=True))
        p = jnp.exp2(s - m_new).astype(BF16)
        acc_scr[rows] = jnp.exp2(m_prev - m_new) * acc_scr[rows] + _dot(p, v1)
        m_scr[rows] = m_new

    def values(c):
        vc = v_ref[rows_of(c), :]
        return jnp.concatenate([vc, jnp.ones_like(vc)], axis=1)

    unroll = max(u for u in (2, 4, 8) if last % u == 0)
    lo = jnp.where(is_ctx_q, last, 0)
    for rows in groups:
        scores(lo, 0, rows)

    def body(i, carry):
        c = lo + unroll * i
        for j in range(unroll):
            v1 = values(c + j)
            for rows in groups:
                scores(c + j + 1, (j + 1) % 2, rows)
                consume(c + j, j % 2, rows, v1, False)
        return carry

    lax.fori_loop(0, (last - lo) // unroll, body, 0)
    v1 = values(last)
    for rows in groups:
        consume(last, 0, rows, v1, True)


def _attn_diff_kernel(q_ref, k_ref, v_ref, lam_ref, gsub_ref, o_ref, m_scr, acc_scr, s_scr,
                      *, n_lat_tiles, tk, s_valid, lam_init):
    q = q_ref[...]
    tq, hd = q.shape
    first = lax.broadcasted_iota(jnp.int32, (1, hd), 1) < DIFF_DIM
    zero = jnp.zeros_like(q)
    qq = jnp.concatenate([jnp.where(first, q, zero), jnp.where(first, zero, q)], axis=0)
    _flash_rows(qq, k_ref, v_ref, m_scr, acc_scr, s_scr, pl.program_id(1) >= n_lat_tiles, tk, s_valid)
    lam = (jnp.exp(jnp.sum(lam_ref[0:1, :] * lam_ref[1:2, :], axis=1, keepdims=True))
           - jnp.exp(jnp.sum(lam_ref[2:3, :] * lam_ref[3:4, :], axis=1, keepdims=True)) + lam_init)
    o = (acc_scr[0:tq, 0:hd] / acc_scr[0:tq, hd:hd + 1]
         - lam * (acc_scr[tq:2 * tq, 0:hd] / acc_scr[tq:2 * tq, hd:hd + 1]))
    o_ref[...] = (_rms(o, gsub_ref[...]) * (1.0 - lam_init)).astype(BF16)


def _attn_diff(p0, lam4, g_subln, n_lat, s_valid, lam_init):
    s_pad = p0.shape[0]
    tq = 2 * ROW_TILE
    assert n_lat % tq == 0
    n_lat_tiles = n_lat // tq
    hd = 2 * DIFF_DIM
    k0 = DIFF_HEADS
    v0 = 2 * DIFF_HEADS
    return pl.pallas_call(
        functools.partial(_attn_diff_kernel, n_lat_tiles=n_lat_tiles, tk=ATTN_TK, s_valid=s_valid, lam_init=lam_init),
        grid=(DIFF_HEADS, pl.cdiv(s_pad, tq)),
        in_specs=[pl.BlockSpec((tq, hd), lambda h, iq: (iq, h)),
                  pl.BlockSpec((s_pad, hd), lambda h, iq: (0, k0 + h)),
                  pl.BlockSpec((s_pad, hd), lambda h, iq: (0, v0 + h)),
                  pl.BlockSpec((4, DIFF_DIM), lambda h, iq: (0, 0)),
                  pl.BlockSpec((1, hd), lambda h, iq: (0, 0))],
        out_specs=pl.BlockSpec((tq, hd), lambda h, iq: (iq, h)),
        out_shape=jax.ShapeDtypeStruct((s_pad, DIFF_HEADS * hd), BF16),
        scratch_shapes=[pltpu.VMEM((2 * tq, 1), F32), pltpu.VMEM((2 * tq, 2 * hd), F32),
                        pltpu.VMEM((2, 2 * tq, ATTN_TK), F32)],
        compiler_params=_cparams(("parallel", "arbitrary")),
        name="attn_diff",
    )(p0, p0, p0, lam4, g_subln.reshape(1, hd))


def _attn_gqa_kernel(q_ref, k_ref, v_ref, o_ref, m_scr, acc_scr, s_scr, *, n_lat_tiles, tk, s_valid, group):
    tq = q_ref.shape[0]
    dh = GQA_DIM
    qq = jnp.concatenate([q_ref[:, g * dh:(g + 1) * dh] for g in range(group)], axis=0)
    _flash_rows(qq, k_ref, v_ref, m_scr, acc_scr, s_scr, pl.program_id(1) >= n_lat_tiles, tk, s_valid)
    for g in range(group):
        rows = slice(g * tq, (g + 1) * tq)
        o_ref[:, g * dh:(g + 1) * dh] = (acc_scr[rows, 0:dh] / acc_scr[rows, dh:dh + 1]).astype(BF16)


def _attn_gqa(p0, n_lat, s_valid):
    s_pad = p0.shape[0]
    tq = ROW_TILE
    group = GQA_HEADS // GQA_KV_HEADS
    qw = group * GQA_DIM
    q0 = 3 * DIFF_HEADS * 2 * DIFF_DIM // qw
    k0 = (3 * DIFF_HEADS * 2 * DIFF_DIM + GQA_HEADS * GQA_DIM) // GQA_DIM
    v0 = k0 + GQA_KV_HEADS
    return pl.pallas_call(
        functools.partial(_attn_gqa_kernel, n_lat_tiles=n_lat // tq, tk=ATTN_TK, s_valid=s_valid, group=group),
        grid=(GQA_KV_HEADS, s_pad // tq),
        in_specs=[pl.BlockSpec((tq, qw), lambda h, iq: (iq, q0 + h)),
                  pl.BlockSpec((s_pad, GQA_DIM), lambda h, iq: (0, k0 + h)),
                  pl.BlockSpec((s_pad, GQA_DIM), lambda h, iq: (0, v0 + h))],
        out_specs=pl.BlockSpec((tq, qw), lambda h, iq: (iq, h)),
        out_shape=jax.ShapeDtypeStruct((s_pad, GQA_HEADS * GQA_DIM), BF16),
        scratch_shapes=[pltpu.VMEM((group * tq, 1), F32), pltpu.VMEM((group * tq, 2 * GQA_DIM), F32),
                        pltpu.VMEM((2, group * tq, ATTN_TK), F32)],
        compiler_params=_cparams(("parallel", "arbitrary")),
        name="attn_gqa",
    )(p0, p0, p0)


def _post_mix(y, x, mod_ref, r, gpost_ref, gffn_ref, wr_ref, xo_ref, hl_ref, aff_ref, d):
    xn = x + _mod_rows(mod_ref, r, 2, d) * _rms(y, gpost_ref[...])
    xo_ref[...] = xn
    hl = _rms(xn, gffn_ref[...]) * (1.0 + _mod_rows(mod_ref, r, 4, d)) + _mod_rows(mod_ref, r, 3, d)
    hl_ref[...] = hl
    logits = _dot_hilo(hl, wr_ref[...])
    e = jnp.exp(logits - jnp.max(logits, axis=1, keepdims=True))
    aff_ref[...] = e / jnp.sum(e, axis=1, keepdims=True)


def _out0_kernel(oa_ref, ob_ref, c_ref, x_ref, mod_ref, w_ref, gpost_ref, gffn_ref, wr_ref, xo_ref, hl_ref, aff_ref,
                 *, n_lat_tiles, d):
    is_ctx = pl.program_id(0) >= n_lat_tiles
    r = jnp.where(is_ctx, 1, 0)
    ka = oa_ref.shape[1]
    y = _dot(oa_ref[...], w_ref[0:ka, :]) + _dot(ob_ref[...], w_ref[ka:, :])
    x = jnp.where(is_ctx, c_ref[...], x_ref[...])
    _post_mix(y, x, mod_ref, r, gpost_ref, gffn_ref, wr_ref, xo_ref, hl_ref, aff_ref, d)


def _out1_kernel(hf_ref, hb_ref, og_ref, ghead_ref, x_ref, mod_ref, w_ref, gpost_ref, gffn_ref, wr_ref,
                 xo_ref, hl_ref, aff_ref, y_scr, *, n_lat_tiles, d):
    r = jnp.where(pl.program_id(0) >= n_lat_tiles, 1, 0)
    for h in range(MLSTM_HEADS):
        sl = slice(h * MLSTM_DIM, (h + 1) * MLSTM_DIM)
        hs = hf_ref[:, sl].astype(F32) + hb_ref[:, sl].astype(F32)
        hn = _rms(hs, ghead_ref[:, sl])
        y_scr[:, sl] = (jax.nn.sigmoid(og_ref[:, sl].astype(F32)) * hn).astype(BF16)
    y = _dot(y_scr[...], w_ref[...])
    _post_mix(y, x_ref[...], mod_ref, r, gpost_ref, gffn_ref, wr_ref, xo_ref, hl_ref, aff_ref, d)


def _mixer_out(kernel_fn, mix_inputs, mix_specs, x, mod, w_out, g_post, g_ffn, w_router, n_rows, n_lat_tiles_tm,
               scratch, name):
    d = x.shape[1]
    tm = 256
    row = lambda i: (i, 0)
    x_row = lambda i: (jnp.minimum(i, x.shape[0] // tm - 1), 0)
    const = lambda i: (0, 0)
    ne = w_router.shape[1]
    return pl.pallas_call(
        functools.partial(kernel_fn, n_lat_tiles=n_lat_tiles_tm, d=d),
        grid=(n_rows // tm,),
        in_specs=mix_specs(tm) + [pl.BlockSpec((tm, d), x_row),
                                  pl.BlockSpec((8, N_MOD * d), const),
                                  pl.BlockSpec(w_out.shape, const),
                                  pl.BlockSpec((1, d), const),
                                  pl.BlockSpec((1, d), const),
                                  pl.BlockSpec((d, ne), const)],
        out_specs=[pl.BlockSpec((tm, d), row), pl.BlockSpec((tm, d), row), pl.BlockSpec((tm, ne), row)],
        out_shape=[jax.ShapeDtypeStruct((n_rows, d), F32), jax.ShapeDtypeStruct((n_rows, d), F32),
                   jax.ShapeDtypeStruct((n_rows, ne), F32)],
        scratch_shapes=scratch(tm),
        compiler_params=_cparams(("parallel",)),
        name=name,
    )(*mix_inputs, x, mod, w_out, g_post.reshape(1, d), g_ffn.reshape(1, d), w_router)


def _incl_cumsum(maskf, upper, strict_lower):
    mb = maskf.astype(BF16)
    local = _dot(mb, upper)
    start = jnp.sum(_dot(strict_lower, mb), axis=1, keepdims=True)
    return local, start


def _topk_kernel(aff_ref, idx_ref, gate_ref, dest_ref, cnt_ref, cnt_acc, *, cap, dest_stride, row_off):
    e = pl.program_id(0)
    ne = pl.num_programs(0)
    nb = aff_ref.shape[1]
    a = aff_ref[0]
    bits = pltpu.bitcast(a, jnp.int32)

    thr = jnp.zeros((1, 1), jnp.int32)
    for bit in range(30, -1, -1):
        cand = thr | (1 << bit)
        c = jnp.sum(jnp.sum(jnp.where(bits >= cand, 1.0, 0.0), axis=1, keepdims=True), axis=0, keepdims=True)
        thr = jnp.where(c >= cap, cand, thr)

    r_i = lax.broadcasted_iota(jnp.int32, (LANES, LANES), 0)
    c_i = lax.broadcasted_iota(jnp.int32, (LANES, LANES), 1)
    upper = jnp.where(r_i <= c_i, 1.0, 0.0).astype(BF16)
    rb = lax.broadcasted_iota(jnp.int32, (nb, nb), 0)
    cb = lax.broadcasted_iota(jnp.int32, (nb, nb), 1)
    strict_lower = jnp.where(cb < rb, 1.0, 0.0).astype(BF16)

    gt = bits > thr
    eq = bits == thr
    n_gt = jnp.sum(jnp.sum(jnp.where(gt, 1.0, 0.0), axis=1, keepdims=True), axis=0, keepdims=True)
    eq_local, eq_start = _incl_cumsum(jnp.where(eq, 1.0, 0.0), upper, strict_lower)
    sel = jnp.logical_or(gt, jnp.logical_and(eq, eq_local + eq_start <= cap - n_gt))
    self_ = jnp.where(sel, 1.0, 0.0)
    sel_local, sel_start = _incl_cumsum(self_, upper, strict_lower)
    sel_tot = jnp.sum(self_, axis=1, keepdims=True)

    @pl.when(e == 0)
    def _():
        cnt_acc[...] = jnp.zeros(cnt_acc.shape, F32)

    earlier = cnt_acc[...]
    cnt_acc[...] = earlier + self_

    lane = lax.broadcasted_iota(jnp.int32, (nb, LANES), 1)
    cols = jnp.where(lane == 0, sel_start, jnp.where(lane == 1, sel_tot, 0.0))
    rows = cols.T
    start_row = rows[0:1, 0:nb]
    tot_row = rows[1:2, 0:nb]

    p = lax.broadcasted_iota(jnp.int32, (cap, 1), 0).astype(F32)
    onehot = jnp.where(jnp.logical_and(start_row <= p, p < start_row + tot_row), 1.0, 0.0)
    blk = lax.broadcasted_iota(jnp.int32, (1, nb), 1).astype(F32)
    p_start = jnp.sum(onehot * start_row, axis=1, keepdims=True)
    ohb = onehot.astype(BF16)
    local_rows = _dot(ohb, sel_local.astype(BF16))
    sel_rows = _dot(ohb, self_.astype(BF16))
    earlier_rows = _dot(ohb, earlier.astype(BF16))
    aff_rows = lax.dot_general(onehot, a, (((1,), (0,)), ((), ())), precision=HIGHEST, preferred_element_type=F32)
    hit = jnp.where(jnp.logical_and(local_rows == p - p_start + 1.0, sel_rows > 0.5), 1.0, 0.0)
    lane_f = lax.broadcasted_iota(jnp.int32, (1, LANES), 1).astype(F32)
    gate_ref[0] = jnp.sum(hit * aff_rows, axis=1, keepdims=True)
    hb = hit.astype(BF16)
    rows8 = lambda v: jnp.broadcast_to(v, (8, v.shape[1])).astype(BF16)
    tok = (_dot_nt(rows8(blk), ohb) * LANES + _dot_nt(rows8(lane_f), hb))[0:1, :]
    rank = _dot_nt(jnp.ones((8, LANES), BF16), (hit * earlier_rows).astype(BF16))[0:1, :]
    idx_ref[0] = tok.astype(jnp.int32) + row_off
    dest_ref[0] = (rank * dest_stride + tok).astype(jnp.int32) + row_off

    @pl.when(e == ne - 1)
    def _():
        cnt_ref[...] = cnt_acc[...].astype(jnp.int32)


def _route(aff_t, cap, dest_stride, row_off):
    ne, n = aff_t.shape
    n_pad = TOPK_BLOCKS * LANES
    assert n <= n_pad
    a = jnp.pad(aff_t, ((0, 0), (0, n_pad - n)), constant_values=-1.0).reshape(ne, TOPK_BLOCKS, LANES)
    slot = pl.BlockSpec((1, cap, 1), lambda e: (e, 0, 0))
    lane_slot = pl.BlockSpec((1, 1, cap), lambda e: (e, 0, 0))
    idx, gate, dest, cnt = pl.pallas_call(
        functools.partial(_topk_kernel, cap=cap, dest_stride=dest_stride, row_off=row_off),
        grid=(ne,),
        in_specs=[pl.BlockSpec((1, TOPK_BLOCKS, LANES), lambda e: (e, 0, 0))],
        out_specs=[lane_slot, slot, lane_slot, pl.BlockSpec((TOPK_BLOCKS, LANES), lambda e: (0, 0))],
        out_shape=[jax.ShapeDtypeStruct((ne, 1, cap), jnp.int32), jax.ShapeDtypeStruct((ne, cap, 1), F32),
                   jax.ShapeDtypeStruct((ne, 1, cap), jnp.int32), jax.ShapeDtypeStruct((TOPK_BLOCKS, LANES), jnp.int32)],
        scratch_shapes=[pltpu.VMEM((TOPK_BLOCKS, LANES), F32)],
        compiler_params=_cparams(("arbitrary",)),
        name="route_topk",
    )(a)
    return idx.reshape(ne * cap), gate, dest.reshape(ne * cap), cnt.reshape(n_pad)[:n]


def _pack_bf16_pairs(y):
    n = y.shape[1] // 2
    bits = lax.bitcast_convert_type(y.astype(BF16).astype(F32), jnp.uint32)
    return (bits[:, n:] & jnp.uint32(0xFFFF0000)) | (bits[:, :n] >> 16)


def _unpack_bf16_pairs(w):
    return (lax.bitcast_convert_type(w << 16, F32), lax.bitcast_convert_type(w & jnp.uint32(0xFFFF0000), F32))


def _row_copy(src, s, dst, t, sem):
    return pltpu.make_async_copy(src.at[pl.ds(s, 1)], dst.at[pl.ds(t, 1)], sem)


def _ffn_body(idx_sm, dest_sm, hl_hbm, gate_ref, wg_ref, wu_ref, wd_ref, z_hbm, xbuf, xb, acc, zp, gsem, ssem, *, tm):
    f = pl.program_id(2)
    nm = pl.num_programs(1)
    t = pl.program_id(0) * nm + pl.program_id(1)
    n_tiles = pl.num_programs(0) * nm
    cur = t % 2
    nxt = 1 - cur
    chunk = tm // FFN_STEPS
    last_step = f == pl.num_programs(2) - 1
    t_next = jnp.minimum(t + 1, n_tiles - 1)
    t_prev = jnp.maximum(t - 1, 0)

    def gather(tile, slot, r):
        return _row_copy(hl_hbm, idx_sm[tile * tm + r], xbuf.at[slot], r, gsem.at[slot])

    def scatter(tile, slot, r):
        return _row_copy(zp.at[slot], r, z_hbm, dest_sm[tile * tm + r], ssem.at[slot])

    def for_rows(fn):
        def body(r, c):
            fn(r)
            return c
        lax.fori_loop(0, tm, body, 0, unroll=8)

    @pl.when(jnp.logical_and(t == 0, f == 0))
    def _():
        for_rows(lambda r: gather(0, 0, r).start())
        zp[1] = jnp.zeros(zp.shape[1:], jnp.uint32)

    @pl.when(f == 0)
    def _():
        for_rows(lambda r: gather(t, cur, r).wait())
        xb[...] = xbuf[cur].astype(BF16)
        acc[...] = jnp.zeros(acc.shape, F32)

        @pl.when(t >= 1)
        def _():
            for_rows(lambda r: scatter(t_prev, cur, r).wait())

    for r in range(chunk):
        gather(t_next, nxt, f * chunk + r).start()
        scatter(t_prev, nxt, f * chunk + r).start()

    x = xb[...]
    g = _dot(x, wg_ref[0].astype(BF16))
    u = _dot(x, wu_ref[0].astype(BF16))
    hid = (g * jax.nn.sigmoid(g) * u).astype(BF16)
    acc[...] += _dot(hid, wd_ref[0].astype(BF16))

    @pl.when(last_step)
    def _():
        zp[cur] = _pack_bf16_pairs(acc[...] * gate_ref[0])

    @pl.when(jnp.logical_and(t == n_tiles - 1, last_step))
    def _():
        for_rows(lambda r: gather(t, nxt, r).wait())
        for_rows(lambda r: scatter(t_prev, nxt, r).wait())
        for_rows(lambda r: scatter(t, cur, r).start())
        for_rows(lambda r: scatter(t, cur, r).wait())


def _expert_ffn(idx, dest, hl, gate, w_gate, w_up, w_down, z_rows, n_tiles):
    ne, d, ff = w_gate.shape
    cap = gate.shape[1]
    tm = cap // n_tiles
    assert tm * n_tiles == cap and tm % FFN_STEPS == 0 and tm % 16 == 0
    tf = ff // FFN_STEPS
    any_spec = pl.BlockSpec(memory_space=pl.ANY)
    return pl.pallas_call(
        functools.partial(_ffn_body, tm=tm),
        grid_spec=pltpu.PrefetchScalarGridSpec(
            num_scalar_prefetch=2,
            grid=(ne, n_tiles, FFN_STEPS),
            in_specs=[any_spec,
                      pl.BlockSpec((1, tm, 1), lambda e, m, f, *_: (e, m, 0)),
                      pl.BlockSpec((1, d, tf), lambda e, m, f, *_: (e, 0, f)),
                      pl.BlockSpec((1, d, tf), lambda e, m, f, *_: (e, 0, f)),
                      pl.BlockSpec((1, tf, d), lambda e, m, f, *_: (e, f, 0))],
            out_specs=any_spec,
            scratch_shapes=[pltpu.VMEM((2, tm, d), F32), pltpu.VMEM((tm, d), BF16), pltpu.VMEM((tm, d), F32),
                            pltpu.VMEM((2, tm, d // 2), jnp.uint32),
                            pltpu.SemaphoreType.DMA((2,)), pltpu.SemaphoreType.DMA((2,))]),
        out_shape=jax.ShapeDtypeStruct((z_rows, d // 2), jnp.uint32),
        compiler_params=_cparams(("arbitrary", "arbitrary", "arbitrary")),
        name="expert_ffn",
    )(idx, dest, hl, gate, w_gate, w_up, w_down)


def _combine_kernel(maxc_sm, za_ref, zb_ref, cnt_ref, x_ref, mod_ref, g_ref, o_ref, acc, *, n_lat_tiles, d):
    i = pl.program_id(0)
    kk = pl.program_id(1)
    nk = pl.num_programs(1)

    @pl.when(kk == 0)
    def _():
        acc[...] = jnp.zeros(acc.shape, F32)

    def add_slab(z_ref, k):
        @pl.when(k < maxc_sm[i])
        def _():
            lo, hi = _unpack_bf16_pairs(z_ref[...])
            live = cnt_ref[...] > k
            acc[:, 0:d // 2] += jnp.where(live, lo, 0.0)
            acc[:, d // 2:d] += jnp.where(live, hi, 0.0)

    add_slab(za_ref, 2 * kk)
    add_slab(zb_ref, 2 * kk + 1)

    @pl.when(kk == nk - 1)
    def _():
        r = jnp.where(i >= n_lat_tiles, 1, 0)
        o_ref[...] = x_ref[...] + _mod_rows(mod_ref, r, 5, d) * _rms(acc[...], g_ref[...])


def _combine(z, cnt, x, mod, g_post, n_rows, z_stride, n_lat_tiles_tb):
    d = x.shape[1]
    tb = ROW_TILE
    nb = n_rows // tb
    zb = z_stride // tb
    maxc = jnp.max(cnt[:n_rows].reshape(nb, tb), axis=1)

    def z_map(j):
        return lambda i, kk, maxc_sm: (jnp.maximum(jnp.minimum(2 * kk + j, maxc_sm[i] - 1), 0) * zb + i, 0)

    return pl.pallas_call(
        functools.partial(_combine_kernel, n_lat_tiles=n_lat_tiles_tb, d=d),
        grid_spec=pltpu.PrefetchScalarGridSpec(
            num_scalar_prefetch=1,
            grid=(nb, N_EXPERTS // 2),
            in_specs=[pl.BlockSpec((tb, d // 2), z_map(0)),
                      pl.BlockSpec((tb, d // 2), z_map(1)),
                      pl.BlockSpec((tb, 1), lambda i, k, s: (i, 0)),
                      pl.BlockSpec((tb, d), lambda i, k, s: (i, 0)),
                      pl.BlockSpec((8, N_MOD * d), lambda i, k, s: (0, 0)),
                      pl.BlockSpec((1, d), lambda i, k, s: (0, 0))],
            out_specs=pl.BlockSpec((tb, d), lambda i, k, s: (i, 0)),
            scratch_shapes=[pltpu.VMEM((tb, d), F32)]),
        out_shape=jax.ShapeDtypeStruct((n_rows, d), F32),
        compiler_params=_cparams(("parallel", "arbitrary")),
        name="moe_combine",
    )(maxc, z, z, cnt.reshape(-1, 1), x, mod, g_post.reshape(1, d))


def _moe(hl, aff, x_mid, mod, p, n_lat, n_ctx, s_pad, with_ctx):
    aff_t = aff.T
    ne = N_EXPERTS
    cap_l = EC_CAPACITY_FACTOR * n_lat // ne
    n_tiles = max(1, cap_l // FFN_MAX_TILE_ROWS)
    idx, gate, dest, cnt = _route(aff_t[:, :n_lat], cap_l, s_pad, 0)
    n_rows = n_lat
    if with_ctx:
        cap_c = EC_CAPACITY_FACTOR * n_ctx // ne
        idx_c, gate_c, dest_c, cnt_c = _route(aff_t[:, n_lat:n_lat + n_ctx], cap_c, s_pad, n_lat)

        def per_tile(a, b):
            return jnp.concatenate([a.reshape(ne, n_tiles, cap_l // n_tiles), b.reshape(ne, n_tiles, cap_c // n_tiles)],
                                   axis=2).reshape(ne, cap_l + cap_c)
        idx = per_tile(idx, idx_c).reshape(-1)
        dest = per_tile(dest, dest_c).reshape(-1)
        gate = per_tile(gate, gate_c)[:, :, None]
        cnt = jnp.concatenate([cnt, cnt_c, jnp.zeros((s_pad - n_lat - n_ctx,), jnp.int32)])
        n_rows = s_pad
    z = _expert_ffn(idx, dest, hl, gate, p['w_gate'], p['w_up'], p['w_down'], ne * s_pad, n_tiles)
    return _combine(z, cnt, x_mid, mod, p['g_post_ffn'], n_rows, s_pad, n_lat // ROW_TILE)


def _proj1_kernel(x_ref, mod_ref, gpre_ref, w_ref, wg_ref, bg_ref, o_ref, g_ref, h_scr, *, n_lat_tiles, d):
    i = pl.program_id(0)
    j = pl.program_id(1)

    @pl.when(j == 0)
    def _():
        r = jnp.where(i >= n_lat_tiles, 1, 0)
        h = _premix(x_ref[...], mod_ref, gpre_ref, r, d)
        h_scr[...] = h.astype(BF16)
        g_ref[...] = _dot_hilo(h, wg_ref[...]) + bg_ref[...]

    o_ref[...] = _dot(h_scr[...], w_ref[...]).astype(BF16)


def _proj1(xs, mod, g_pre, w_main, n_out, w_gates, b_gate, n_lat_tiles):
    s_pad, d = xs.shape
    ng = w_gates.shape[1]
    tm, tn = ROW_TILE, 1024
    row = lambda i, j: (i, 0)
    const = lambda i, j: (0, 0)
    return pl.pallas_call(
        functools.partial(_proj1_kernel, n_lat_tiles=n_lat_tiles, d=d),
        grid=(s_pad // tm, n_out // tn),
        in_specs=[pl.BlockSpec((tm, d), row),
                  pl.BlockSpec((8, N_MOD * d), const),
                  pl.BlockSpec((1, d), const),
                  pl.BlockSpec((d, tn), lambda i, j: (0, j)),
                  pl.BlockSpec((d, ng), const),
                  pl.BlockSpec((1, ng), const)],
        out_specs=[pl.BlockSpec((tm, tn), lambda i, j: (i, j)), pl.BlockSpec((tm, ng), row)],
        out_shape=[jax.ShapeDtypeStruct((s_pad, n_out), BF16), jax.ShapeDtypeStruct((s_pad, ng), F32)],
        scratch_shapes=[pltpu.VMEM((tm, d), BF16)],
        compiler_params=_cparams(("parallel", "arbitrary")),
        name="proj1",
    )(xs, mod, g_pre.reshape(1, d), w_main, w_gates, b_gate.reshape(1, ng))


def _conv_kernel(cur_ref, prev_ref, next_ref, w_ref, b_ref, o_ref, *, n_lat, n_ctx, k_cols0, k_scale):
    i = pl.program_id(0)
    j = pl.program_id(1)
    tm, tc = cur_ref.shape
    hr = prev_ref.shape[0]
    cur = cur_ref[...].astype(F32)
    row = lax.broadcasted_iota(jnp.int32, (tm, 1), 0)
    g = i * tm + row
    prev_row = prev_ref[hr - 1:hr, :].astype(F32)
    next_row = next_ref[0:1, :].astype(F32)
    before = jnp.where(row == 0, prev_row, pltpu.roll(cur, 1, 0))
    after = jnp.where(row == tm - 1, next_row, pltpu.roll(cur, tm - 1, 0))
    before = jnp.where(jnp.logical_or(g == 0, g == n_lat), 0.0, before)
    after = jnp.where(jnp.logical_or(g == n_lat - 1, g == n_lat + n_ctx - 1), 0.0, after)
    y = before * w_ref[0:1, :] + cur * w_ref[1:2, :] + after * w_ref[2:3, :] + b_ref[...]
    y = y * jax.nn.sigmoid(y)
    o_ref[...] = (y * jnp.where(j * tc >= k_cols0, k_scale, 1.0)).astype(BF16)


def _conv_silu(u, w_conv, b_conv, n_cols, n_lat, n_ctx):
    s_pad = u.shape[0]
    tm, tc, hr = ROW_TILE, 1024, 16
    rb = tm // hr
    last = s_pad // hr - 1
    return pl.pallas_call(
        functools.partial(_conv_kernel, n_lat=n_lat, n_ctx=n_ctx, k_cols0=n_cols // 2, k_scale=MLSTM_DIM ** -0.5),
        grid=(s_pad // tm, n_cols // tc),
        in_specs=[pl.BlockSpec((tm, tc), lambda i, j: (i, j)),
                  pl.BlockSpec((hr, tc), lambda i, j: (jnp.maximum(i * rb - 1, 0), j)),
                  pl.BlockSpec((hr, tc), lambda i, j: (jnp.minimum((i + 1) * rb, last), j)),
                  pl.BlockSpec((8, tc), lambda i, j: (0, j)),
                  pl.BlockSpec((1, tc), lambda i, j: (0, j))],
        out_specs=pl.BlockSpec((tm, tc), lambda i, j: (i, j)),
        out_shape=jax.ShapeDtypeStruct((s_pad, n_cols), BF16),
        compiler_params=_cparams(("parallel", "parallel")),
        name="conv_silu",
    )(u, u, u, jnp.pad(w_conv, ((0, 8 - w_conv.shape[0]), (0, 0))), b_conv.reshape(1, n_cols))


def _log_sigmoid(x):
    return -(jnp.maximum(-x, 0.0) + jnp.log1p(jnp.exp(-jnp.abs(x))))


def _mlstm_chains(chains):
    def each(fn, *lists):
        return [fn(*a) for a in zip(*lists)]

    L = chains[0]['q'].shape[0]
    t_i = lax.broadcasted_iota(jnp.int32, (L, L), 0)
    s_i = lax.broadcasted_iota(jnp.int32, (L, L), 1)
    seen = [(s_i >= t_i) if ch['reverse'] else (s_i <= t_i) for ch in chains]
    q = [ch['q'] for ch in chains]
    k = [ch['k'] for ch in chains]
    v = [ch['v'] for ch in chains]
    ig_c = [ch['ig_c'] for ch in chains]
    ig_r = [ch['ig_r'] for ch in chains]
    lf_c = [_log_sigmoid(ch['fg_c']) for ch in chains]
    lf_r = [_log_sigmoid(ch['fg_r']) for ch in chains]
    m_prev = [ch['m'][...] for ch in chains]
    c_prev = [ch['c'][...] for ch in chains]
    n_prev = [ch['n'][...] for ch in chains]
    qk = each(_dot_nt, q, k)
    qc = each(lambda a, c: _dot_nt(a, c.astype(BF16)), q, c_prev)
    b_c = each(lambda sn, r: jnp.sum(jnp.where(sn, 1.0, 0.0) * r, axis=1, keepdims=True), seen, lf_r)
    b_r = each(lambda sn, c, r: jnp.sum(jnp.where(sn, 0.0, 1.0) * c, axis=0, keepdims=True) + r, seen, lf_c, lf_r)
    log_d = each(lambda sn, bc, br, g: jnp.where(sn, bc - br + g, -jnp.inf), seen, b_c, b_r, ig_r)
    log_inter = each(lambda bc, m: bc + m, b_c, m_prev)
    m_t = each(lambda li, ld: jnp.maximum(li, jnp.max(ld, axis=1, keepdims=True)), log_inter, log_d)
    w_intra = each(lambda ld, mt, s: jnp.exp(ld - mt) * s, log_d, m_t, qk)
    w_inter = each(lambda li, mt: jnp.exp(li - mt), log_inter, m_t)
    wiv = each(lambda w, vv: _dot(w.astype(BF16), vv), w_intra, v)
    num = each(lambda wi, a, b: wi * a + b, w_inter, qc, wiv)
    den = each(lambda wi, a, n, w: wi * jnp.sum(a.astype(F32) * n, axis=1, keepdims=True)
               + jnp.sum(w, axis=1, keepdims=True), w_inter, q, n_prev, w_intra)
    for ch, nu, de, mt in zip(chains, num, den, m_t):
        ch['o'][...] = (nu / jnp.maximum(jnp.abs(de), jnp.exp(-mt))).astype(ch['o'].dtype)
    b_end = each(lambda c: jnp.sum(c, axis=0, keepdims=True), lf_c)
    log_w = each(lambda be, bc, g: be - bc + g, b_end, b_c, ig_c)
    m_new = each(lambda be, m, lw: jnp.maximum(be + m, jnp.max(lw, axis=0, keepdims=True)), b_end, m_prev, log_w)
    w = each(lambda lw, mn: jnp.exp(lw - mn), log_w, m_new)
    decay = each(lambda be, m, mn: jnp.exp(be + m - mn), b_end, m_prev, m_new)
    upd = each(lambda ww, vv, kk: lax.dot_general((ww * vv.astype(F32)).astype(BF16), kk, (((0,), (0,)), ((), ())),
                                                  preferred_element_type=F32), w, v, k)
    for ch, dc, cp, u_, np_, ww, kk, mn in zip(chains, decay, c_prev, upd, n_prev, w, k, m_new):
        ch['c'][...] = dc * cp + u_
        ch['n'][...] = dc * np_ + jnp.sum(ww * kk.astype(F32), axis=0, keepdims=True)
        ch['m'][...] = mn


def _mlstm_kernel(qf_ref, kf_ref, vf_ref, gcf_ref, grf_ref, qb_ref, kb_ref, vb_ref, gcb_ref, grb_ref,
                  hf_ref, hb_ref, c_scr, n_scr, m_scr):
    @pl.when(pl.program_id(1) == 0)
    def _():
        c_scr[...] = jnp.zeros(c_scr.shape, F32)
        n_scr[...] = jnp.zeros(n_scr.shape, F32)
        m_scr[...] = jnp.zeros(m_scr.shape, F32)

    dh = MLSTM_DIM
    chains = []
    for i in range(gcf_ref.shape[0]):
        cols = slice(i * dh, (i + 1) * dh)
        for j, (q_ref, k_ref, v_ref, gc_ref, gr_ref, o_ref) in enumerate(
                ((qf_ref, kf_ref, vf_ref, gcf_ref, grf_ref, hf_ref), (qb_ref, kb_ref, vb_ref, gcb_ref, grb_ref, hb_ref))):
            gc = gc_ref[i]
            gr = gr_ref[i]
            s = 2 * i + j
            chains.append(dict(q=q_ref[:, cols], k=k_ref[:, cols], v=v_ref[:, cols],
                               ig_c=gc[:, 2 * j:2 * j + 1], fg_c=gc[:, 2 * j + 1:2 * j + 2],
                               ig_r=gr[2 * j:2 * j + 1, :], fg_r=gr[2 * j + 1:2 * j + 2, :],
                               c=c_scr.at[s], n=n_scr.at[s], m=m_scr.at[s], o=o_ref.at[:, cols], reverse=j == 1))
    _mlstm_chains(chains)


def _mlstm_scan(qk, u, gates, n_lat, n_ctx):
    s_pad = qk.shape[0]
    L, dh, nh, hp = MLSTM_CHUNK, MLSTM_DIM, MLSTM_HEADS, MLSTM_HEADS_PER_STEP
    nlc, ncc = n_lat // L, n_ctx // L
    steps = nlc + ncc
    ng = nh // hp
    g4 = gates.reshape(s_pad, 2, 2, nh).transpose(3, 0, 1, 2).reshape(nh, s_pad, 4)
    g4t = g4.transpose(0, 2, 1)

    def fwd(c):
        return jnp.where(c < ncc, nlc + c, c - ncc)

    def bwd(c):
        return jnp.where(c < ncc, nlc + ncc - 1 - c, nlc - 1 - (c - ncc))

    def specs(pos):
        return [pl.BlockSpec((L, hp * dh), lambda h, c: (pos(c), h)),
                pl.BlockSpec((L, hp * dh), lambda h, c: (pos(c), ng + h)),
                pl.BlockSpec((L, hp * dh), lambda h, c: (pos(c), 2 * ng + h)),
                pl.BlockSpec((hp, L, 4), lambda h, c: (h, pos(c), 0)),
                pl.BlockSpec((hp, 4, L), lambda h, c: (h, 0, pos(c)))]

    out_shape = jax.ShapeDtypeStruct((steps * L, nh * dh), BF16)
    return pl.pallas_call(
        _mlstm_kernel,
        grid=(ng, steps),
        in_specs=specs(fwd) + specs(bwd),
        out_specs=[pl.BlockSpec((L, hp * dh), lambda h, c: (fwd(c), h)),
                   pl.BlockSpec((L, hp * dh), lambda h, c: (bwd(c), h))],
        out_shape=[out_shape, out_shape],
        scratch_shapes=[pltpu.VMEM((2 * hp, dh, dh), F32), pltpu.VMEM((2 * hp, 1, dh), F32),
                        pltpu.VMEM((2 * hp, 1, 1), F32)],
        compiler_params=_cparams(("parallel", "arbitrary")),
        name="mlstm_scan",
    )(qk, qk, u, g4, g4t, qk, qk, u, g4, g4t)


def kernel(x, c, ctx, c_ctx, w_mod_0, b_mod_0, g_pre_mix_0, g_post_mix_0, w_in_0, lambda_q1_0, lambda_k1_0, lambda_q2_0, lambda_k2_0, g_subln_0, g_q_0, g_k_0, w_out_0, g_pre_ffn_0, g_post_ffn_0, w_router_0, w_gate_0, w_up_0, w_down_0, w_mod_1, b_mod_1, g_pre_mix_1, g_post_mix_1, w_in_1, b_gate_1, w_conv_1, b_conv_1, g_head_1, w_out_1, g_pre_ffn_1, g_post_ffn_1, w_router_1, w_gate_1, w_up_1, w_down_1):
    b, n_lat, d = x.shape
    n_ctx = ctx.shape[1]
    assert b == 1 and n_lat % ROW_TILE == 0 and n_ctx % MLSTM_CHUNK == 0 and n_ctx <= ROW_TILE
    s_valid = n_lat + n_ctx
    s_pad = n_lat + ROW_TILE
    n_lat_tiles = n_lat // ROW_TILE

    ctx_pad = jnp.concatenate([ctx[0], jnp.zeros((s_pad - s_valid, d), F32)], axis=0)
    cc = jnp.concatenate([c, c_ctx[None, :], jnp.zeros((6, d), F32)], axis=0)

    mod0 = _modulation(cc, w_mod_0, b_mod_0)
    tabs = _rope_tables(n_lat, n_ctx, s_pad)
    p0 = _proj0(x[0], ctx_pad, mod0, g_pre_mix_0, w_in_0.astype(BF16), tabs, g_q_0, g_k_0, n_lat_tiles)
    lam_init = 0.8 - 0.6 * math.exp(-0.3 * 0)
    lam4 = jnp.stack([lambda_q1_0, lambda_k1_0, lambda_q2_0, lambda_k2_0])
    o_a = _attn_diff(p0, lam4, g_subln_0, n_lat, s_valid, lam_init)
    o_b = _attn_gqa(p0, n_lat, s_valid)
    ka = o_a.shape[1]
    x_mid, hl, aff = _mixer_out(
        _out0_kernel, (o_a, o_b, ctx_pad),
        lambda tm: [pl.BlockSpec((tm, ka), lambda i: (i, 0)), pl.BlockSpec((tm, o_b.shape[1]), lambda i: (i, 0)),
                    pl.BlockSpec((tm, d), lambda i: (jnp.maximum(i - n_lat // tm, 0), 0))],
        x[0], mod0, w_out_0.astype(BF16), g_post_mix_0, g_pre_ffn_0, w_router_0, s_pad, n_lat // 256,
        lambda tm: [], "attn_out")
    p_moe0 = dict(w_gate=w_gate_0, w_up=w_up_0, w_down=w_down_0, g_post_ffn=g_post_ffn_0)
    xs = _moe(hl, aff, x_mid, mod0, p_moe0, n_lat, n_ctx, s_pad, True)

    mod1 = _modulation(cc, w_mod_1, b_mod_1)
    inner = MLSTM_HEADS * MLSTM_DIM
    u, gates = _proj1(xs, mod1, g_pre_mix_1, w_in_1.astype(BF16), 4 * inner, w_in_1[:, 4 * inner:], b_gate_1,
                      n_lat_tiles)
    qk = _conv_silu(u, w_conv_1, b_conv_1, 2 * inner, n_lat, n_ctx)
    hf, hb = _mlstm_scan(qk, u, gates, n_lat, n_ctx)
    og_blk = 3 * inner // inner
    x_mid, hl, aff = _mixer_out(
        _out1_kernel, (hf, hb, u, g_head_1.reshape(1, inner)),
        lambda tm: [pl.BlockSpec((tm, inner), lambda i: (i, 0)), pl.BlockSpec((tm, inner), lambda i: (i, 0)),
                    pl.BlockSpec((tm, inner), lambda i: (i, og_blk)), pl.BlockSpec((1, inner), lambda i: (0, 0))],
        xs, mod1, w_out_1.astype(BF16), g_post_mix_1, g_pre_ffn_1, w_router_1, n_lat, n_lat // 256,
        lambda tm: [pltpu.VMEM((tm, inner), BF16)], "mlstm_out")
    p_moe1 = dict(w_gate=w_gate_1, w_up=w_up_1, w_down=w_down_1, g_post_ffn=g_post_ffn_1)
    out = _moe(hl, aff, x_mid, mod1, p_moe1, n_lat, n_ctx, s_pad, False)
    return out[None]
```
